```python
import math
import jax, jax.numpy as jnp
from jax import lax
import numpy as np

D_MODEL = 1024
BATCH = 8
SEQ = 4096
DEPTH = 2

N_HEADS = 16
HEAD_DIM = D_MODEL // N_HEADS
ROPE_DIM = HEAD_DIM // 4
ROPE_THETA = 500000.0
ATTN_SCALE = HEAD_DIM ** -0.5
DILATED_PAIRS = ((128, 1), (512, 4), (2048, 16))
N_DIL = len(DILATED_PAIRS)
BAND_BLOCK = 128
MOBA_BLOCK = 256
MOBA_TOPK = 3
MOBA_QCHUNK = 128
D_FF = 2816
CONV_WIDTH = 3
RMS_EPS = 1e-6
N_LAYERS_A = (DEPTH + 1) // 2
N_LAYERS_B = DEPTH // 2

kernel_name = "hybrid_dilated_moba_convffn"


def rmsnorm(x, gain):
    xf = x.astype(jnp.float32)
    y = xf * lax.rsqrt(jnp.mean(xf * xf, axis=-1, keepdims=True) + RMS_EPS)
    return (y * gain.astype(jnp.float32)).astype(x.dtype)


def rope_tables(seq_len):
    pos = jnp.arange(seq_len, dtype=jnp.float32)
    inv_freq = ROPE_THETA ** (-jnp.arange(0, ROPE_DIM, 2, dtype=jnp.float32) / ROPE_DIM)
    ang = pos[:, None] * inv_freq[None, :]
    return jnp.cos(ang), jnp.sin(ang)


def apply_partial_rope(x, cos, sin):
    extra = x.ndim - 3
    c = cos.reshape((cos.shape[0],) + (1,) * extra + (cos.shape[1],)).astype(x.dtype)
    s = sin.reshape((sin.shape[0],) + (1,) * extra + (sin.shape[1],)).astype(x.dtype)
    half = ROPE_DIM // 2
    x1, x2, xp = x[..., :half], x[..., half:ROPE_DIM], x[..., ROPE_DIM:]
    return jnp.concatenate([x1 * c - x2 * s, x2 * c + x1 * s, xp], axis=-1)


def band_attention(q, k, v, dil, window):
    S, H, hd = q.shape
    reach = window // dil
    assert reach <= BAND_BLOCK
    span = dil * BAND_BLOCK
    s_pad = -(-S // span) * span
    L = s_pad // dil
    nb = L // BAND_BLOCK

    def to_res(t):
        t = jnp.pad(t, ((0, s_pad - S), (0, 0), (0, 0)))
        return t.reshape(L, dil, H, hd).transpose(1, 0, 2, 3).reshape(dil, nb, BAND_BLOCK, H, hd)

    qr, kr, vr = to_res(q), to_res(k), to_res(v)

    def with_prev(t):
        prev = jnp.concatenate([jnp.zeros_like(t[:, :1]), t[:, :-1]], axis=1)
        return jnp.concatenate([prev, t], axis=2)

    kc, vc = with_prev(kr), with_prev(vr)
    scores = jnp.einsum('rnqhd,rnkhd->rnhqk', qr, kc).astype(jnp.float32) * ATTN_SCALE
    qi = jnp.arange(BAND_BLOCK)
    kj = jnp.arange(2 * BAND_BLOCK)
    dist = qi[:, None] + BAND_BLOCK - kj[None, :]
    kpos = jnp.arange(nb)[:, None] * BAND_BLOCK - BAND_BLOCK + kj[None, :]
    mask = ((dist >= 0) & (dist <= reach))[None] & (kpos >= 0)[:, None, :]
    scores = jnp.where(mask[None, :, None], scores, -jnp.inf)
    m = scores.max(axis=-1)
    p = jnp.exp(scores - m[..., None])
    den = p.sum(axis=-1)
    num = jnp.einsum('rnhqk,rnkhd->rnqhd', p, vc.astype(jnp.float32))
    m_t = m.transpose(0, 1, 3, 2)
    den_t = den.transpose(0, 1, 3, 2)
    o = num / den_t[..., None]

    def from_res(t):
        t = t.reshape((dil, L) + t.shape[3:])
        t = jnp.swapaxes(t, 0, 1)
        return t.reshape((s_pad,) + t.shape[2:])[:S]

    return from_res(o), from_res(m_t), from_res(den_t)


def mixer_dilated(h, w_qkv, q_gain, k_gain, w_o, cos, sin):
    B, S, _ = h.shape
    qkv = (h @ w_qkv).reshape(B, S, 3, N_DIL, N_HEADS, HEAD_DIM)
    q = apply_partial_rope(rmsnorm(qkv[:, :, 0], q_gain[:, None, :]), cos, sin)
    k = apply_partial_rope(rmsnorm(qkv[:, :, 1], k_gain[:, None, :]), cos, sin)
    v = qkv[:, :, 2]

    def one_sequence(args):
        qb, kb, vb = args
        outs, maxes, dens = [], [], []
        for g, (window, dil) in enumerate(DILATED_PAIRS):
            o, m, d = band_attention(qb[:, g], kb[:, g], vb[:, g], dil, window)
            outs.append(o)
            maxes.append(m)
            dens.append(d)
        o = jnp.stack(outs)
        m = jnp.stack(maxes)
        d = jnp.stack(dens)
        wgt = d * jnp.exp(m - m.max(axis=0, keepdims=True))
        wgt = wgt / wgt.sum(axis=0, keepdims=True)
        return jnp.einsum('gsh,gshd->shd', wgt, o)

    attn = lax.map(one_sequence, (q, k, v))
    return attn.reshape(B, S, N_HEADS * HEAD_DIM).astype(h.dtype) @ w_o


def mixer_moba(h, w_qkv, q_gain, k_gain, w_o, cos, sin):
    B, S, _ = h.shape
    qkv = (h @ w_qkv).reshape(B, S, 3, N_HEADS, HEAD_DIM)
    q = apply_partial_rope(rmsnorm(qkv[:, :, 0], q_gain), cos, sin)
    k = apply_partial_rope(rmsnorm(qkv[:, :, 1], k_gain), cos, sin)
    v = qkv[:, :, 2]
    s_pad = -(-S // MOBA_BLOCK) * MOBA_BLOCK
    nblk = s_pad // MOBA_BLOCK
    n_chunk = s_pad // MOBA_QCHUNK
    topk = min(MOBA_TOPK, nblk)

    def to_bhsd(t):
        return jnp.pad(t, ((0, 0), (0, s_pad - S), (0, 0), (0, 0))).transpose(0, 2, 1, 3)

    q, k, v = to_bhsd(q), to_bhsd(k), to_bhsd(v)
    k_blocks = k.reshape(B, N_HEADS, nblk, MOBA_BLOCK, HEAD_DIM)
    v_blocks = v.reshape(B, N_HEADS, nblk, MOBA_BLOCK, HEAD_DIM)
    k_mean = k_blocks.astype(jnp.float32).mean(axis=3)
    gate = jnp.einsum('bhsd,bhnd->bhsn', q.astype(jnp.float32), k_mean)
    own_blk = jnp.arange(s_pad) // MOBA_BLOCK
    fully_past = jnp.arange(nblk)[None, :] < own_blk[:, None]
    gate = jnp.where(fully_past, gate, -jnp.inf)
    _, sel = lax.top_k(gate, topk)

    def to_chunks(t):
        t = t.reshape(B, N_HEADS, n_chunk, MOBA_QCHUNK, t.shape[-1])
        return t.transpose(0, 2, 1, 3, 4).reshape(B * n_chunk, N_HEADS, MOBA_QCHUNK, t.shape[-1])

    q_chunks = to_chunks(q)
    sel_chunks = to_chunks(sel)
    chunk_ids = jnp.arange(B * n_chunk, dtype=jnp.int32)
    n_sel = topk * MOBA_BLOCK

    def one_chunk(args):
        qc, selc, cid = args
        b = cid // n_chunk
        c = cid % n_chunk
        kb = k_blocks[b]
        vb = v_blocks[b]
        qpos = c * MOBA_QCHUNK + jnp.arange(MOBA_QCHUNK)
        ob = (c * MOBA_QCHUNK) // MOBA_BLOCK
        h_idx = jnp.arange(N_HEADS)[:, None, None]
        k_sel = kb[h_idx, selc]
        v_sel = vb[h_idx, selc]
        s_sel = jnp.einsum('hqd,hqnkd->hqnk', qc, k_sel).astype(jnp.float32) * ATTN_SCALE
        valid = jnp.arange(topk) < ob
        s_sel = jnp.where(valid[None, None, :, None], s_sel, -jnp.inf)
        s_sel = s_sel.reshape(N_HEADS, MOBA_QCHUNK, n_sel)
        k_own = lax.dynamic_index_in_dim(kb, ob, axis=1, keepdims=False)
        v_own = lax.dynamic_index_in_dim(vb, ob, axis=1, keepdims=False)
        s_own = jnp.einsum('hqd,hkd->hqk', qc, k_own).astype(jnp.float32) * ATTN_SCALE
        kpos = ob * MOBA_BLOCK + jnp.arange(MOBA_BLOCK)
        s_own = jnp.where((kpos[None, :] <= qpos[:, None])[None], s_own, -jnp.inf)
        p = jax.nn.softmax(jnp.concatenate([s_sel, s_own], axis=-1), axis=-1).astype(v_sel.dtype)
        o = jnp.einsum('hqn,hqnd->hqd', p[..., :n_sel],
                       v_sel.reshape(N_HEADS, MOBA_QCHUNK, n_sel, HEAD_DIM))
        return o + jnp.einsum('hqk,hkd->hqd', p[..., n_sel:], v_own)

    out = lax.map(one_chunk, (q_chunks, sel_chunks, chunk_ids))
    out = out.reshape(B, n_chunk, N_HEADS, MOBA_QCHUNK, HEAD_DIM).transpose(0, 1, 3, 2, 4)
    out = out.reshape(B, s_pad, N_HEADS * HEAD_DIM)[:, :S]
    return out.astype(h.dtype) @ w_o


def conv_ffn(h, w_up, conv_w, conv_b, w_down):
    S = h.shape[1]
    u = h @ w_up
    up = jnp.pad(u, ((0, 0), (CONV_WIDTH - 1, 0), (0, 0)))
    uc = conv_b + sum(conv_w[j] * up[:, j:j + S] for j in range(CONV_WIDTH))
    gate, val = uc[..., :D_FF], uc[..., D_FF:]
    return (jax.nn.silu(gate) * val) @ w_down


def setup_inputs(seed: int = 0) -> dict:
    key = jax.random.key(seed)
    ks = jax.random.split(key, 16)
    hd_all = N_HEADS * HEAD_DIM

    def w(k, shape, fan_in):
        return jax.random.normal(k, shape, jnp.float32) * fan_in ** -0.5

    def gain(k, shape):
        return 1.0 + 0.1 * jax.random.normal(k, shape, jnp.float32)

    return {
        'x': jax.random.normal(ks[0], (BATCH, SEQ, D_MODEL), jnp.float32),
        'attn_norm': gain(ks[1], (DEPTH, D_MODEL)),
        'a_w_qkv': w(ks[2], (N_LAYERS_A, D_MODEL, 3 * N_DIL * hd_all), D_MODEL),
        'a_q_norm': gain(ks[3], (N_LAYERS_A, N_DIL, HEAD_DIM)),
        'a_k_norm': gain(ks[4], (N_LAYERS_A, N_DIL, HEAD_DIM)),
        'a_w_o': w(ks[5], (N_LAYERS_A, hd_all, D_MODEL), hd_all),
        'b_w_qkv': w(ks[6], (N_LAYERS_B, D_MODEL, 3 * hd_all), D_MODEL),
        'b_q_norm': gain(ks[7], (N_LAYERS_B, HEAD_DIM)),
        'b_k_norm': gain(ks[8], (N_LAYERS_B, HEAD_DIM)),
        'b_w_o': w(ks[9], (N_LAYERS_B, hd_all, D_MODEL), hd_all),
        'ffn_norm': gain(ks[10], (DEPTH, D_MODEL)),
        'ffn_w_up': w(ks[11], (DEPTH, D_MODEL, 2 * D_FF), D_MODEL),
        'ffn_conv_w': w(ks[12], (DEPTH, CONV_WIDTH, 2 * D_FF), CONV_WIDTH),
        'ffn_conv_b': 0.01 * jax.random.normal(ks[13], (DEPTH, 2 * D_FF), jnp.float32),
        'ffn_w_down': w(ks[14], (DEPTH, D_FF, D_MODEL), D_FF),
    }


def reference(x, attn_norm, a_w_qkv, a_q_norm, a_k_norm, a_w_o,
              b_w_qkv, b_q_norm, b_k_norm, b_w_o,
              ffn_norm, ffn_w_up, ffn_conv_w, ffn_conv_b, ffn_w_down):
    cos, sin = rope_tables(x.shape[1])
    for i in range(DEPTH):
        hn = rmsnorm(x, attn_norm[i])
        if i % 2 == 0:
            j = i // 2
            x = x + mixer_dilated(hn, a_w_qkv[j], a_q_norm[j], a_k_norm[j], a_w_o[j], cos, sin)
        else:
            j = i // 2
            x = x + mixer_moba(hn, b_w_qkv[j], b_q_norm[j], b_k_norm[j], b_w_o[j], cos, sin)
        hn = rmsnorm(x, ffn_norm[i])
        x = x + conv_ffn(hn, ffn_w_up[i], ffn_conv_w[i], ffn_conv_b[i], ffn_w_down[i])
    return x
```

```python
import functools
import math

import jax
import jax.numpy as jnp
from jax import lax
from jax.experimental import pallas as pl
from jax.experimental.pallas import tpu as pltpu

N_HEADS = 16
HEAD_DIM = 64
ROPE_DIM = HEAD_DIM // 4
ROPE_THETA = 500000.0
ATTN_SCALE = HEAD_DIM ** -0.5
DILATED_PAIRS = ((128, 1), (512, 4), (2048, 16))
N_DIL = len(DILATED_PAIRS)
BAND_BLOCK = 128
MOBA_BLOCK = 256
MOBA_TOPK = 3
CONV_WIDTH = 3
RMS_EPS = 1e-6

LANES = 128
HEADS_PER_SLAB = LANES // HEAD_DIM
MASK_VALUE = -1e30
CONV_HALO = 8
VMEM_LIMIT = 56 * 1024 * 1024

F32 = jnp.float32
BF16 = jnp.bfloat16


def _params(semantics):
    return pltpu.CompilerParams(dimension_semantics=semantics, vmem_limit_bytes=VMEM_LIMIT)


def _proj_kernel(x_ref, g_ref, w_ref, cg_ref, cos_ref, sa_ref, sb_ref, bd_ref, o_ref, hn_ref,
                 *, n_qk_tiles, tn):
    j = pl.program_id(1)

    @pl.when(j == 0)
    def _():
        x = x_ref[...]
        ms = jnp.mean(x * x, axis=-1, keepdims=True)
        hn_ref[...] = (x * lax.rsqrt(ms + RMS_EPS) * g_ref[...]).astype(BF16)

    acc = jnp.dot(hn_ref[...], w_ref[...], preferred_element_type=F32)

    @pl.when(j < n_qk_tiles)
    def _():
        bd = bd_ref[...]
        for c in range(tn // LANES):
            sl = slice(c * LANES, (c + 1) * LANES)
            a = acc[:, sl]
            a2 = a * a
            hi = a2.astype(BF16)
            lo = (a2 - hi.astype(F32)).astype(BF16)
            ss = (jnp.dot(hi, bd, preferred_element_type=F32)
                  + jnp.dot(lo, bd, preferred_element_type=F32))
            y = a * lax.rsqrt(ss * (1.0 / HEAD_DIM) + RMS_EPS) * cg_ref[:, sl]
            half = ROPE_DIM // 2
            y = (y * cos_ref[...]
                 + pltpu.roll(y, half, 1) * sa_ref[...]
                 + pltpu.roll(y, LANES - half, 1) * sb_ref[...])
            o_ref[:, sl] = y.astype(o_ref.dtype)

    @pl.when(j >= n_qk_tiles)
    def _():
        o_ref[...] = acc.astype(o_ref.dtype)


def _norm_proj(x2, gain, w, colgain, rope, bd, seq, n_qk_cols, out_dtype, tm, tn):
    m, d = x2.shape
    n = w.shape[1]
    cos_t, sa_t, sb_t = rope
    tiles_per_seq = seq // tm
    rope_spec = pl.BlockSpec((tm, LANES), lambda i, j: (i % tiles_per_seq, 0))
    return pl.pallas_call(
        functools.partial(_proj_kernel, n_qk_tiles=n_qk_cols // tn, tn=tn),
        grid=(m // tm, n // tn),
        in_specs=[
            pl.BlockSpec((tm, d), lambda i, j: (i, 0)),
            pl.BlockSpec((1, d), lambda i, j: (0, 0)),
            pl.BlockSpec((d, tn), lambda i, j: (0, j)),
            pl.BlockSpec((1, tn), lambda i, j: (0, j)),
            rope_spec, rope_spec, rope_spec,
            pl.BlockSpec((LANES, LANES), lambda i, j: (0, 0)),
        ],
        out_specs=pl.BlockSpec((tm, tn), lambda i, j: (i, j)),
        out_shape=jax.ShapeDtypeStruct((m, n), out_dtype),
        scratch_shapes=[pltpu.VMEM((tm, d), BF16)],
        compiler_params=_params(("parallel", "arbitrary")),
        name="norm_qkv_proj",
    )(x2, gain, w, colgain, cos_t, sa_t, sb_t, bd)


def _dilated_kernel(q0, q1, q2, k0, k1, k2, v0, v1, v2, o_ref, og_ref, lg_ref, *, tq):
    t = pl.program_id(2)
    q_refs, k_refs, v_refs = (q0, q1, q2), (k0, k1, k2), (v0, v1, v2)
    blk = BAND_BLOCK

    lane = lax.broadcasted_iota(jnp.int32, (blk, LANES), 1)
    first_head = lane < HEAD_DIM
    qi = lax.broadcasted_iota(jnp.int32, (2 * blk, 2 * blk), 0) % blk
    kj = lax.broadcasted_iota(jnp.int32, (2 * blk, 2 * blk), 1)
    ones_rhs = jnp.ones((2 * blk, LANES), BF16)

    for g, (window, dil) in enumerate(DILATED_PAIRS):
        reach = window // dil
        dist = qi + blk - kj
        band = (dist >= 0) & (dist <= reach)
        span = blk * dil
        assert dil & (dil - 1) == 0 and reach <= blk
        qg, kg, vg = q_refs[g], k_refs[g], v_refs[g]

        def body(idx, carry, dil=dil, span=span, band=band, qg=qg, kg=kg, vg=vg, g=g):
            u = lax.shift_right_logical(idx, int(math.log2(dil)))
            r = lax.bitwise_and(idx, dil - 1)
            qbase = u * span + r
            kbase = t * tq + qbase
            pbase = kbase - span
            first_key = jnp.where(pbase >= 0, 0, blk)
            pbase = jnp.maximum(pbase, 0)
            if dil == 1:
                qbase = pl.multiple_of(qbase, blk)
                kbase = pl.multiple_of(kbase, blk)
                pbase = pl.multiple_of(pbase, blk)
            q = qg[pl.ds(qbase, blk, stride=dil), :]
            kc = kg[pl.ds(kbase, blk, stride=dil), :]
            vc = vg[pl.ds(kbase, blk, stride=dil), :]
            kp = kg[pl.ds(pbase, blk, stride=dil), :]
            vp = vg[pl.ds(pbase, blk, stride=dil), :]
            zero = jnp.zeros_like(q)
            q2 = jnp.concatenate([jnp.where(first_head, q, zero),
                                  jnp.where(first_head, zero, q)], axis=0).astype(BF16)
            kcat = jnp.concatenate([kp, kc], axis=0).astype(BF16)
            vcat = jnp.concatenate([jnp.concatenate([vp, vc], axis=0).astype(BF16), ones_rhs], axis=1)
            s = lax.dot_general(q2, kcat, (((1,), (1,)), ((), ())), preferred_element_type=F32)
            s = jnp.where(band & (kj >= first_key), s, MASK_VALUE)
            m = jnp.max(s, axis=-1, keepdims=True)
            p = jnp.exp(s - m).astype(BF16)
            ov = jnp.dot(p, vcat, preferred_element_type=F32)
            num, den = ov[:, :LANES], ov[:, LANES:]
            o = num / den
            lse = m + jnp.log(den)
            o_pair = jnp.where(first_head, o[:blk], o[blk:])
            lse_pair = jnp.where(first_head, lse[:blk], lse[blk:])
            og_ref[g, pl.ds(qbase, blk, stride=dil), :] = o_pair
            lg_ref[g, pl.ds(qbase, blk, stride=dil), :] = lse_pair
            return carry

        lax.fori_loop(0, (tq // span) * dil, body, 0)

    rows = 2 * blk

    def merge(c, carry):
        sl = pl.ds(pl.multiple_of(c * rows, rows), rows)
        lses = [lg_ref[g, sl, :] for g in range(N_DIL)]
        mx = functools.reduce(jnp.maximum, lses)
        ws = [jnp.exp(l - mx) for l in lses]
        num = sum(w * og_ref[g, sl, :] for g, w in enumerate(ws))
        o_ref[sl, :] = (num / sum(ws)).astype(o_ref.dtype)
        return carry

    lax.fori_loop(0, tq // rows, merge, 0)


def _dilated_attention(qkv, batch, seq):
    span_max = BAND_BLOCK * max(d for _, d in DILATED_PAIRS)
    tq = span_max
    assert seq % tq == 0
    slabs = N_HEADS // HEADS_PER_SLAB
    group_slabs = N_HEADS * HEAD_DIM // LANES

    def col(which, g):
        return (which * N_DIL + g) * group_slabs

    q_specs = [pl.BlockSpec((None, tq, LANES), lambda b, s, t, g=g: (b, t, col(0, g) + s)) for g in range(N_DIL)]
    k_specs = [pl.BlockSpec((None, seq, LANES), lambda b, s, t, g=g: (b, 0, col(1, g) + s)) for g in range(N_DIL)]
    v_specs = [pl.BlockSpec((None, seq, LANES), lambda b, s, t, g=g: (b, 0, col(2, g) + s)) for g in range(N_DIL)]
    return pl.pallas_call(
        functools.partial(_dilated_kernel, tq=tq),
        grid=(batch, slabs, seq // tq),
        in_specs=q_specs + k_specs + v_specs,
        out_specs=pl.BlockSpec((None, tq, LANES), lambda b, s, t: (b, t, s)),
        out_shape=jax.ShapeDtypeStruct((batch, seq, N_HEADS * HEAD_DIM), BF16),
        scratch_shapes=[pltpu.VMEM((N_DIL, tq, LANES), F32), pltpu.VMEM((N_DIL, tq, LANES), F32)],
        compiler_params=_params(("parallel", "parallel", "arbitrary")),
        name="dilated_attention",
    )(*([qkv] * 9))


def _moba_kernel(q_ref, k_ref, v_ref, o_ref, km_ref, kmh_ref, kml_ref, m_ref, acc_ref, *, nblk):
    i = pl.program_id(2)
    blk = MOBA_BLOCK

    @pl.when(i == 0)
    def _():
        km_ref[...] = jnp.zeros_like(km_ref)
        for j in range(nblk):
            km_ref[j:j + 1, :] = jnp.sum(k_ref[j * blk:(j + 1) * blk, :].astype(F32), axis=0,
                                         keepdims=True) * (1.0 / blk)
        km = km_ref[...]
        hi = km.astype(BF16)
        kmh_ref[...] = hi
        kml_ref[...] = (km - hi.astype(F32)).astype(BF16)

    lane = lax.broadcasted_iota(jnp.int32, (blk, LANES), 1)
    first_head = lane < HEAD_DIM
    q = q_ref[...]
    zero = jnp.zeros_like(q)
    q2 = jnp.concatenate([jnp.where(first_head, q, zero), jnp.where(first_head, zero, q)], axis=0)

    nt = (((1,), (1,)), ((), ()))
    gate = (lax.dot_general(q2, kmh_ref[...], nt, preferred_element_type=F32)
            + lax.dot_general(q2, kml_ref[...], nt, preferred_element_type=F32))
    lane2 = lax.broadcasted_iota(jnp.int32, (2 * blk, LANES), 1)
    lane2f = lane2.astype(F32)
    remaining = lane2 < i
    sel = lane2 >= i
    for _ in range(MOBA_TOPK):
        gm = jnp.max(jnp.where(remaining, gate, -jnp.inf), axis=-1, keepdims=True)
        cand = remaining & (gate == gm)
        first = jnp.min(jnp.where(cand, lane2f, float(LANES)), axis=-1, keepdims=True)
        pick = lane2f == first
        sel = sel | pick
        remaining = remaining & jnp.logical_not(pick)
    bias = jnp.where(sel, 0.0, MASK_VALUE).astype(BF16)
    q_aug = jnp.concatenate([q2, bias], axis=1)

    ones_rhs = jnp.ones((blk, LANES), BF16)

    off = pl.multiple_of(i * blk, blk)
    k_own = k_ref[pl.ds(off, blk), :]
    v_own = jnp.concatenate([v_ref[pl.ds(off, blk), :], ones_rhs], axis=1)
    s = lax.dot_general(q2, k_own, nt, preferred_element_type=F32)
    qi = lax.broadcasted_iota(jnp.int32, (2 * blk, blk), 0) % blk
    kj = lax.broadcasted_iota(jnp.int32, (2 * blk, blk), 1)
    s = jnp.where(kj <= qi, s, MASK_VALUE)
    m0 = jnp.max(s, axis=-1, keepdims=True)
    p = jnp.exp(s - m0).astype(BF16)
    acc_ref[...] = jnp.dot(p, v_own, preferred_element_type=F32)
    m_ref[...] = jnp.broadcast_to(m0, m_ref.shape)

    def body(j, carry):
        joff = pl.multiple_of(j * blk, blk)
        onehot = (lax.broadcasted_iota(jnp.int32, (blk, LANES), 1) == j).astype(BF16)
        k_aug = jnp.concatenate([k_ref[pl.ds(joff, blk), :], onehot], axis=1)
        v_aug = jnp.concatenate([v_ref[pl.ds(joff, blk), :], ones_rhs], axis=1)
        sj = lax.dot_general(q_aug, k_aug, nt, preferred_element_type=F32)
        m_old = m_ref[...]
        m_new = jnp.maximum(m_old, jnp.max(sj, axis=-1, keepdims=True))
        alpha = jnp.exp(m_old - m_new)
        pj = jnp.exp(sj - jnp.concatenate([m_new, m_new], axis=1)).astype(BF16)
        acc_ref[...] = (acc_ref[...] * jnp.concatenate([alpha, alpha], axis=1)
                        + jnp.dot(pj, v_aug, preferred_element_type=F32))
        m_ref[...] = m_new
        return carry

    lax.fori_loop(0, i, body, 0)

    acc = acc_ref[...]
    o = acc[:, :LANES] / acc[:, LANES:]
    o_ref[...] = jnp.where(first_head, o[:blk], o[blk:]).astype(o_ref.dtype)


def _moba_attention(qkv, batch, seq):
    assert seq % MOBA_BLOCK == 0
    nblk = seq // MOBA_BLOCK
    assert nblk <= LANES
    slabs = N_HEADS * HEAD_DIM // LANES
    blk = MOBA_BLOCK
    return pl.pallas_call(
        functools.partial(_moba_kernel, nblk=nblk),
        grid=(batch, slabs, nblk),
        in_specs=[
            pl.BlockSpec((None, blk, LANES), lambda b, s, i: (b, i, s)),
            pl.BlockSpec((None, seq, LANES), lambda b, s, i: (b, 0, slabs + s)),
            pl.BlockSpec((None, seq, LANES), lambda b, s, i: (b, 0, 2 * slabs + s)),
        ],
        out_specs=pl.BlockSpec((None, blk, LANES), lambda b, s, i: (b, i, s)),
        out_shape=jax.ShapeDtypeStruct((batch, seq, N_HEADS * HEAD_DIM), BF16),
        scratch_shapes=[
            pltpu.VMEM((LANES, LANES), F32), pltpu.VMEM((LANES, LANES), BF16), pltpu.VMEM((LANES, LANES), BF16),
            pltpu.VMEM((2 * blk, LANES), F32), pltpu.VMEM((2 * blk, 2 * LANES), F32),
        ],
        compiler_params=_params(("parallel", "parallel", "arbitrary")),
        name="moba_attention",
    )(qkv, qkv, qkv)


def _out_proj_kernel(x_ref, a_ref, w_ref, o_ref):
    o_ref[...] = x_ref[...] + jnp.dot(a_ref[...], w_ref[...], preferred_element_type=F32)


def _out_proj_residual(x2, attn2, w_o, tm):
    m, d = x2.shape
    da = attn2.shape[1]
    return pl.pallas_call(
        _out_proj_kernel,
        grid=(m // tm,),
        in_specs=[
            pl.BlockSpec((tm, d), lambda i: (i, 0)),
            pl.BlockSpec((tm, da), lambda i: (i, 0)),
            pl.BlockSpec((da, d), lambda i: (0, 0)),
        ],
        out_specs=pl.BlockSpec((tm, d), lambda i: (i, 0)),
        out_shape=jax.ShapeDtypeStruct((m, d), F32),
        compiler_params=_params(("parallel",)),
        name="out_proj_residual",
    )(x2, attn2, w_o)


def _ffn_kernel(x_ref, xh_ref, g_ref, wg_ref, wv_ref, cwg_ref, cwv_ref, cbg_ref, cbv_ref, wd_ref,
                o_ref, hn_ref, acc_ref, *, tiles_per_seq):
    i = pl.program_id(0)
    f = pl.program_id(1)
    halo = CONV_HALO

    def rms(x):
        ms = jnp.mean(x * x, axis=-1, keepdims=True)
        return x * lax.rsqrt(ms + RMS_EPS) * g_ref[...]

    @pl.when(f == 0)
    def _():
        keep = (i % tiles_per_seq != 0).astype(F32)
        hn_ref[:halo, :] = (rms(xh_ref[...]) * keep).astype(BF16)
        hn_ref[halo:, :] = rms(x_ref[...]).astype(BF16)
        acc_ref[...] = jnp.zeros_like(acc_ref)

    hn = hn_ref[...]

    def conv(w_ref, cw_ref, cb_ref):
        u = jnp.dot(hn, w_ref[...], preferred_element_type=F32)
        cw = cw_ref[...]
        return (cb_ref[...]
                + cw[0:1, :] * pltpu.roll(u, 2, 0)[halo:]
                + cw[1:2, :] * pltpu.roll(u, 1, 0)[halo:]
                + cw[2:3, :] * u[halo:])

    gate = conv(wg_ref, cwg_ref, cbg_ref)
    val = conv(wv_ref, cwv_ref, cbv_ref)
    act = (gate * jax.nn.sigmoid(gate) * val).astype(BF16)
    acc_ref[...] += jnp.dot(act, wd_ref[...], preferred_element_type=F32)

    @pl.when(f == pl.num_programs(1) - 1)
    def _():
        o_ref[...] = x_ref[...] + acc_ref[...]


def _conv_ffn_residual(x2, gain, w_up, conv_w, conv_b, w_down, seq, tm, tf):
    m, d = x2.shape
    d_ff = w_down.shape[0]
    nf = d_ff // tf
    halo_blocks = tm // CONV_HALO
    return pl.pallas_call(
        functools.partial(_ffn_kernel, tiles_per_seq=seq // tm),
        grid=(m // tm, nf),
        in_specs=[
            pl.BlockSpec((tm, d), lambda i, f: (i, 0)),
            pl.BlockSpec((CONV_HALO, d), lambda i, f: (jnp.maximum(i * halo_blocks - 1, 0), 0)),
            pl.BlockSpec((1, d), lambda i, f: (0, 0)),
            pl.BlockSpec((d, tf), lambda i, f: (0, f)),
            pl.BlockSpec((d, tf), lambda i, f: (0, nf + f)),
            pl.BlockSpec((CONV_WIDTH, tf), lambda i, f: (0, f)),
            pl.BlockSpec((CONV_WIDTH, tf), lambda i, f: (0, nf + f)),
            pl.BlockSpec((1, tf), lambda i, f: (0, f)),
            pl.BlockSpec((1, tf), lambda i, f: (0, nf + f)),
            pl.BlockSpec((tf, d), lambda i, f: (f, 0)),
        ],
        out_specs=pl.BlockSpec((tm, d), lambda i, f: (i, 0)),
        out_shape=jax.ShapeDtypeStruct((m, d), F32),
        scratch_shapes=[pltpu.VMEM((CONV_HALO + tm, d), BF16), pltpu.VMEM((tm, d), F32)],
        compiler_params=_params(("parallel", "arbitrary")),
        name="conv_ffn_residual",
    )(x2, x2, gain, w_up, w_up, conv_w, conv_w, conv_b, conv_b, w_down)


def _rope_tables(seq):
    pos = jnp.arange(seq, dtype=F32)
    inv_freq = ROPE_THETA ** (-jnp.arange(0, ROPE_DIM, 2, dtype=F32) / ROPE_DIM)
    ang = pos[:, None] * inv_freq[None, :]
    cos, sin = jnp.cos(ang), jnp.sin(ang)
    half = ROPE_DIM // 2
    pad = jnp.zeros((seq, HEAD_DIM - ROPE_DIM), F32)
    zeros = jnp.zeros((seq, half), F32)
    cos_h = jnp.concatenate([cos, cos, pad + 1.0], axis=1)
    sa_h = jnp.concatenate([zeros, sin, pad], axis=1)
    sb_h = jnp.concatenate([-sin, zeros, pad], axis=1)
    rep = LANES // HEAD_DIM
    return tuple(jnp.tile(t, (1, rep)) for t in (cos_h, sa_h, sb_h))


def _block_diag_ones():
    r = jnp.arange(LANES) // HEAD_DIM
    return (r[:, None] == r[None, :]).astype(BF16)


def kernel(x, attn_norm, a_w_qkv, a_q_norm, a_k_norm, a_w_o, b_w_qkv, b_q_norm, b_k_norm, b_w_o,
           ffn_norm, ffn_w_up, ffn_conv_w, ffn_conv_b, ffn_w_down):
    batch, seq, d_model = x.shape
    depth = attn_norm.shape[0]
    hd_all = N_HEADS * HEAD_DIM
    rope = _rope_tables(seq)
    bd = _block_diag_ones()
    x2 = x.reshape(batch * seq, d_model)
    tm = 512

    for layer in range(depth):
        j = layer // 2
        gain = attn_norm[layer][None, :]
        if layer % 2 == 0:
            n_groups = N_DIL
            qg = jnp.tile(a_q_norm[j][:, None, :], (1, N_HEADS, 1)).reshape(-1) * ATTN_SCALE
            kg = jnp.tile(a_k_norm[j][:, None, :], (1, N_HEADS, 1)).reshape(-1)
            w_qkv, w_o = a_w_qkv[j], a_w_o[j]
        else:
            n_groups = 1
            qg = jnp.tile(b_q_norm[j], N_HEADS) * ATTN_SCALE
            kg = jnp.tile(b_k_norm[j], N_HEADS)
            w_qkv, w_o = b_w_qkv[j], b_w_o[j]
        n_qk = 2 * n_groups * hd_all
        colgain = jnp.concatenate([qg, kg, jnp.ones((n_groups * hd_all,), F32)])[None, :]
        if layer % 2 == 0:
            qkv = _norm_proj(x2, gain, w_qkv.astype(BF16), colgain, rope, bd, seq, n_qk, F32, tm, 512)
            attn = _dilated_attention(qkv.reshape(batch, seq, -1), batch, seq)
        else:
            qkv = _norm_proj(x2, gain, w_qkv.astype(BF16), colgain, rope, bd, seq, n_qk, BF16, tm, 512)
            attn = _moba_attention(qkv.reshape(batch, seq, -1), batch, seq)
        x2 = _out_proj_residual(x2, attn.reshape(batch * seq, hd_all), w_o.astype(BF16), tm)
        x2 = _conv_ffn_residual(x2, ffn_norm[layer][None, :], ffn_w_up[layer].astype(BF16),
                                ffn_conv_w[layer], ffn_conv_b[layer][None, :],
                                ffn_w_down[layer].astype(BF16), seq, tm, 256)
    return x2.reshape(batch, seq, d_model)
```

```python
import functools
import math

import jax
import jax.numpy as jnp
from jax import lax
from jax.experimental import pallas as pl
from jax.experimental.pallas import tpu as pltpu

N_HEADS = 16
HEAD_DIM = 64
ROPE_DIM = HEAD_DIM // 4
ROPE_THETA = 500000.0
ATTN_SCALE = HEAD_DIM ** -0.5
DILATED_PAIRS = ((128, 1), (512, 4), (2048, 16))
N_DIL = len(DILATED_PAIRS)
BAND_BLOCK = 128
MOBA_BLOCK = 256
MOBA_TOPK = 3
CONV_WIDTH = 3
RMS_EPS = 1e-6

LANES = 128
MXU_WIDTH = 256
HEADS_PER_SLAB = LANES // HEAD_DIM
MASK_VALUE = -1e30
BAND_UNROLL = 4
CONV_HALO = 8
VMEM_LIMIT = 56 * 1024 * 1024
LOG2_E = math.log2(math.e)

F32 = jnp.float32
BF16 = jnp.bfloat16
NT_DIMS = (((1,), (1,)), ((), ()))


def _params(semantics):
    return pltpu.CompilerParams(dimension_semantics=semantics, vmem_limit_bytes=VMEM_LIMIT)


def _rmsnorm_to_scratch(x_ref, g_ref, hn_ref):
    x = x_ref[...]
    ms = jnp.mean(x * x, axis=-1, keepdims=True)
    hn_ref[...] = (x * lax.rsqrt(ms + RMS_EPS) * g_ref[...]).astype(BF16)


def _qk_proj_kernel(x_ref, g_ref, w_ref, cg_ref, cos_ref, sa_ref, sb_ref, bd_ref, o_ref, hn_ref, *, tn):
    @pl.when(pl.program_id(1) == 0)
    def _():
        _rmsnorm_to_scratch(x_ref, g_ref, hn_ref)

    acc = jnp.dot(hn_ref[...], w_ref[...], preferred_element_type=F32)
    bd = bd_ref[...]
    half = ROPE_DIM // 2
    for c in range(tn // LANES):
        sl = slice(c * LANES, (c + 1) * LANES)
        a = acc[:, sl]
        ss = jnp.dot((a * a).astype(BF16), bd, preferred_element_type=F32)
        y = a * lax.rsqrt(ss * (1.0 / HEAD_DIM) + RMS_EPS) * cg_ref[:, sl]
        y = (y * cos_ref[...]
             + pltpu.roll(y, half, 1) * sa_ref[...]
             + pltpu.roll(y, LANES - half, 1) * sb_ref[...])
        o_ref[:, sl] = y.astype(o_ref.dtype)


def _v_proj_kernel(x_ref, g_ref, w_ref, o_ref, hn_ref):
    @pl.when(pl.program_id(1) == 0)
    def _():
        _rmsnorm_to_scratch(x_ref, g_ref, hn_ref)

    o_ref[...] = jnp.dot(hn_ref[...], w_ref[...], preferred_element_type=F32).astype(o_ref.dtype)


def _norm_qk_proj(x2, gain, w, colgain, rope, bd, seq, out_dtype, tm, tn):
    m, d = x2.shape
    n = w.shape[1]
    cos_t, sa_t, sb_t = rope
    tiles_per_seq = seq // tm
    rope_spec = pl.BlockSpec((tm, LANES), lambda i, j: (i % tiles_per_seq, 0))
    return pl.pallas_call(
        functools.partial(_qk_proj_kernel, tn=tn),
        grid=(m // tm, n // tn),
        in_specs=[
            pl.BlockSpec((tm, d), lambda i, j: (i, 0)),
            pl.BlockSpec((1, d), lambda i, j: (0, 0)),
            pl.BlockSpec((d, tn), lambda i, j: (0, j)),
            pl.BlockSpec((1, tn), lambda i, j: (0, j)),
            rope_spec, rope_spec, rope_spec,
            pl.BlockSpec((LANES, LANES), lambda i, j: (0, 0)),
        ],
        out_specs=pl.BlockSpec((tm, tn), lambda i, j: (i, j)),
        out_shape=jax.ShapeDtypeStruct((m, n), out_dtype),
        scratch_shapes=[pltpu.VMEM((tm, d), BF16)],
        compiler_params=_params(("parallel", "arbitrary")),
        name="norm_qk_proj",
    )(x2, gain, w, colgain, cos_t, sa_t, sb_t, bd)


def _norm_v_proj(x2, gain, w, out_dtype, tm, tn):
    m, d = x2.shape
    n = w.shape[1]
    return pl.pallas_call(
        _v_proj_kernel,
        grid=(m // tm, n // tn),
        in_specs=[
            pl.BlockSpec((tm, d), lambda i, j: (i, 0)),
            pl.BlockSpec((1, d), lambda i, j: (0, 0)),
            pl.BlockSpec((d, tn), lambda i, j: (0, j)),
        ],
        out_specs=pl.BlockSpec((tm, tn), lambda i, j: (i, j)),
        out_shape=jax.ShapeDtypeStruct((m, n), out_dtype),
        scratch_shapes=[pltpu.VMEM((tm, d), BF16)],
        compiler_params=_params(("parallel", "arbitrary")),
        name="norm_v_proj",
    )(x2, gain, w)


def _dilated_kernel(q0, q1, q2, k0, k1, k2, v0, v1, v2, o_ref, num_ref, m_ref, den_ref, mask_ref, *, tq):
    t = pl.program_id(2)
    q_refs, k_refs, v_refs = (q0, q1, q2), (k0, k1, k2), (v0, v1, v2)
    blk = BAND_BLOCK

    lane = lax.broadcasted_iota(jnp.int32, (blk, LANES), 1)
    first_head = lane < HEAD_DIM
    ones_rhs = jnp.ones((2 * blk, LANES), BF16)

    reaches = sorted({w // d for w, d in DILATED_PAIRS})
    qi = lax.broadcasted_iota(jnp.int32, (2 * blk, 2 * blk), 0) % blk
    kj = lax.broadcasted_iota(jnp.int32, (2 * blk, 2 * blk), 1)
    dist = qi + blk - kj
    for n, reach in enumerate(reaches):
        band = (dist >= 0) & (dist <= reach)
        mask_ref[n, 0] = jnp.where(band, 0.0, MASK_VALUE)
        mask_ref[n, 1] = jnp.where(band & (kj >= blk), 0.0, MASK_VALUE)

    for g, (window, dil) in enumerate(DILATED_PAIRS):
        reach = window // dil
        span = blk * dil
        assert dil & (dil - 1) == 0 and reach <= blk
        qg, kg, vg = q_refs[g], k_refs[g], v_refs[g]
        mask_g = reaches.index(reach)

        def body(idx, carry, dil=dil, span=span, mask_g=mask_g, qg=qg, kg=kg, vg=vg, g=g):
            u = lax.shift_right_logical(idx, int(math.log2(dil)))
            r = lax.bitwise_and(idx, dil - 1)
            qbase = u * span + r
            kbase = t * tq + qbase
            pbase = kbase - span
            no_prev = (pbase < 0).astype(jnp.int32)
            pbase = jnp.maximum(pbase, 0)
            if dil == 1:
                qbase = pl.multiple_of(qbase, blk)
                kbase = pl.multiple_of(kbase, blk)
                pbase = pl.multiple_of(pbase, blk)
            q = qg[pl.ds(qbase, blk, stride=dil), :]
            kc = kg[pl.ds(kbase, blk, stride=dil), :]
            vc = vg[pl.ds(kbase, blk, stride=dil), :]
            kp = kg[pl.ds(pbase, blk, stride=dil), :]
            vp = vg[pl.ds(pbase, blk, stride=dil), :]
            zero = jnp.zeros_like(q)
            q2 = jnp.concatenate([jnp.where(first_head, q, zero),
                                  jnp.where(first_head, zero, q)], axis=0).astype(BF16)
            kcat = jnp.concatenate([kp, kc], axis=0).astype(BF16)
            vcat = jnp.concatenate([jnp.concatenate([vp, vc], axis=0).astype(BF16), ones_rhs], axis=1)
            s = lax.dot_general(q2, kcat, NT_DIMS, preferred_element_type=F32)
            s = s + mask_ref[mask_g, no_prev]
            m = jnp.max(s, axis=-1, keepdims=True)
            p = jnp.exp2(s - m).astype(BF16)
            ov = jnp.dot(p, vcat, preferred_element_type=F32)
            mb = jnp.broadcast_to(m, (2 * blk, LANES))
            rows = pl.ds(qbase, blk, stride=dil)
            num_ref[g, rows, :] = jnp.where(first_head, ov[:blk, :LANES], ov[blk:, :LANES])
            den_ref[g, rows, :] = jnp.where(first_head, ov[:blk, LANES:], ov[blk:, LANES:])
            m_ref[g, rows, :] = jnp.where(first_head, mb[:blk], mb[blk:])
            return carry

        lax.fori_loop(0, (tq // span) * dil, body, 0, unroll=BAND_UNROLL)

    rows = 2 * blk

    def merge(c, carry):
        sl = pl.ds(pl.multiple_of(c * rows, rows), rows)
        ms = [m_ref[g, sl, :] for g in range(N_DIL)]
        mx = functools.reduce(jnp.maximum, ms)
        ws = [jnp.exp2(mg - mx) for mg in ms]
        num = sum(w * num_ref[g, sl, :] for g, w in enumerate(ws))
        den = sum(w * den_ref[g, sl, :] for g, w in enumerate(ws))
        o_ref[sl, :] = (num / den).astype(o_ref.dtype)
        return carry

    lax.fori_loop(0, tq // rows, merge, 0)


def _dilated_attention(qk, v, batch, seq):
    span_max = BAND_BLOCK * max(d for _, d in DILATED_PAIRS)
    tq = span_max
    assert seq % tq == 0
    slabs = N_HEADS // HEADS_PER_SLAB
    n_reach = len({w // d for w, d in DILATED_PAIRS})

    q_specs = [pl.BlockSpec((None, tq, LANES), lambda b, s, t, g=g: (b, t, g * slabs + s)) for g in range(N_DIL)]
    k_specs = [pl.BlockSpec((None, seq, LANES), lambda b, s, t, g=g: (b, 0, (N_DIL + g) * slabs + s))
               for g in range(N_DIL)]
    v_specs = [pl.BlockSpec((None, seq, LANES), lambda b, s, t, g=g: (b, 0, g * slabs + s)) for g in range(N_DIL)]
    stat = pltpu.VMEM((N_DIL, tq, LANES), F32)
    return pl.pallas_call(
        functools.partial(_dilated_kernel, tq=tq),
        grid=(batch, slabs, seq // tq),
        in_specs=q_specs + k_specs + v_specs,
        out_specs=pl.BlockSpec((None, tq, LANES), lambda b, s, t: (b, t, s)),
        out_shape=jax.ShapeDtypeStruct((batch, seq, N_HEADS * HEAD_DIM), BF16),
        scratch_shapes=[stat, stat, stat, pltpu.VMEM((n_reach, 2, 2 * BAND_BLOCK, 2 * BAND_BLOCK), F32)],
        compiler_params=_params(("parallel", "parallel", "arbitrary")),
        name="dilated_attention",
    )(*([qk] * 6 + [v] * 3))


def _moba_kernel(q_ref, k_ref, v_ref, oh_ref, o_ref, km_ref, kmh_ref, kml_ref, vaug_ref, s_ref, m_ref, acc_ref,
                 *, nblk):
    i = pl.program_id(2)
    blk = MOBA_BLOCK
    sup = 2 * blk
    lane = lax.broadcasted_iota(jnp.int32, (blk, LANES), 1)
    first_head = lane < HEAD_DIM

    @pl.when(i == 0)
    def _():
        km_ref[...] = jnp.zeros_like(km_ref)
        for j in range(nblk):
            rows = slice(j * blk, (j + 1) * blk)
            km_ref[j:j + 1, :] = jnp.sum(k_ref[rows, :].astype(F32), axis=0, keepdims=True) * (1.0 / blk)
            vj = v_ref[rows, :].astype(F32)
            vaug_ref[0, rows, :] = jnp.where(first_head, vj, 1.0).astype(BF16)
            vaug_ref[1, rows, :] = jnp.where(first_head, 1.0, vj).astype(BF16)
        km = km_ref[...]
        hi = km.astype(BF16)
        kmh_ref[...] = hi
        kml_ref[...] = (km - hi.astype(F32)).astype(BF16)

    q = q_ref[...]
    zero = jnp.zeros_like(q)
    q2 = jnp.concatenate([jnp.where(first_head, q, zero), jnp.where(first_head, zero, q)], axis=0)

    gate = (lax.dot_general(q2, kmh_ref[...], NT_DIMS, preferred_element_type=F32)
            + lax.dot_general(q2, kml_ref[...], NT_DIMS, preferred_element_type=F32))
    lane2 = lax.broadcasted_iota(jnp.int32, (2 * blk, LANES), 1)
    lane2f = lane2.astype(F32)
    remaining = lane2 < i
    sel = lane2 >= i
    for _ in range(MOBA_TOPK):
        gm = jnp.max(jnp.where(remaining, gate, -jnp.inf), axis=-1, keepdims=True)
        cand = remaining & (gate == gm)
        first = jnp.min(jnp.where(cand, lane2f, float(LANES)), axis=-1, keepdims=True)
        pick = lane2f == first
        sel = sel | pick
        remaining = remaining & jnp.logical_not(pick)
    bias = jnp.where(sel, 0.0, MASK_VALUE).astype(BF16)
    q_aug = jnp.concatenate([q2, bias], axis=1)

    def scores(base):
        rows = pl.ds(pl.multiple_of(base, sup), sup)
        k_aug = jnp.concatenate([k_ref[rows, :], oh_ref[rows, :]], axis=1)
        return lax.dot_general(q_aug, k_aug, NT_DIMS, preferred_element_type=F32)

    def accumulate(buf, tile):
        base = jnp.where(tile == 0, own_base, (tile - 1) * sup)
        s = s_ref[buf]
        m_old = m_ref[...]
        m_new = jnp.maximum(m_old, jnp.max(s, axis=-1, keepdims=True))
        alpha = jnp.exp2(m_old - m_new)
        p = jnp.exp2(s - jnp.concatenate([m_new] * (sup // LANES), axis=1)).astype(BF16)
        rows = pl.ds(pl.multiple_of(base, sup), sup)
        pv = jnp.concatenate([jnp.dot(p[:blk], vaug_ref[0, rows, :], preferred_element_type=F32),
                              jnp.dot(p[blk:], vaug_ref[1, rows, :], preferred_element_type=F32)], axis=0)
        acc_ref[...] = acc_ref[...] * alpha + pv
        m_ref[...] = m_new

    own_pair = lax.shift_right_logical(i, 1)
    own_base = own_pair * sup
    qi = lax.broadcasted_iota(jnp.int32, (2 * blk, sup), 0) % blk
    kj = lax.broadcasted_iota(jnp.int32, (2 * blk, sup), 1)
    s_ref[0] = jnp.where(kj - qi <= lax.bitwise_and(i, 1) * blk, scores(own_base), MASK_VALUE)
    m_ref[...] = jnp.full(m_ref.shape, MASK_VALUE, F32)
    acc_ref[...] = jnp.zeros_like(acc_ref)

    def advance(dst, src, tile):
        s_ref[dst] = scores((tile - 1) * sup)
        accumulate(src, tile - 1)

    def body(n, carry):
        advance(1, 0, 2 * n + 1)
        advance(0, 1, 2 * n + 2)
        return carry

    lax.fori_loop(0, lax.shift_right_logical(own_pair, 1), body, 0)

    @pl.when(lax.bitwise_and(own_pair, 1) == 1)
    def _():
        advance(1, 0, own_pair)
        accumulate(1, own_pair)

    @pl.when(lax.bitwise_and(own_pair, 1) == 0)
    def _():
        accumulate(0, own_pair)

    acc = acc_ref[...]
    o = acc / pltpu.roll(acc, HEAD_DIM, 1)
    o_ref[...] = jnp.where(first_head, o[:blk], o[blk:]).astype(o_ref.dtype)


def _moba_attention(qk, v, batch, seq):
    assert seq % (2 * MOBA_BLOCK) == 0
    nblk = seq // MOBA_BLOCK
    assert nblk <= LANES
    slabs = N_HEADS * HEAD_DIM // LANES
    blk = MOBA_BLOCK
    block_id = jnp.arange(seq, dtype=jnp.int32) // blk
    onehot = (block_id[:, None] == jnp.arange(LANES, dtype=jnp.int32)[None, :]).astype(BF16)
    return pl.pallas_call(
        functools.partial(_moba_kernel, nblk=nblk),
        grid=(batch, slabs, nblk),
        in_specs=[
            pl.BlockSpec((None, blk, LANES), lambda b, s, i: (b, i, s)),
            pl.BlockSpec((None, seq, LANES), lambda b, s, i: (b, 0, slabs + s)),
            pl.BlockSpec((None, seq, LANES), lambda b, s, i: (b, 0, s)),
            pl.BlockSpec((seq, LANES), lambda b, s, i: (0, 0)),
        ],
        out_specs=pl.BlockSpec((None, blk, LANES), lambda b, s, i: (b, i, s)),
        out_shape=jax.ShapeDtypeStruct((batch, seq, N_HEADS * HEAD_DIM), BF16),
        scratch_shapes=[
            pltpu.VMEM((LANES, LANES), F32), pltpu.VMEM((LANES, LANES), BF16), pltpu.VMEM((LANES, LANES), BF16),
            pltpu.VMEM((HEADS_PER_SLAB, seq, LANES), BF16),
            pltpu.VMEM((2, 2 * blk, 2 * blk), F32),
            pltpu.VMEM((2 * blk, LANES), F32), pltpu.VMEM((2 * blk, LANES), F32),
        ],
        compiler_params=_params(("parallel", "parallel", "arbitrary")),
        name="moba_attention",
    )(qk, qk, v, onehot)


def _out_proj_kernel(x_ref, a_ref, w_ref, o_ref):
    o_ref[...] = x_ref[...] + jnp.dot(a_ref[...], w_ref[...], preferred_element_type=F32)


def _out_proj_residual(x2, attn2, w_o, tm):
    m, d = x2.shape
    da = attn2.shape[1]
    return pl.pallas_call(
        _out_proj_kernel,
        grid=(m // tm,),
        in_specs=[
            pl.BlockSpec((tm, d), lambda i: (i, 0)),
            pl.BlockSpec((tm, da), lambda i: (i, 0)),
            pl.BlockSpec((da, d), lambda i: (0, 0)),
        ],
        out_specs=pl.BlockSpec((tm, d), lambda i: (i, 0)),
        out_shape=jax.ShapeDtypeStruct((m, d), F32),
        compiler_params=_params(("parallel",)),
        name="out_proj_residual",
    )(x2, attn2, w_o)


def _ffn_kernel(x_ref, xh_ref, g_ref, wg_ref, wv_ref, cwg_ref, cwv_ref, cbg_ref, cbv_ref, wd_ref,
                o_ref, hn_ref, acc_ref, *, tiles_per_seq):
    i = pl.program_id(0)
    f = pl.program_id(1)
    halo = CONV_HALO

    def rms(x):
        ms = jnp.mean(x * x, axis=-1, keepdims=True)
        return x * lax.rsqrt(ms + RMS_EPS) * g_ref[...]

    @pl.when(f == 0)
    def _():
        keep = (i % tiles_per_seq != 0).astype(F32)
        hn_ref[:halo, :] = (rms(xh_ref[...]) * keep).astype(BF16)
        hn_ref[halo:, :] = rms(x_ref[...]).astype(BF16)
        acc_ref[...] = jnp.zeros_like(acc_ref)

    hn = hn_ref[...]

    def conv(w_ref, cw_ref, cb_ref):
        u = jnp.dot(hn, w_ref[...], preferred_element_type=F32)
        cw = cw_ref[...]
        return (cb_ref[...]
                + cw[0:1, :] * pltpu.roll(u, 2, 0)[halo:]
                + cw[1:2, :] * pltpu.roll(u, 1, 0)[halo:]
                + cw[2:3, :] * u[halo:])

    gate = conv(wg_ref, cwg_ref, cbg_ref)
    val = conv(wv_ref, cwv_ref, cbv_ref)
    act = (gate * jax.nn.sigmoid(gate) * val).astype(BF16)
    acc_ref[...] += jnp.dot(act, wd_ref[...], preferred_element_type=F32)

    @pl.when(f == pl.num_programs(1) - 1)
    def _():
        o_ref[...] = x_ref[...] + acc_ref[...]


def _conv_ffn_residual(x2, gain, w_up, conv_w, conv_b, w_down, seq, tm, tf):
    m, d = x2.shape
    d_ff = w_down.shape[0]
    nf = d_ff // tf
    halo_blocks = tm // CONV_HALO
    return pl.pallas_call(
        functools.partial(_ffn_kernel, tiles_per_seq=seq // tm),
        grid=(m // tm, nf),
        in_specs=[
            pl.BlockSpec((tm, d), lambda i, f: (i, 0)),
            pl.BlockSpec((CONV_HALO, d), lambda i, f: (jnp.maximum(i * halo_blocks - 1, 0), 0)),
            pl.BlockSpec((1, d), lambda i, f: (0, 0)),
            pl.BlockSpec((d, tf), lambda i, f: (0, f)),
            pl.BlockSpec((d, tf), lambda i, f: (0, nf + f)),
            pl.BlockSpec((CONV_WIDTH, tf), lambda i, f: (0, f)),
            pl.BlockSpec((CONV_WIDTH, tf), lambda i, f: (0, nf + f)),
            pl.BlockSpec((1, tf), lambda i, f: (0, f)),
            pl.BlockSpec((1, tf), lambda i, f: (0, nf + f)),
            pl.BlockSpec((tf, d), lambda i, f: (f, 0)),
        ],
        out_specs=pl.BlockSpec((tm, d), lambda i, f: (i, 0)),
        out_shape=jax.ShapeDtypeStruct((m, d), F32),
        scratch_shapes=[pltpu.VMEM((CONV_HALO + tm, d), BF16), pltpu.VMEM((tm, d), F32)],
        compiler_params=_params(("parallel", "arbitrary")),
        name="conv_ffn_residual",
    )(x2, x2, gain, w_up, w_up, conv_w, conv_w, conv_b, conv_b, w_down)


def _rope_tables(seq):
    pos = jnp.arange(seq, dtype=F32)
    inv_freq = ROPE_THETA ** (-jnp.arange(0, ROPE_DIM, 2, dtype=F32) / ROPE_DIM)
    ang = pos[:, None] * inv_freq[None, :]
    cos, sin = jnp.cos(ang), jnp.sin(ang)
    half = ROPE_DIM // 2
    pad = jnp.zeros((seq, HEAD_DIM - ROPE_DIM), F32)
    zeros = jnp.zeros((seq, half), F32)
    cos_h = jnp.concatenate([cos, cos, pad + 1.0], axis=1)
    sa_h = jnp.concatenate([zeros, sin, pad], axis=1)
    sb_h = jnp.concatenate([-sin, zeros, pad], axis=1)
    rep = LANES // HEAD_DIM
    return tuple(jnp.tile(t, (1, rep)) for t in (cos_h, sa_h, sb_h))


def _block_diag_ones():
    r = jnp.arange(LANES) // HEAD_DIM
    return (r[:, None] == r[None, :]).astype(BF16)


def kernel(x, attn_norm, a_w_qkv, a_q_norm, a_k_norm, a_w_o, b_w_qkv, b_q_norm, b_k_norm, b_w_o,
           ffn_norm, ffn_w_up, ffn_conv_w, ffn_conv_b, ffn_w_down):
    batch, seq, d_model = x.shape
    depth = attn_norm.shape[0]
    hd_all = N_HEADS * HEAD_DIM
    rope = _rope_tables(seq)
    bd = _block_diag_ones()
    x2 = x.reshape(batch * seq, d_model)
    tm = 512
    q_scale = ATTN_SCALE * LOG2_E

    for layer in range(depth):
        j = layer // 2
        gain = attn_norm[layer][None, :]
        dilated = layer % 2 == 0
        if dilated:
            n_groups = N_DIL
            qg = jnp.tile(a_q_norm[j][:, None, :], (1, N_HEADS, 1)).reshape(-1) * q_scale
            kg = jnp.tile(a_k_norm[j][:, None, :], (1, N_HEADS, 1)).reshape(-1)
            w_qkv, w_o = a_w_qkv[j], a_w_o[j]
        else:
            n_groups = 1
            qg = jnp.tile(b_q_norm[j], N_HEADS) * q_scale
            kg = jnp.tile(b_k_norm[j], N_HEADS)
            w_qkv, w_o = b_w_qkv[j], b_w_o[j]
        n_qk = 2 * n_groups * hd_all
        colgain = jnp.concatenate([qg, kg])[None, :]
        w_qkv = w_qkv.astype(BF16)
        act_dtype = F32 if dilated else BF16
        qk = _norm_qk_proj(x2, gain, w_qkv[:, :n_qk], colgain, rope, bd, seq, act_dtype, tm, 512)
        v = _norm_v_proj(x2, gain, w_qkv[:, n_qk:], act_dtype, tm, 512)
        qk, v = qk.reshape(batch, seq, -1), v.reshape(batch, seq, -1)
        attn = _dilated_attention(qk, v, batch, seq) if dilated else _moba_attention(qk, v, batch, seq)
        x2 = _out_proj_residual(x2, attn.reshape(batch * seq, hd_all), w_o.astype(BF16), tm)
        x2 = _conv_ffn_residual(x2, ffn_norm[layer][None, :], ffn_w_up[layer].astype(BF16),
                                ffn_conv_w[layer], ffn_conv_b[layer][None, :],
                                ffn_w_down[layer].astype(BF16), seq, tm, 256)
    return x2.reshape(batch, seq, d_model)
```

```python
import functools
import math

import jax
import jax.numpy as jnp
from jax import lax
from jax.experimental import pallas as pl
from jax.experimental.pallas import tpu as pltpu

N_HEADS = 16
HEAD_DIM = 64
ROPE_DIM = HEAD_DIM // 4
ROPE_THETA = 500000.0
ATTN_SCALE = HEAD_DIM ** -0.5
DILATED_PAIRS = ((128, 1), (512, 4), (2048, 16))
N_DIL = len(DILATED_PAIRS)
BAND_BLOCK = 128
MOBA_BLOCK = 256
MOBA_TOPK = 3
CONV_WIDTH = 3
RMS_EPS = 1e-6

LANES = 128
MXU_WIDTH = 256
HEADS_PER_SLAB = LANES // HEAD_DIM
MASK_VALUE = -1e30
BAND_UNROLL = 4
CONV_HALO = 8
VMEM_LIMIT = 56 * 1024 * 1024
LOG2_E = math.log2(math.e)

F32 = jnp.float32
BF16 = jnp.bfloat16
NT_DIMS = (((1,), (1,)), ((), ()))


def _params(semantics):
    return pltpu.CompilerParams(dimension_semantics=semantics, vmem_limit_bytes=VMEM_LIMIT)


def _rmsnorm_bf16(x_ref, g_ref):
    x = x_ref[...]
    ms = jnp.mean(x * x, axis=-1, keepdims=True)
    return (x * lax.rsqrt(ms + RMS_EPS) * g_ref[...]).astype(BF16)


def _qk_proj_kernel(x_ref, g_ref, w_ref, cg_ref, cos_ref, sa_ref, sb_ref, bd_ref, o_ref, *, tn):
    hn = _rmsnorm_bf16(x_ref, g_ref)
    bd = bd_ref[...]
    half = ROPE_DIM // 2
    for lo in range(0, w_ref.shape[1], tn):
        acc = jnp.dot(hn, w_ref[:, lo:lo + tn], preferred_element_type=F32)
        for c in range(tn // LANES):
            sl = slice(lo + c * LANES, lo + (c + 1) * LANES)
            a = acc[:, c * LANES:(c + 1) * LANES]
            ss = jnp.dot((a * a).astype(BF16), bd, preferred_element_type=F32)
            y = a * lax.rsqrt(ss * (1.0 / HEAD_DIM) + RMS_EPS) * cg_ref[:, sl]
            y = (y * cos_ref[...]
                 + pltpu.roll(y, half, 1) * sa_ref[...]
                 + pltpu.roll(y, LANES - half, 1) * sb_ref[...])
            o_ref[:, sl] = y.astype(o_ref.dtype)


def _v_proj_kernel(x_ref, g_ref, w_ref, o_ref, *, tn):
    hn = _rmsnorm_bf16(x_ref, g_ref)
    for lo in range(0, w_ref.shape[1], tn):
        o_ref[:, lo:lo + tn] = jnp.dot(hn, w_ref[:, lo:lo + tn], preferred_element_type=F32).astype(o_ref.dtype)


def _norm_qk_proj(x2, gain, w, colgain, rope, bd, seq, out_dtype, tm, tn):
    m, d = x2.shape
    n = w.shape[1]
    cos_t, sa_t, sb_t = rope
    tiles_per_seq = seq // tm
    rope_spec = pl.BlockSpec((tm, LANES), lambda i: (i % tiles_per_seq, 0))
    return pl.pallas_call(
        functools.partial(_qk_proj_kernel, tn=tn),
        grid=(m // tm,),
        in_specs=[
            pl.BlockSpec((tm, d), lambda i: (i, 0)),
            pl.BlockSpec((1, d), lambda i: (0, 0)),
            pl.BlockSpec((d, n), lambda i: (0, 0), pipeline_mode=pl.Buffered(1)),
            pl.BlockSpec((1, n), lambda i: (0, 0)),
            rope_spec, rope_spec, rope_spec,
            pl.BlockSpec((LANES, LANES), lambda i: (0, 0)),
        ],
        out_specs=pl.BlockSpec((tm, n), lambda i: (i, 0)),
        out_shape=jax.ShapeDtypeStruct((m, n), out_dtype),
        compiler_params=_params(("parallel",)),
        name="norm_qk_proj",
    )(x2, gain, w, colgain, cos_t, sa_t, sb_t, bd)


def _norm_v_proj(x2, gain, w, out_dtype, tm, tn):
    m, d = x2.shape
    n = w.shape[1]
    return pl.pallas_call(
        functools.partial(_v_proj_kernel, tn=tn),
        grid=(m // tm,),
        in_specs=[
            pl.BlockSpec((tm, d), lambda i: (i, 0)),
            pl.BlockSpec((1, d), lambda i: (0, 0)),
            pl.BlockSpec((d, n), lambda i: (0, 0), pipeline_mode=pl.Buffered(1)),
        ],
        out_specs=pl.BlockSpec((tm, n), lambda i: (i, 0)),
        out_shape=jax.ShapeDtypeStruct((m, n), out_dtype),
        compiler_params=_params(("parallel",)),
        name="norm_v_proj",
    )(x2, gain, w)


def _dilated_kernel(q0, q1, q2, k0, k1, k2, v0, v1, v2, o_ref, num_ref, m_ref, den_ref, mask_ref, *, tq):
    t = pl.program_id(2)
    q_refs, k_refs, v_refs = (q0, q1, q2), (k0, k1, k2), (v0, v1, v2)
    blk = BAND_BLOCK

    lane = lax.broadcasted_iota(jnp.int32, (blk, LANES), 1)
    first_head = lane < HEAD_DIM
    ones_rhs = jnp.ones((2 * blk, LANES), BF16)

    reaches = sorted({w // d for w, d in DILATED_PAIRS})
    qi = lax.broadcasted_iota(jnp.int32, (2 * blk, 2 * blk), 0) % blk
    kj = lax.broadcasted_iota(jnp.int32, (2 * blk, 2 * blk), 1)
    dist = qi + blk - kj
    for n, reach in enumerate(reaches):
        band = (dist >= 0) & (dist <= reach)
        mask_ref[n, 0] = jnp.where(band, 0.0, MASK_VALUE)
        mask_ref[n, 1] = jnp.where(band & (kj >= blk), 0.0, MASK_VALUE)

    for g, (window, dil) in enumerate(DILATED_PAIRS):
        reach = window // dil
        span = blk * dil
        assert dil & (dil - 1) == 0 and reach <= blk
        qg, kg, vg = q_refs[g], k_refs[g], v_refs[g]
        mask_g = reaches.index(reach)

        def body(idx, carry, dil=dil, span=span, mask_g=mask_g, qg=qg, kg=kg, vg=vg, g=g):
            u = lax.shift_right_logical(idx, int(math.log2(dil)))
            r = lax.bitwise_and(idx, dil - 1)
            qbase = u * span + r
            kbase = t * tq + qbase
            pbase = kbase - span
            no_prev = (pbase < 0).astype(jnp.int32)
            pbase = jnp.maximum(pbase, 0)
            if dil == 1:
                qbase = pl.multiple_of(qbase, blk)
                kbase = pl.multiple_of(kbase, blk)
                pbase = pl.multiple_of(pbase, blk)
            q = qg[pl.ds(qbase, blk, stride=dil), :]
            kc = kg[pl.ds(kbase, blk, stride=dil), :]
            vc = vg[pl.ds(kbase, blk, stride=dil), :]
            kp = kg[pl.ds(pbase, blk, stride=dil), :]
            vp = vg[pl.ds(pbase, blk, stride=dil), :]
            zero = jnp.zeros_like(q)
            q2 = jnp.concatenate([jnp.where(first_head, q, zero),
                                  jnp.where(first_head, zero, q)], axis=0).astype(BF16)
            kcat = jnp.concatenate([kp, kc], axis=0).astype(BF16)
            vcat = jnp.concatenate([jnp.concatenate([vp, vc], axis=0).astype(BF16), ones_rhs], axis=1)
            s = lax.dot_general(q2, kcat, NT_DIMS, preferred_element_type=F32)
            s = s + mask_ref[mask_g, no_prev]
            m = jnp.max(s, axis=-1, keepdims=True)
            p = jnp.exp2(s - m).astype(BF16)
            ov = jnp.dot(p, vcat, preferred_element_type=F32)
            mb = jnp.broadcast_to(m, (2 * blk, LANES))
            rows = pl.ds(qbase, blk, stride=dil)
            num_ref[g, rows, :] = jnp.where(first_head, ov[:blk, :LANES], ov[blk:, :LANES])
            den_ref[g, rows, :] = jnp.where(first_head, ov[:blk, LANES:], ov[blk:, LANES:])
            m_ref[g, rows, :] = jnp.where(first_head, mb[:blk], mb[blk:])
            return carry

        lax.fori_loop(0, (tq // span) * dil, body, 0, unroll=BAND_UNROLL)

    rows = 2 * blk

    def merge(c, carry):
        sl = pl.ds(pl.multiple_of(c * rows, rows), rows)
        ms = [m_ref[g, sl, :] for g in range(N_DIL)]
        mx = functools.reduce(jnp.maximum, ms)
        ws = [jnp.exp2(mg - mx) for mg in ms]
        num = sum(w * num_ref[g, sl, :] for g, w in enumerate(ws))
        den = sum(w * den_ref[g, sl, :] for g, w in enumerate(ws))
        o_ref[sl, :] = (num / den).astype(o_ref.dtype)
        return carry

    lax.fori_loop(0, tq // rows, merge, 0)


def _dilated_attention(qk, v, batch, seq):
    span_max = BAND_BLOCK * max(d for _, d in DILATED_PAIRS)
    tq = span_max
    assert seq % tq == 0
    slabs = N_HEADS // HEADS_PER_SLAB
    n_reach = len({w // d for w, d in DILATED_PAIRS})

    q_specs = [pl.BlockSpec((None, tq, LANES), lambda b, s, t, g=g: (b, t, g * slabs + s)) for g in range(N_DIL)]
    k_specs = [pl.BlockSpec((None, seq, LANES), lambda b, s, t, g=g: (b, 0, (N_DIL + g) * slabs + s))
               for g in range(N_DIL)]
    v_specs = [pl.BlockSpec((None, seq, LANES), lambda b, s, t, g=g: (b, 0, g * slabs + s)) for g in range(N_DIL)]
    stat = pltpu.VMEM((N_DIL, tq, LANES), F32)
    return pl.pallas_call(
        functools.partial(_dilated_kernel, tq=tq),
        grid=(batch, slabs, seq // tq),
        in_specs=q_specs + k_specs + v_specs,
        out_specs=pl.BlockSpec((None, tq, LANES), lambda b, s, t: (b, t, s)),
        out_shape=jax.ShapeDtypeStruct((batch, seq, N_HEADS * HEAD_DIM), BF16),
        scratch_shapes=[stat, stat, stat, pltpu.VMEM((n_reach, 2, 2 * BAND_BLOCK, 2 * BAND_BLOCK), F32)],
        compiler_params=_params(("parallel", "parallel", "arbitrary")),
        name="dilated_attention",
    )(*([qk] * 6 + [v] * 3))


def _moba_kernel(q_ref, k_ref, v_ref, oh_ref, o_ref, km_ref, kmh_ref, kml_ref, vaug_ref, s_ref, m_ref, acc_ref,
                 *, nblk):
    i = pl.program_id(2)
    blk = MOBA_BLOCK
    sup = 2 * blk
    lane = lax.broadcasted_iota(jnp.int32, (blk, LANES), 1)
    first_head = lane < HEAD_DIM

    @pl.when(i == 0)
    def _():
        km_ref[...] = jnp.zeros_like(km_ref)
        for j in range(nblk):
            rows = slice(j * blk, (j + 1) * blk)
            km_ref[j:j + 1, :] = jnp.sum(k_ref[rows, :].astype(F32), axis=0, keepdims=True) * (1.0 / blk)
            vj = v_ref[rows, :].astype(F32)
            vaug_ref[0, rows, :] = jnp.where(first_head, vj, 1.0).astype(BF16)
            vaug_ref[1, rows, :] = jnp.where(first_head, 1.0, vj).astype(BF16)
        km = km_ref[...]
        hi = km.astype(BF16)
        kmh_ref[...] = hi
        kml_ref[...] = (km - hi.astype(F32)).astype(BF16)

    q = q_ref[...]
    zero = jnp.zeros_like(q)
    q2 = jnp.concatenate([jnp.where(first_head, q, zero), jnp.where(first_head, zero, q)], axis=0)

    gate = (lax.dot_general(q2, kmh_ref[...], NT_DIMS, preferred_element_type=F32)
            + lax.dot_general(q2, kml_ref[...], NT_DIMS, preferred_element_type=F32))
    lane2 = lax.broadcasted_iota(jnp.int32, (2 * blk, LANES), 1)
    lane2f = lane2.astype(F32)
    remaining = lane2 < i
    sel = lane2 >= i
    for _ in range(MOBA_TOPK):
        gm = jnp.max(jnp.where(remaining, gate, -jnp.inf), axis=-1, keepdims=True)
        cand = remaining & (gate == gm)
        first = jnp.min(jnp.where(cand, lane2f, float(LANES)), axis=-1, keepdims=True)
        pick = lane2f == first
        sel = sel | pick
        remaining = remaining & jnp.logical_not(pick)
    bias = jnp.where(sel, 0.0, MASK_VALUE).astype(BF16)
    q_aug = jnp.concatenate([q2, bias], axis=1)

    def scores(base):
        rows = pl.ds(pl.multiple_of(base, sup), sup)
        k_aug = jnp.concatenate([k_ref[rows, :], oh_ref[rows, :]], axis=1)
        return lax.dot_general(q_aug, k_aug, NT_DIMS, preferred_element_type=F32)

    def accumulate(buf, tile):
        base = jnp.where(tile == 0, own_base, (tile - 1) * sup)
        s = s_ref[buf]
        m_old = m_ref[...]
        m_new = jnp.maximum(m_old, jnp.max(s, axis=-1, keepdims=True))
        alpha = jnp.exp2(m_old - m_new)
        p = jnp.exp2(s - jnp.concatenate([m_new] * (sup // LANES), axis=1)).astype(BF16)
        rows = pl.ds(pl.multiple_of(base, sup), sup)
        pv = jnp.concatenate([jnp.dot(p[:blk], vaug_ref[0, rows, :], preferred_element_type=F32),
                              jnp.dot(p[blk:], vaug_ref[1, rows, :], preferred_element_type=F32)], axis=0)
        acc_ref[...] = acc_ref[...] * alpha + pv
        m_ref[...] = m_new

    own_pair = lax.shift_right_logical(i, 1)
    own_base = own_pair * sup
    qi = lax.broadcasted_iota(jnp.int32, (2 * blk, sup), 0) % blk
    kj = lax.broadcasted_iota(jnp.int32, (2 * blk, sup), 1)
    s_ref[0] = jnp.where(kj - qi <= lax.bitwise_and(i, 1) * blk, scores(own_base), MASK_VALUE)
    m_ref[...] = jnp.full(m_ref.shape, MASK_VALUE, F32)
    acc_ref[...] = jnp.zeros_like(acc_ref)

    def advance(dst, src, tile):
        s_ref[dst] = scores((tile - 1) * sup)
        accumulate(src, tile - 1)

    def body(n, carry):
        advance(1, 0, 2 * n + 1)
        advance(0, 1, 2 * n + 2)
        return carry

    lax.fori_loop(0, lax.shift_right_logical(own_pair, 1), body, 0)

    @pl.when(lax.bitwise_and(own_pair, 1) == 1)
    def _():
        advance(1, 0, own_pair)
        accumulate(1, own_pair)

    @pl.when(lax.bitwise_and(own_pair, 1) == 0)
    def _():
        accumulate(0, own_pair)

    acc = acc_ref[...]
    o = acc / pltpu.roll(acc, HEAD_DIM, 1)
    o_ref[...] = jnp.where(first_head, o[:blk], o[blk:]).astype(o_ref.dtype)


def _moba_attention(qk, v, batch, seq):
    assert seq % (2 * MOBA_BLOCK) == 0
    nblk = seq // MOBA_BLOCK
    assert nblk <= LANES
    slabs = N_HEADS * HEAD_DIM // LANES
    blk = MOBA_BLOCK
    block_id = jnp.arange(seq, dtype=jnp.int32) // blk
    onehot = (block_id[:, None] == jnp.arange(LANES, dtype=jnp.int32)[None, :]).astype(BF16)
    return pl.pallas_call(
        functools.partial(_moba_kernel, nblk=nblk),
        grid=(batch, slabs, nblk),
        in_specs=[
            pl.BlockSpec((None, blk, LANES), lambda b, s, i: (b, i, s)),
            pl.BlockSpec((None, seq, LANES), lambda b, s, i: (b, 0, slabs + s)),
            pl.BlockSpec((None, seq, LANES), lambda b, s, i: (b, 0, s)),
            pl.BlockSpec((seq, LANES), lambda b, s, i: (0, 0)),
        ],
        out_specs=pl.BlockSpec((None, blk, LANES), lambda b, s, i: (b, i, s)),
        out_shape=jax.ShapeDtypeStruct((batch, seq, N_HEADS * HEAD_DIM), BF16),
        scratch_shapes=[
            pltpu.VMEM((LANES, LANES), F32), pltpu.VMEM((LANES, LANES), BF16), pltpu.VMEM((LANES, LANES), BF16),
            pltpu.VMEM((HEADS_PER_SLAB, seq, LANES), BF16),
            pltpu.VMEM((2, 2 * blk, 2 * blk), F32),
            pltpu.VMEM((2 * blk, LANES), F32), pltpu.VMEM((2 * blk, LANES), F32),
        ],
        compiler_params=_params(("parallel", "parallel", "arbitrary")),
        name="moba_attention",
    )(qk, qk, v, onehot)


def _out_proj_kernel(x_ref, a_ref, w_ref, o_ref):
    o_ref[...] = x_ref[...] + jnp.dot(a_ref[...], w_ref[...], preferred_element_type=F32)


def _out_proj_residual(x2, attn2, w_o, tm):
    m, d = x2.shape
    da = attn2.shape[1]
    return pl.pallas_call(
        _out_proj_kernel,
        grid=(m // tm,),
        in_specs=[
            pl.BlockSpec((tm, d), lambda i: (i, 0)),
            pl.BlockSpec((tm, da), lambda i: (i, 0)),
            pl.BlockSpec((da, d), lambda i: (0, 0)),
        ],
        out_specs=pl.BlockSpec((tm, d), lambda i: (i, 0)),
        out_shape=jax.ShapeDtypeStruct((m, d), F32),
        compiler_params=_params(("parallel",)),
        name="out_proj_residual",
    )(x2, attn2, w_o)


def _ffn_kernel(x_ref, xh_ref, g_ref, wu_ref, cw_ref, cb_ref, wd_ref, o_ref, *, tiles_per_seq, tf, group):
    i = pl.program_id(0)
    halo = CONV_HALO
    d_ff = wd_ref.shape[0]

    def rms(x):
        ms = jnp.mean(x * x, axis=-1, keepdims=True)
        return x * lax.rsqrt(ms + RMS_EPS) * g_ref[...]

    keep = (i % tiles_per_seq != 0).astype(F32)
    hn = jnp.concatenate([(rms(xh_ref[...]) * keep).astype(BF16), rms(x_ref[...]).astype(BF16)], axis=0)

    def conv(col):
        u = jnp.dot(hn, wu_ref[:, col:col + tf], preferred_element_type=F32)
        cw = cw_ref[:, col:col + tf]
        return (cb_ref[:, col:col + tf]
                + cw[0:1, :] * pltpu.roll(u, 2, 0)[halo:]
                + cw[1:2, :] * pltpu.roll(u, 1, 0)[halo:]
                + cw[2:3, :] * u[halo:])

    acc = x_ref[...]
    for lo in range(0, d_ff, group * tf):
        hi = min(lo + group * tf, d_ff)
        acts = []
        for col in range(lo, hi, tf):
            gate = conv(col)
            val = conv(d_ff + col)
            acts.append((gate * jax.nn.sigmoid(gate) * val).astype(BF16))
        acc = acc + jnp.dot(jnp.concatenate(acts, axis=1), wd_ref[lo:hi, :], preferred_element_type=F32)
    o_ref[...] = acc


def _conv_ffn_residual(x2, gain, w_up, conv_w, conv_b, w_down, seq, tm, tf, group):
    m, d = x2.shape
    d_ff = w_down.shape[0]
    assert d_ff % tf == 0
    halo_blocks = tm // CONV_HALO
    resident = dict(pipeline_mode=pl.Buffered(1))
    return pl.pallas_call(
        functools.partial(_ffn_kernel, tiles_per_seq=seq // tm, tf=tf, group=group),
        grid=(m // tm,),
        in_specs=[
            pl.BlockSpec((tm, d), lambda i: (i, 0)),
            pl.BlockSpec((CONV_HALO, d), lambda i: (jnp.maximum(i * halo_blocks - 1, 0), 0)),
            pl.BlockSpec((1, d), lambda i: (0, 0)),
            pl.BlockSpec((d, 2 * d_ff), lambda i: (0, 0), **resident),
            pl.BlockSpec((CONV_WIDTH, 2 * d_ff), lambda i: (0, 0)),
            pl.BlockSpec((1, 2 * d_ff), lambda i: (0, 0)),
            pl.BlockSpec((d_ff, d), lambda i: (0, 0), **resident),
        ],
        out_specs=pl.BlockSpec((tm, d), lambda i: (i, 0)),
        out_shape=jax.ShapeDtypeStruct((m, d), F32),
        compiler_params=_params(("parallel",)),
        name="conv_ffn_residual",
    )(x2, x2, gain, w_up, conv_w, conv_b, w_down)


def _rope_tables(seq):
    pos = jnp.arange(seq, dtype=F32)
    inv_freq = ROPE_THETA ** (-jnp.arange(0, ROPE_DIM, 2, dtype=F32) / ROPE_DIM)
    ang = pos[:, None] * inv_freq[None, :]
    cos, sin = jnp.cos(ang), jnp.sin(ang)
    half = ROPE_DIM // 2
    pad = jnp.zeros((seq, HEAD_DIM - ROPE_DIM), F32)
    zeros = jnp.zeros((seq, half), F32)
    cos_h = jnp.concatenate([cos, cos, pad + 1.0], axis=1)
    sa_h = jnp.concatenate([zeros, sin, pad], axis=1)
    sb_h = jnp.concatenate([-sin, zeros, pad], axis=1)
    rep = LANES // HEAD_DIM
    return tuple(jnp.tile(t, (1, rep)) for t in (cos_h, sa_h, sb_h))


def _block_diag_ones():
    r = jnp.arange(LANES) // HEAD_DIM
    return (r[:, None] == r[None, :]).astype(BF16)


def kernel(x, attn_norm, a_w_qkv, a_q_norm, a_k_norm, a_w_o, b_w_qkv, b_q_norm, b_k_norm, b_w_o,
           ffn_norm, ffn_w_up, ffn_conv_w, ffn_conv_b, ffn_w_down):
    batch, seq, d_model = x.shape
    depth = attn_norm.shape[0]
    hd_all = N_HEADS * HEAD_DIM
    rope = _rope_tables(seq)
    bd = _block_diag_ones()
    x2 = x.reshape(batch * seq, d_model)
    tm = 512
    q_scale = ATTN_SCALE * LOG2_E

    for layer in range(depth):
        j = layer // 2
        gain = attn_norm[layer][None, :]
        dilated = layer % 2 == 0
        if dilated:
            n_groups = N_DIL
            qg = jnp.tile(a_q_norm[j][:, None, :], (1, N_HEADS, 1)).reshape(-1) * q_scale
            kg = jnp.tile(a_k_norm[j][:, None, :], (1, N_HEADS, 1)).reshape(-1)
            w_qkv, w_o = a_w_qkv[j], a_w_o[j]
        else:
            n_groups = 1
            qg = jnp.tile(b_q_norm[j], N_HEADS) * q_scale
            kg = jnp.tile(b_k_norm[j], N_HEADS)
            w_qkv, w_o = b_w_qkv[j], b_w_o[j]
        n_qk = 2 * n_groups * hd_all
        colgain = jnp.concatenate([qg, kg])[None, :]
        w_qkv = w_qkv.astype(BF16)
        act_dtype = F32 if dilated else BF16
        qk = _norm_qk_proj(x2, gain, w_qkv[:, :n_qk], colgain, rope, bd, seq, act_dtype, tm, 512)
        v = _norm_v_proj(x2, gain, w_qkv[:, n_qk:], act_dtype, tm, 512)
        qk, v = qk.reshape(batch, seq, -1), v.reshape(batch, seq, -1)
        attn = _dilated_attention(qk, v, batch, seq) if dilated else _moba_attention(qk, v, batch, seq)
        x2 = _out_proj_residual(x2, attn.reshape(batch * seq, hd_all), w_o.astype(BF16), tm)
        x2 = _conv_ffn_residual(x2, ffn_norm[layer][None, :], ffn_w_up[layer].astype(BF16),
                                ffn_conv_w[layer], ffn_conv_b[layer][None, :],
                                ffn_w_down[layer].astype(BF16), seq, 512, 256, 4)
    return x2.reshape(batch, seq, d_model)
```

```python
import functools
import math

import jax
import jax.numpy as jnp
from jax import lax
from jax.experimental import pallas as pl
from jax.experimental.pallas import tpu as pltpu

N_HEADS = 16
HEAD_DIM = 64
ROPE_DIM = HEAD_DIM // 4
ROPE_THETA = 500000.0
ATTN_SCALE = HEAD_DIM ** -0.5
DILATED_PAIRS = ((128, 1), (512, 4), (2048, 16))
N_DIL = len(DILATED_PAIRS)
BAND_BLOCK = 128
MOBA_BLOCK = 256
MOBA_TOPK = 3
CONV_WIDTH = 3
RMS_EPS = 1e-6

LANES = 128
BF16_SUBLANES = 16
MXU_WIDTH = 256
HEADS_PER_SLAB = LANES // HEAD_DIM
MASK_VALUE = -1e30
BAND_UNROLL = 16
CONV_HALO = 8
VMEM_LIMIT = 56 * 1024 * 1024
LOG2_E = math.log2(math.e)

F32 = jnp.float32
BF16 = jnp.bfloat16
NT_DIMS = (((1,), (1,)), ((), ()))


def _params(semantics):
    return pltpu.CompilerParams(dimension_semantics=semantics, vmem_limit_bytes=VMEM_LIMIT)


def _rmsnorm_bf16(x_ref, g_ref):
    x = x_ref[...]
    ms = jnp.mean(x * x, axis=-1, keepdims=True)
    return (x * lax.rsqrt(ms + RMS_EPS) * g_ref[...]).astype(BF16)


def _qk_proj_kernel(x_ref, g_ref, w_ref, cg_ref, cos_ref, sa_ref, sb_ref, bd_ref, o_ref, *, tn):
    hn = _rmsnorm_bf16(x_ref, g_ref)
    bd = bd_ref[...]
    half = ROPE_DIM // 2
    for lo in range(0, w_ref.shape[1], tn):
        acc = jnp.dot(hn, w_ref[:, lo:lo + tn], preferred_element_type=F32)
        for c in range(tn // LANES):
            sl = slice(lo + c * LANES, lo + (c + 1) * LANES)
            a = acc[:, c * LANES:(c + 1) * LANES]
            ss = jnp.dot((a * a).astype(BF16), bd, preferred_element_type=F32)
            y = a * lax.rsqrt(ss * (1.0 / HEAD_DIM) + RMS_EPS) * cg_ref[:, sl]
            y = (y * cos_ref[...]
                 + pltpu.roll(y, half, 1) * sa_ref[...]
                 + pltpu.roll(y, LANES - half, 1) * sb_ref[...])
            o_ref[:, sl] = y.astype(o_ref.dtype)


def _v_proj_kernel(x_ref, g_ref, w_ref, o_ref, *, tn):
    hn = _rmsnorm_bf16(x_ref, g_ref)
    for lo in range(0, w_ref.shape[1], tn):
        o_ref[:, lo:lo + tn] = jnp.dot(hn, w_ref[:, lo:lo + tn], preferred_element_type=F32).astype(o_ref.dtype)


def _norm_qk_proj(x2, gain, w, colgain, rope, bd, seq, out_dtype, tm, tn):
    m, d = x2.shape
    n = w.shape[1]
    cos_t, sa_t, sb_t = rope
    tiles_per_seq = seq // tm
    rope_spec = pl.BlockSpec((tm, LANES), lambda i: (i % tiles_per_seq, 0))
    return pl.pallas_call(
        functools.partial(_qk_proj_kernel, tn=tn),
        grid=(m // tm,),
        in_specs=[
            pl.BlockSpec((tm, d), lambda i: (i, 0)),
            pl.BlockSpec((1, d), lambda i: (0, 0)),
            pl.BlockSpec((d, n), lambda i: (0, 0), pipeline_mode=pl.Buffered(1)),
            pl.BlockSpec((1, n), lambda i: (0, 0)),
            rope_spec, rope_spec, rope_spec,
            pl.BlockSpec((LANES, LANES), lambda i: (0, 0)),
        ],
        out_specs=pl.BlockSpec((tm, n), lambda i: (i, 0)),
        out_shape=jax.ShapeDtypeStruct((m, n), out_dtype),
        compiler_params=_params(("parallel",)),
        name="norm_qk_proj",
    )(x2, gain, w, colgain, cos_t, sa_t, sb_t, bd)


def _norm_v_proj(x2, gain, w, out_dtype, tm, tn):
    m, d = x2.shape
    n = w.shape[1]
    return pl.pallas_call(
        functools.partial(_v_proj_kernel, tn=tn),
        grid=(m // tm,),
        in_specs=[
            pl.BlockSpec((tm, d), lambda i: (i, 0)),
            pl.BlockSpec((1, d), lambda i: (0, 0)),
            pl.BlockSpec((d, n), lambda i: (0, 0), pipeline_mode=pl.Buffered(1)),
        ],
        out_specs=pl.BlockSpec((tm, n), lambda i: (i, 0)),
        out_shape=jax.ShapeDtypeStruct((m, n), out_dtype),
        compiler_params=_params(("parallel",)),
        name="norm_v_proj",
    )(x2, gain, w)


def _dilated_kernel(q0, q1, q2, k0, k1, k2, v0, v1, v2, o_ref, num_ref, m_ref, den_ref, mask_ref, *, tq):
    t = pl.program_id(2)
    q_refs, k_refs, v_refs = (q0, q1, q2), (k0, k1, k2), (v0, v1, v2)
    blk = BAND_BLOCK

    lane = lax.broadcasted_iota(jnp.int32, (blk, LANES), 1)
    first_head = lane < HEAD_DIM
    ones_rhs = jnp.ones((2 * blk, LANES), BF16)

    reaches = sorted({w // d for w, d in DILATED_PAIRS})
    qi = lax.broadcasted_iota(jnp.int32, (2 * blk, 2 * blk), 0) % blk
    kj = lax.broadcasted_iota(jnp.int32, (2 * blk, 2 * blk), 1)
    dist = qi + blk - kj
    for n, reach in enumerate(reaches):
        band = (dist >= 0) & (dist <= reach)
        mask_ref[n, 0] = jnp.where(band, 0.0, MASK_VALUE)
        mask_ref[n, 1] = jnp.where(band & (kj >= blk), 0.0, MASK_VALUE)

    for g, (window, dil) in enumerate(DILATED_PAIRS):
        reach = window // dil
        span = blk * dil
        assert dil & (dil - 1) == 0 and reach <= blk
        qg, kg, vg = q_refs[g], k_refs[g], v_refs[g]
        mask_g = reaches.index(reach)

        def body(idx, carry, dil=dil, span=span, mask_g=mask_g, qg=qg, kg=kg, vg=vg, g=g):
            u = lax.shift_right_logical(idx, int(math.log2(dil)))
            r = lax.bitwise_and(idx, dil - 1)
            qbase = u * span + r
            kbase = t * tq + qbase
            pbase = kbase - span
            no_prev = (pbase < 0).astype(jnp.int32)
            pbase = jnp.maximum(pbase, 0)
            if dil == 1:
                qbase = pl.multiple_of(qbase, blk)
                kbase = pl.multiple_of(kbase, blk)
                pbase = pl.multiple_of(pbase, blk)
            q = qg[pl.ds(qbase, blk, stride=dil), :]
            kc = kg[pl.ds(kbase, blk, stride=dil), :]
            vc = vg[pl.ds(kbase, blk, stride=dil), :]
            kp = kg[pl.ds(pbase, blk, stride=dil), :]
            vp = vg[pl.ds(pbase, blk, stride=dil), :]
            zero = jnp.zeros_like(q)
            q2 = jnp.concatenate([jnp.where(first_head, q, zero),
                                  jnp.where(first_head, zero, q)], axis=0).astype(BF16)
            kcat = jnp.concatenate([kp, kc], axis=0).astype(BF16)
            vcat = jnp.concatenate([jnp.concatenate([vp, vc], axis=0).astype(BF16), ones_rhs], axis=1)
            s = lax.dot_general(q2, kcat, NT_DIMS, preferred_element_type=F32)
            s = s + mask_ref[mask_g, no_prev]
            m = jnp.max(s, axis=-1, keepdims=True)
            p = jnp.exp2(s - m).astype(BF16)
            ov = jnp.dot(p, vcat, preferred_element_type=F32)
            mb = jnp.broadcast_to(m, (2 * blk, LANES))
            rows = pl.ds(qbase, blk, stride=dil)
            num_ref[g, rows, :] = jnp.where(first_head, ov[:blk, :LANES], ov[blk:, :LANES])
            den_ref[g, rows, :] = jnp.where(first_head, ov[:blk, LANES:], ov[blk:, LANES:])
            m_ref[g, rows, :] = jnp.where(first_head, mb[:blk], mb[blk:])
            return carry

        lax.fori_loop(0, (tq // span) * dil, body, 0, unroll=BAND_UNROLL)

    rows = 2 * blk

    def merge(c, carry):
        sl = pl.ds(pl.multiple_of(c * rows, rows), rows)
        ms = [m_ref[g, sl, :] for g in range(N_DIL)]
        mx = functools.reduce(jnp.maximum, ms)
        ws = [jnp.exp2(mg - mx) for mg in ms]
        num = sum(w * num_ref[g, sl, :] for g, w in enumerate(ws))
        den = sum(w * den_ref[g, sl, :] for g, w in enumerate(ws))
        o_ref[sl, :] = (num / den).astype(o_ref.dtype)
        return carry

    lax.fori_loop(0, tq // rows, merge, 0)


def _dilated_attention(qk, v, batch, seq):
    span_max = BAND_BLOCK * max(d for _, d in DILATED_PAIRS)
    tq = span_max
    assert seq % tq == 0
    slabs = N_HEADS // HEADS_PER_SLAB
    n_reach = len({w // d for w, d in DILATED_PAIRS})

    q_specs = [pl.BlockSpec((None, tq, LANES), lambda b, s, t, g=g: (b, t, g * slabs + s)) for g in range(N_DIL)]
    k_specs = [pl.BlockSpec((None, seq, LANES), lambda b, s, t, g=g: (b, 0, (N_DIL + g) * slabs + s))
               for g in range(N_DIL)]
    v_specs = [pl.BlockSpec((None, seq, LANES), lambda b, s, t, g=g: (b, 0, g * slabs + s)) for g in range(N_DIL)]
    stat = pltpu.VMEM((N_DIL, tq, LANES), F32)
    return pl.pallas_call(
        functools.partial(_dilated_kernel, tq=tq),
        grid=(batch, slabs, seq // tq),
        in_specs=q_specs + k_specs + v_specs,
        out_specs=pl.BlockSpec((None, tq, LANES), lambda b, s, t: (b, t, s)),
        out_shape=jax.ShapeDtypeStruct((batch, seq, N_HEADS * HEAD_DIM), BF16),
        scratch_shapes=[stat, stat, stat, pltpu.VMEM((n_reach, 2, 2 * BAND_BLOCK, 2 * BAND_BLOCK), F32)],
        compiler_params=_params(("parallel", "parallel", "arbitrary")),
        name="dilated_attention",
    )(*([qk] * 6 + [v] * 3))


def _moba_kernel(q_ref, k_ref, v_ref, oh_ref, o_ref, km_ref, kmhl_ref, vaug_ref, s_ref, m_ref, acc_ref, *, nblk):
    i = pl.program_id(2)
    blk = MOBA_BLOCK
    nrow = km_ref.shape[0]
    sup = 2 * blk
    lane = lax.broadcasted_iota(jnp.int32, (blk, LANES), 1)
    first_head = lane < HEAD_DIM

    @pl.when(i == 0)
    def _():
        km_ref[...] = jnp.zeros_like(km_ref)
        for j in range(nblk):
            rows = slice(j * blk, (j + 1) * blk)
            km_ref[j:j + 1, :] = jnp.sum(k_ref[rows, :].astype(F32), axis=0, keepdims=True) * (1.0 / blk)
            vj = v_ref[rows, :].astype(F32)
            vaug_ref[0, rows, :] = jnp.where(first_head, vj, 1.0).astype(BF16)
            vaug_ref[1, rows, :] = jnp.where(first_head, 1.0, vj).astype(BF16)
        km = km_ref[...]
        hi = km.astype(BF16)
        kmhl_ref[...] = jnp.concatenate([hi, (km - hi.astype(F32)).astype(BF16)], axis=0)

    q = q_ref[...]
    zero = jnp.zeros_like(q)
    q2 = jnp.concatenate([jnp.where(first_head, q, zero), jnp.where(first_head, zero, q)], axis=0)

    gate2 = lax.dot_general(kmhl_ref[...], q2, NT_DIMS, preferred_element_type=F32)
    gate = gate2[:nrow] + gate2[nrow:]
    row = lax.broadcasted_iota(jnp.int32, (nrow, 2 * blk), 0)
    rowf = row.astype(F32)
    remaining = row < i
    sel = row >= i
    for _ in range(MOBA_TOPK):
        gm = jnp.max(jnp.where(remaining, gate, -jnp.inf), axis=0, keepdims=True)
        cand = remaining & (gate == gm)
        first = jnp.min(jnp.where(cand, rowf, float(LANES)), axis=0, keepdims=True)
        pick = rowf == first
        sel = sel | pick
        remaining = remaining & jnp.logical_not(pick)
    bias_t = jnp.concatenate([jnp.where(sel, 0.0, MASK_VALUE),
                              jnp.zeros((LANES - nrow, 2 * blk), F32)], axis=0)
    q_aug = jnp.concatenate([q2, bias_t.T.astype(BF16)], axis=1)

    def scores(base):
        rows = pl.ds(pl.multiple_of(base, sup), sup)
        k_aug = jnp.concatenate([k_ref[rows, :], oh_ref[rows, :]], axis=1)
        return lax.dot_general(q_aug, k_aug, NT_DIMS, preferred_element_type=F32)

    def accumulate(buf, tile):
        base = jnp.where(tile == 0, own_base, (tile - 1) * sup)
        s = s_ref[buf]
        m_old = m_ref[...]
        m_new = jnp.maximum(m_old, jnp.max(s, axis=-1, keepdims=True))
        alpha = jnp.exp2(m_old - m_new)
        p = jnp.exp2(s - jnp.concatenate([m_new] * (sup // LANES), axis=1)).astype(BF16)
        rows = pl.ds(pl.multiple_of(base, sup), sup)
        pv = jnp.concatenate([jnp.dot(p[:blk], vaug_ref[0, rows, :], preferred_element_type=F32),
                              jnp.dot(p[blk:], vaug_ref[1, rows, :], preferred_element_type=F32)], axis=0)
        acc_ref[...] = acc_ref[...] * alpha + pv
        m_ref[...] = m_new

    own_pair = lax.shift_right_logical(i, 1)
    own_base = own_pair * sup
    qi = lax.broadcasted_iota(jnp.int32, (2 * blk, sup), 0) % blk
    kj = lax.broadcasted_iota(jnp.int32, (2 * blk, sup), 1)
    s_ref[0] = jnp.where(kj - qi <= lax.bitwise_and(i, 1) * blk, scores(own_base), MASK_VALUE)
    m_ref[...] = jnp.full(m_ref.shape, MASK_VALUE, F32)
    acc_ref[...] = jnp.zeros_like(acc_ref)

    def advance(dst, src, tile):
        s_ref[dst] = scores((tile - 1) * sup)
        accumulate(src, tile - 1)

    def body(n, carry):
        advance(1, 0, 2 * n + 1)
        advance(0, 1, 2 * n + 2)
        return carry

    lax.fori_loop(0, lax.shift_right_logical(own_pair, 1), body, 0)

    @pl.when(lax.bitwise_and(own_pair, 1) == 1)
    def _():
        advance(1, 0, own_pair)
        accumulate(1, own_pair)

    @pl.when(lax.bitwise_and(own_pair, 1) == 0)
    def _():
        accumulate(0, own_pair)

    acc = acc_ref[...]
    o = acc / pltpu.roll(acc, HEAD_DIM, 1)
    o_ref[...] = jnp.where(first_head, o[:blk], o[blk:]).astype(o_ref.dtype)


def _moba_attention(qk, v, batch, seq):
    assert seq % (2 * MOBA_BLOCK) == 0
    nblk = seq // MOBA_BLOCK
    assert nblk <= LANES
    nrow = -(-nblk // BF16_SUBLANES) * BF16_SUBLANES
    slabs = N_HEADS * HEAD_DIM // LANES
    blk = MOBA_BLOCK
    block_id = jnp.arange(seq, dtype=jnp.int32) // blk
    onehot = (block_id[:, None] == jnp.arange(LANES, dtype=jnp.int32)[None, :]).astype(BF16)
    return pl.pallas_call(
        functools.partial(_moba_kernel, nblk=nblk),
        grid=(batch, slabs, nblk),
        in_specs=[
            pl.BlockSpec((None, blk, LANES), lambda b, s, i: (b, i, s)),
            pl.BlockSpec((None, seq, LANES), lambda b, s, i: (b, 0, slabs + s)),
            pl.BlockSpec((None, seq, LANES), lambda b, s, i: (b, 0, s)),
            pl.BlockSpec((seq, LANES), lambda b, s, i: (0, 0)),
        ],
        out_specs=pl.BlockSpec((None, blk, LANES), lambda b, s, i: (b, i, s)),
        out_shape=jax.ShapeDtypeStruct((batch, seq, N_HEADS * HEAD_DIM), BF16),
        scratch_shapes=[
            pltpu.VMEM((nrow, LANES), F32), pltpu.VMEM((2 * nrow, LANES), BF16),
            pltpu.VMEM((HEADS_PER_SLAB, seq, LANES), BF16),
            pltpu.VMEM((2, 2 * blk, 2 * blk), F32),
            pltpu.VMEM((2 * blk, LANES), F32), pltpu.VMEM((2 * blk, LANES), F32),
        ],
        compiler_params=_params(("parallel", "parallel", "arbitrary")),
        name="moba_attention",
    )(qk, qk, v, onehot)


def _out_proj_kernel(x_ref, a_ref, w_ref, o_ref):
    o_ref[...] = x_ref[...] + jnp.dot(a_ref[...], w_ref[...], preferred_element_type=F32)


def _out_proj_residual(x2, attn2, w_o, tm):
    m, d = x2.shape
    da = attn2.shape[1]
    return pl.pallas_call(
        _out_proj_kernel,
        grid=(m // tm,),
        in_specs=[
            pl.BlockSpec((tm, d), lambda i: (i, 0)),
            pl.BlockSpec((tm, da), lambda i: (i, 0)),
            pl.BlockSpec((da, d), lambda i: (0, 0)),
        ],
        out_specs=pl.BlockSpec((tm, d), lambda i: (i, 0)),
        out_shape=jax.ShapeDtypeStruct((m, d), F32),
        compiler_params=_params(("parallel",)),
        name="out_proj_residual",
    )(x2, attn2, w_o)


def _ffn_kernel(x_ref, xh_ref, g_ref, wu_ref, cw_ref, cb_ref, wd_ref, o_ref, *, tiles_per_seq, tf, group):
    i = pl.program_id(0)
    halo = CONV_HALO
    d_ff = wd_ref.shape[0]

    def rms(x):
        ms = jnp.mean(x * x, axis=-1, keepdims=True)
        return x * lax.rsqrt(ms + RMS_EPS) * g_ref[...]

    keep = (i % tiles_per_seq != 0).astype(F32)
    hn = jnp.concatenate([(rms(xh_ref[...]) * keep).astype(BF16), rms(x_ref[...]).astype(BF16)], axis=0)

    def conv(col):
        u = jnp.dot(hn, wu_ref[:, col:col + tf], preferred_element_type=F32)
        cw = cw_ref[:, col:col + tf]
        return (cb_ref[:, col:col + tf]
                + cw[0:1, :] * pltpu.roll(u, 2, 0)[halo:]
                + cw[1:2, :] * pltpu.roll(u, 1, 0)[halo:]
                + cw[2:3, :] * u[halo:])

    acc = x_ref[...]
    for lo in range(0, d_ff, group * tf):
        hi = min(lo + group * tf, d_ff)
        acts = []
        for col in range(lo, hi, tf):
            gate = conv(col)
            val = conv(d_ff + col)
            acts.append((gate * jax.nn.sigmoid(gate) * val).astype(BF16))
        acc = acc + jnp.dot(jnp.concatenate(acts, axis=1), wd_ref[lo:hi, :], preferred_element_type=F32)
    o_ref[...] = acc


def _conv_ffn_residual(x2, gain, w_up, conv_w, conv_b, w_down, seq, tm, tf, group):
    m, d = x2.shape
    d_ff = w_down.shape[0]
    assert d_ff % tf == 0
    halo_blocks = tm // CONV_HALO
    resident = dict(pipeline_mode=pl.Buffered(1))
    return pl.pallas_call(
        functools.partial(_ffn_kernel, tiles_per_seq=seq // tm, tf=tf, group=group),
        grid=(m // tm,),
        in_specs=[
            pl.BlockSpec((tm, d), lambda i: (i, 0)),
            pl.BlockSpec((CONV_HALO, d), lambda i: (jnp.maximum(i * halo_blocks - 1, 0), 0)),
            pl.BlockSpec((1, d), lambda i: (0, 0)),
            pl.BlockSpec((d, 2 * d_ff), lambda i: (0, 0), **resident),
            pl.BlockSpec((CONV_WIDTH, 2 * d_ff), lambda i: (0, 0)),
            pl.BlockSpec((1, 2 * d_ff), lambda i: (0, 0)),
            pl.BlockSpec((d_ff, d), lambda i: (0, 0), **resident),
        ],
        out_specs=pl.BlockSpec((tm, d), lambda i: (i, 0)),
        out_shape=jax.ShapeDtypeStruct((m, d), F32),
        compiler_params=_params(("parallel",)),
        name="conv_ffn_residual",
    )(x2, x2, gain, w_up, conv_w, conv_b, w_down)


def _rope_tables(seq):
    pos = jnp.arange(seq, dtype=F32)
    inv_freq = ROPE_THETA ** (-jnp.arange(0, ROPE_DIM, 2, dtype=F32) / ROPE_DIM)
    ang = pos[:, None] * inv_freq[None, :]
    cos, sin = jnp.cos(ang), jnp.sin(ang)
    half = ROPE_DIM // 2
    pad = jnp.zeros((seq, HEAD_DIM - ROPE_DIM), F32)
    zeros = jnp.zeros((seq, half), F32)
    cos_h = jnp.concatenate([cos, cos, pad + 1.0], axis=1)
    sa_h = jnp.concatenate([zeros, sin, pad], axis=1)
    sb_h = jnp.concatenate([-sin, zeros, pad], axis=1)
    rep = LANES // HEAD_DIM
    return tuple(jnp.tile(t, (1, rep)) for t in (cos_h, sa_h, sb_h))


def _block_diag_ones():
    r = jnp.arange(LANES) // HEAD_DIM
    return (r[:, None] == r[None, :]).astype(BF16)


def kernel(x, attn_norm, a_w_qkv, a_q_norm, a_k_norm, a_w_o, b_w_qkv, b_q_norm, b_k_norm, b_w_o,
           ffn_norm, ffn_w_up, ffn_conv_w, ffn_conv_b, ffn_w_down):
    batch, seq, d_model = x.shape
    depth = attn_norm.shape[0]
    hd_all = N_HEADS * HEAD_DIM
    rope = _rope_tables(seq)
    bd = _block_diag_ones()
    x2 = x.reshape(batch * seq, d_model)
    tm = 512
    q_scale = ATTN_SCALE * LOG2_E

    for layer in range(depth):
        j = layer // 2
        gain = attn_norm[layer][None, :]
        dilated = layer % 2 == 0
        if dilated:
            n_groups = N_DIL
            qg = jnp.tile(a_q_norm[j][:, None, :], (1, N_HEADS, 1)).reshape(-1) * q_scale
            kg = jnp.tile(a_k_norm[j][:, None, :], (1, N_HEADS, 1)).reshape(-1)
            w_qkv, w_o = a_w_qkv[j], a_w_o[j]
        else:
            n_groups = 1
            qg = jnp.tile(b_q_norm[j], N_HEADS) * q_scale
            kg = jnp.tile(b_k_norm[j], N_HEADS)
            w_qkv, w_o = b_w_qkv[j], b_w_o[j]
        n_qk = 2 * n_groups * hd_all
        colgain = jnp.concatenate([qg, kg])[None, :]
        w_qkv = w_qkv.astype(BF16)
        act_dtype = F32 if dilated else BF16
        qk = _norm_qk_proj(x2, gain, w_qkv[:, :n_qk], colgain, rope, bd, seq, act_dtype, tm, 512)
        v = _norm_v_proj(x2, gain, w_qkv[:, n_qk:], act_dtype, tm, 512)
        qk, v = qk.reshape(batch, seq, -1), v.reshape(batch, seq, -1)
        attn = _dilated_attention(qk, v, batch, seq) if dilated else _moba_attention(qk, v, batch, seq)
        x2 = _out_proj_residual(x2, attn.reshape(batch * seq, hd_all), w_o.astype(BF16), tm)
        x2 = _conv_ffn_residual(x2, ffn_norm[layer][None, :], ffn_w_up[layer].astype(BF16),
                                ffn_conv_w[layer], ffn_conv_b[layer][None, :],
                                ffn_w_down[layer].astype(BF16), seq, 512, 256, 4)
    return x2.reshape(batch, seq, d_model)
```

```python
import functools
import math

import jax
import jax.numpy as jnp
from jax import lax
from jax.experimental import pallas as pl
from jax.experimental.pallas import tpu as pltpu

N_HEADS = 16
HEAD_DIM = 64
ROPE_DIM = HEAD_DIM // 4
ROPE_THETA = 500000.0
ATTN_SCALE = HEAD_DIM ** -0.5
DILATED_PAIRS = ((128, 1), (512, 4), (2048, 16))
N_DIL = len(DILATED_PAIRS)
BAND_BLOCK = 128
MOBA_BLOCK = 256
MOBA_TOPK = 3
CONV_WIDTH = 3
RMS_EPS = 1e-6

LANES = 128
BF16_SUBLANES = 16
MXU_WIDTH = 256
HEADS_PER_SLAB = LANES // HEAD_DIM
MASK_VALUE = -1e30
MOBA_CHAINS = 4
BAND_UNROLL = 16
CONV_HALO = 8
VMEM_LIMIT = 56 * 1024 * 1024
LOG2_E = math.log2(math.e)

F32 = jnp.float32
BF16 = jnp.bfloat16
NT_DIMS = (((1,), (1,)), ((), ()))


def _params(semantics):
    return pltpu.CompilerParams(dimension_semantics=semantics, vmem_limit_bytes=VMEM_LIMIT)


def _rmsnorm_bf16(x_ref, g_ref):
    x = x_ref[...]
    ms = jnp.mean(x * x, axis=-1, keepdims=True)
    return (x * lax.rsqrt(ms + RMS_EPS) * g_ref[...]).astype(BF16)


def _qk_proj_kernel(x_ref, g_ref, w_ref, cg_ref, cos_ref, sa_ref, sb_ref, bd_ref, o_ref, *, tn):
    hn = _rmsnorm_bf16(x_ref, g_ref)
    bd = bd_ref[...]
    half = ROPE_DIM // 2
    for lo in range(0, w_ref.shape[1], tn):
        acc = jnp.dot(hn, w_ref[:, lo:lo + tn], preferred_element_type=F32)
        for c in range(tn // MXU_WIDTH):
            wide = acc[:, c * MXU_WIDTH:(c + 1) * MXU_WIDTH]
            ss_wide = jnp.dot((wide * wide).astype(BF16), bd, preferred_element_type=F32)
            for h in range(MXU_WIDTH // LANES):
                sl = slice(lo + c * MXU_WIDTH + h * LANES, lo + c * MXU_WIDTH + (h + 1) * LANES)
                a = wide[:, h * LANES:(h + 1) * LANES]
                ss = ss_wide[:, h * LANES:(h + 1) * LANES]
                y = a * lax.rsqrt(ss * (1.0 / HEAD_DIM) + RMS_EPS) * cg_ref[:, sl]
                y = (y * cos_ref[...]
                     + pltpu.roll(y, half, 1) * sa_ref[...]
                     + pltpu.roll(y, LANES - half, 1) * sb_ref[...])
                o_ref[:, sl] = y.astype(o_ref.dtype)


def _v_proj_kernel(x_ref, g_ref, w_ref, o_ref, *, tn):
    hn = _rmsnorm_bf16(x_ref, g_ref)
    for lo in range(0, w_ref.shape[1], tn):
        o_ref[:, lo:lo + tn] = jnp.dot(hn, w_ref[:, lo:lo + tn], preferred_element_type=F32).astype(o_ref.dtype)


def _norm_qk_proj(x2, gain, w, colgain, rope, bd, seq, out_dtype, tm, tn):
    m, d = x2.shape
    n = w.shape[1]
    cos_t, sa_t, sb_t = rope
    tiles_per_seq = seq // tm
    rope_spec = pl.BlockSpec((tm, LANES), lambda i: (i % tiles_per_seq, 0))
    return pl.pallas_call(
        functools.partial(_qk_proj_kernel, tn=tn),
        grid=(m // tm,),
        in_specs=[
            pl.BlockSpec((tm, d), lambda i: (i, 0)),
            pl.BlockSpec((1, d), lambda i: (0, 0)),
            pl.BlockSpec((d, n), lambda i: (0, 0), pipeline_mode=pl.Buffered(1)),
            pl.BlockSpec((1, n), lambda i: (0, 0)),
            rope_spec, rope_spec, rope_spec,
            pl.BlockSpec((MXU_WIDTH, MXU_WIDTH), lambda i: (0, 0)),
        ],
        out_specs=pl.BlockSpec((tm, n), lambda i: (i, 0)),
        out_shape=jax.ShapeDtypeStruct((m, n), out_dtype),
        compiler_params=_params(("parallel",)),
        name="norm_qk_proj",
    )(x2, gain, w, colgain, cos_t, sa_t, sb_t, bd)


def _norm_v_proj(x2, gain, w, out_dtype, tm, tn):
    m, d = x2.shape
    n = w.shape[1]
    return pl.pallas_call(
        functools.partial(_v_proj_kernel, tn=tn),
        grid=(m // tm,),
        in_specs=[
            pl.BlockSpec((tm, d), lambda i: (i, 0)),
            pl.BlockSpec((1, d), lambda i: (0, 0)),
            pl.BlockSpec((d, n), lambda i: (0, 0), pipeline_mode=pl.Buffered(1)),
        ],
        out_specs=pl.BlockSpec((tm, n), lambda i: (i, 0)),
        out_shape=jax.ShapeDtypeStruct((m, n), out_dtype),
        compiler_params=_params(("parallel",)),
        name="norm_v_proj",
    )(x2, gain, w)


def _dilated_kernel(q0, q1, q2, k0, k1, k2, v0, v1, v2, o_ref, num_ref, m_ref, den_ref, mask_ref, *, tq):
    t = pl.program_id(2)
    q_refs, k_refs, v_refs = (q0, q1, q2), (k0, k1, k2), (v0, v1, v2)
    blk = BAND_BLOCK

    lane = lax.broadcasted_iota(jnp.int32, (blk, LANES), 1)
    first_head = lane < HEAD_DIM
    ones_rhs = jnp.ones((2 * blk, LANES), BF16)

    reaches = sorted({w // d for w, d in DILATED_PAIRS})
    qi = lax.broadcasted_iota(jnp.int32, (2 * blk, 2 * blk), 0) % blk
    kj = lax.broadcasted_iota(jnp.int32, (2 * blk, 2 * blk), 1)
    dist = qi + blk - kj
    for n, reach in enumerate(reaches):
        band = (dist >= 0) & (dist <= reach)
        mask_ref[n, 0] = jnp.where(band, 0.0, MASK_VALUE)
        mask_ref[n, 1] = jnp.where(band & (kj >= blk), 0.0, MASK_VALUE)

    for g, (window, dil) in enumerate(DILATED_PAIRS):
        reach = window // dil
        span = blk * dil
        assert dil & (dil - 1) == 0 and reach <= blk
        qg, kg, vg = q_refs[g], k_refs[g], v_refs[g]
        mask_g = reaches.index(reach)

        def body(idx, carry, dil=dil, span=span, mask_g=mask_g, qg=qg, kg=kg, vg=vg, g=g):
            u = lax.shift_right_logical(idx, int(math.log2(dil)))
            r = lax.bitwise_and(idx, dil - 1)
            qbase = u * span + r
            kbase = t * tq + qbase
            pbase = kbase - span
            no_prev = (pbase < 0).astype(jnp.int32)
            pbase = jnp.maximum(pbase, 0)
            if dil == 1:
                qbase = pl.multiple_of(qbase, blk)
                kbase = pl.multiple_of(kbase, blk)
                pbase = pl.multiple_of(pbase, blk)
            q = qg[pl.ds(qbase, blk, stride=dil), :]
            kc = kg[pl.ds(kbase, blk, stride=dil), :]
            vc = vg[pl.ds(kbase, blk, stride=dil), :]
            kp = kg[pl.ds(pbase, blk, stride=dil), :]
            vp = vg[pl.ds(pbase, blk, stride=dil), :]
            zero = jnp.zeros_like(q)
            q2 = jnp.concatenate([jnp.where(first_head, q, zero),
                                  jnp.where(first_head, zero, q)], axis=0).astype(BF16)
            kcat = jnp.concatenate([kp, kc], axis=0).astype(BF16)
            vcat = jnp.concatenate([jnp.concatenate([vp, vc], axis=0).astype(BF16), ones_rhs], axis=1)
            s = lax.dot_general(q2, kcat, NT_DIMS, preferred_element_type=F32)
            s = s + mask_ref[mask_g, no_prev]
            m = jnp.max(s, axis=-1, keepdims=True)
            p = jnp.exp2(s - m).astype(BF16)
            ov = jnp.dot(p, vcat, preferred_element_type=F32)
            mb = jnp.broadcast_to(m, (2 * blk, LANES))
            rows = pl.ds(qbase, blk, stride=dil)
            num_ref[g, rows, :] = jnp.where(first_head, ov[:blk, :LANES], ov[blk:, :LANES])
            den_ref[g, rows, :] = jnp.where(first_head, ov[:blk, LANES:], ov[blk:, LANES:])
            m_ref[g, rows, :] = jnp.where(first_head, mb[:blk], mb[blk:])
            return carry

        lax.fori_loop(0, (tq // span) * dil, body, 0, unroll=BAND_UNROLL)

    rows = 2 * blk

    def merge(c, carry):
        sl = pl.ds(pl.multiple_of(c * rows, rows), rows)
        ms = [m_ref[g, sl, :] for g in range(N_DIL)]
        mx = functools.reduce(jnp.maximum, ms)
        ws = [jnp.exp2(mg - mx) for mg in ms]
        num = sum(w * num_ref[g, sl, :] for g, w in enumerate(ws))
        den = sum(w * den_ref[g, sl, :] for g, w in enumerate(ws))
        o_ref[sl, :] = (num / den).astype(o_ref.dtype)
        return carry

    lax.fori_loop(0, tq // rows, merge, 0)


def _dilated_attention(qk, v, batch, seq):
    span_max = BAND_BLOCK * max(d for _, d in DILATED_PAIRS)
    tq = span_max
    assert seq % tq == 0
    slabs = N_HEADS // HEADS_PER_SLAB
    n_reach = len({w // d for w, d in DILATED_PAIRS})

    q_specs = [pl.BlockSpec((None, tq, LANES), lambda b, s, t, g=g: (b, t, g * slabs + s)) for g in range(N_DIL)]
    k_specs = [pl.BlockSpec((None, seq, LANES), lambda b, s, t, g=g: (b, 0, (N_DIL + g) * slabs + s))
               for g in range(N_DIL)]
    v_specs = [pl.BlockSpec((None, seq, LANES), lambda b, s, t, g=g: (b, 0, g * slabs + s)) for g in range(N_DIL)]
    stat = pltpu.VMEM((N_DIL, tq, LANES), F32)
    return pl.pallas_call(
        functools.partial(_dilated_kernel, tq=tq),
        grid=(batch, slabs, seq // tq),
        in_specs=q_specs + k_specs + v_specs,
        out_specs=pl.BlockSpec((None, tq, LANES), lambda b, s, t: (b, t, s)),
        out_shape=jax.ShapeDtypeStruct((batch, seq, N_HEADS * HEAD_DIM), BF16),
        scratch_shapes=[stat, stat, stat, pltpu.VMEM((n_reach, 2, 2 * BAND_BLOCK, 2 * BAND_BLOCK), F32)],
        compiler_params=_params(("parallel", "parallel", "arbitrary")),
        name="dilated_attention",
    )(*([qk] * 6 + [v] * 3))


def _moba_kernel(q_ref, k_ref, v_ref, oh_ref, mask_ref, o_ref, km_ref, kmhl_ref, vaug_ref, s_ref, m_ref, acc_ref,
                 *, nblk, chains):
    i = pl.program_id(2)
    blk = MOBA_BLOCK
    nrow = km_ref.shape[1]
    sup = 2 * blk
    lane = lax.broadcasted_iota(jnp.int32, (blk, LANES), 1)
    first_head = lane < HEAD_DIM
    cs = range(chains)

    def slab(c):
        return slice(c * LANES, (c + 1) * LANES)

    @pl.when(i == 0)
    def _():
        for c in cs:
            km_ref[c] = jnp.zeros(km_ref.shape[1:], F32)
            for j in range(nblk):
                rows = slice(j * blk, (j + 1) * blk)
                km_ref[c, j:j + 1, :] = jnp.sum(k_ref[rows, slab(c)].astype(F32), axis=0,
                                                keepdims=True) * (1.0 / blk)
                vj = v_ref[rows, slab(c)].astype(F32)
                vaug_ref[c, 0, rows, :] = jnp.where(first_head, vj, 1.0).astype(BF16)
                vaug_ref[c, 1, rows, :] = jnp.where(first_head, 1.0, vj).astype(BF16)
            km = km_ref[c]
            hi = km.astype(BF16)
            kmhl_ref[c] = jnp.concatenate([hi, (km - hi.astype(F32)).astype(BF16)], axis=0)

    row = lax.broadcasted_iota(jnp.int32, (nrow, 2 * blk), 0)
    rowf = row.astype(F32)

    def augmented_queries(c):
        q = q_ref[:, slab(c)]
        zero = jnp.zeros_like(q)
        q2 = jnp.concatenate([jnp.where(first_head, q, zero), jnp.where(first_head, zero, q)], axis=0)
        gate2 = lax.dot_general(kmhl_ref[c], q2, NT_DIMS, preferred_element_type=F32)
        gate = gate2[:nrow] + gate2[nrow:]
        remaining = row < i
        sel = row >= i
        for _ in range(MOBA_TOPK):
            gm = jnp.max(jnp.where(remaining, gate, -jnp.inf), axis=0, keepdims=True)
            cand = remaining & (gate == gm)
            first = jnp.min(jnp.where(cand, rowf, float(LANES)), axis=0, keepdims=True)
            pick = rowf == first
            sel = sel | pick
            remaining = remaining & jnp.logical_not(pick)
        bias_t = jnp.concatenate([jnp.where(sel, 0.0, MASK_VALUE),
                                  jnp.zeros((LANES - nrow, 2 * blk), F32)], axis=0)
        return jnp.concatenate([q2, bias_t.T.astype(BF16)], axis=1)

    q_aug = [augmented_queries(c) for c in cs]

    own_pair = lax.shift_right_logical(i, 1)
    own_base = own_pair * sup

    def scores(c, base):
        rows = pl.ds(pl.multiple_of(base, sup), sup)
        k_aug = jnp.concatenate([k_ref[rows, slab(c)], oh_ref[rows, :]], axis=1)
        return lax.dot_general(q_aug[c], k_aug, NT_DIMS, preferred_element_type=F32)

    def accumulate(c, buf, tile):
        base = jnp.where(tile == 0, own_base, (tile - 1) * sup)
        s = s_ref[c, buf]
        m_old = m_ref[c]
        m_new = jnp.maximum(m_old, jnp.max(s, axis=-1, keepdims=True))
        alpha = jnp.exp2(m_old - m_new)
        p = jnp.exp2(s - jnp.concatenate([m_new] * (sup // LANES), axis=1)).astype(BF16)
        rows = pl.ds(pl.multiple_of(base, sup), sup)
        pv = jnp.concatenate([jnp.dot(p[:blk], vaug_ref[c, 0, rows, :], preferred_element_type=F32),
                              jnp.dot(p[blk:], vaug_ref[c, 1, rows, :], preferred_element_type=F32)], axis=0)
        acc_ref[c] = acc_ref[c] * alpha + pv
        m_ref[c] = m_new

    own_mask = mask_ref[lax.bitwise_and(i, 1)]
    for c in cs:
        s_ref[c, 0] = scores(c, own_base) + own_mask
        m_ref[c] = jnp.full(m_ref.shape[1:], MASK_VALUE, F32)
        acc_ref[c] = jnp.zeros(acc_ref.shape[1:], F32)

    def advance(dst, src, tile):
        for c in cs:
            s_ref[c, dst] = scores(c, (tile - 1) * sup)
        for c in cs:
            accumulate(c, src, tile - 1)

    def body(n, carry):
        advance(1, 0, 2 * n + 1)
        advance(0, 1, 2 * n + 2)
        return carry

    lax.fori_loop(0, lax.shift_right_logical(own_pair, 1), body, 0)

    @pl.when(lax.bitwise_and(own_pair, 1) == 1)
    def _():
        advance(1, 0, own_pair)
        for c in cs:
            accumulate(c, 1, own_pair)

    @pl.when(lax.bitwise_and(own_pair, 1) == 0)
    def _():
        for c in cs:
            accumulate(c, 0, own_pair)

    for c in cs:
        acc = acc_ref[c]
        o = acc / pltpu.roll(acc, HEAD_DIM, 1)
        o_ref[:, slab(c)] = jnp.where(first_head, o[:blk], o[blk:]).astype(o_ref.dtype)


def _moba_attention(qk, v, batch, seq):
    assert seq % (2 * MOBA_BLOCK) == 0
    nblk = seq // MOBA_BLOCK
    assert nblk <= LANES
    nrow = -(-nblk // BF16_SUBLANES) * BF16_SUBLANES
    slabs = N_HEADS * HEAD_DIM // LANES
    blk = MOBA_BLOCK
    block_id = jnp.arange(seq, dtype=jnp.int32) // blk
    onehot = (block_id[:, None] == jnp.arange(LANES, dtype=jnp.int32)[None, :]).astype(BF16)
    qi = jnp.arange(2 * blk, dtype=jnp.int32)[:, None] % blk
    kj = jnp.arange(2 * blk, dtype=jnp.int32)[None, :]
    own_mask = jnp.stack([jnp.where(kj - qi <= parity * blk, 0.0, MASK_VALUE) for parity in (0, 1)]).astype(F32)
    chains = MOBA_CHAINS
    assert slabs % chains == 0
    width = chains * LANES
    return pl.pallas_call(
        functools.partial(_moba_kernel, nblk=nblk, chains=chains),
        grid=(batch, slabs // chains, nblk),
        in_specs=[
            pl.BlockSpec((None, blk, width), lambda b, s, i: (b, i, s)),
            pl.BlockSpec((None, seq, width), lambda b, s, i: (b, 0, slabs // chains + s)),
            pl.BlockSpec((None, seq, width), lambda b, s, i: (b, 0, s)),
            pl.BlockSpec((seq, LANES), lambda b, s, i: (0, 0)),
            pl.BlockSpec((2, 2 * blk, 2 * blk), lambda b, s, i: (0, 0, 0)),
        ],
        out_specs=pl.BlockSpec((None, blk, width), lambda b, s, i: (b, i, s)),
        out_shape=jax.ShapeDtypeStruct((batch, seq, N_HEADS * HEAD_DIM), BF16),
        scratch_shapes=[
            pltpu.VMEM((chains, nrow, LANES), F32), pltpu.VMEM((chains, 2 * nrow, LANES), BF16),
            pltpu.VMEM((chains, HEADS_PER_SLAB, seq, LANES), BF16),
            pltpu.VMEM((chains, 2, 2 * blk, 2 * blk), F32),
            pltpu.VMEM((chains, 2 * blk, LANES), F32), pltpu.VMEM((chains, 2 * blk, LANES), F32),
        ],
        compiler_params=_params(("parallel", "parallel", "arbitrary")),
        name="moba_attention",
    )(qk, qk, v, onehot, own_mask)


def _out_proj_kernel(x_ref, a_ref, w_ref, o_ref):
    o_ref[...] = x_ref[...] + jnp.dot(a_ref[...], w_ref[...], preferred_element_type=F32)


def _out_proj_residual(x2, attn2, w_o, tm):
    m, d = x2.shape
    da = attn2.shape[1]
    return pl.pallas_call(
        _out_proj_kernel,
        grid=(m // tm,),
        in_specs=[
            pl.BlockSpec((tm, d), lambda i: (i, 0)),
            pl.BlockSpec((tm, da), lambda i: (i, 0)),
            pl.BlockSpec((da, d), lambda i: (0, 0)),
        ],
        out_specs=pl.BlockSpec((tm, d), lambda i: (i, 0)),
        out_shape=jax.ShapeDtypeStruct((m, d), F32),
        compiler_params=_params(("parallel",)),
        name="out_proj_residual",
    )(x2, attn2, w_o)


def _ffn_kernel(x_ref, xh_ref, g_ref, wu_ref, cw_ref, cb_ref, wd_ref, o_ref, *, tiles_per_seq, tf, group):
    i = pl.program_id(0)
    halo = CONV_HALO
    d_ff = wd_ref.shape[0]

    def rms(x):
        ms = jnp.mean(x * x, axis=-1, keepdims=True)
        return x * lax.rsqrt(ms + RMS_EPS) * g_ref[...]

    keep = (i % tiles_per_seq != 0).astype(F32)
    hn = jnp.concatenate([(rms(xh_ref[...]) * keep).astype(BF16), rms(x_ref[...]).astype(BF16)], axis=0)

    def conv(col):
        u = jnp.dot(hn, wu_ref[:, col:col + tf], preferred_element_type=F32)
        cw = cw_ref[:, col:col + tf]
        return (cb_ref[:, col:col + tf]
                + cw[0:1, :] * pltpu.roll(u, 2, 0)[halo:]
                + cw[1:2, :] * pltpu.roll(u, 1, 0)[halo:]
                + cw[2:3, :] * u[halo:])

    acc = x_ref[...]
    for lo in range(0, d_ff, group * tf):
        hi = min(lo + group * tf, d_ff)
        acts = []
        for col in range(lo, hi, tf):
            gate = conv(col)
            val = conv(d_ff + col)
            acts.append((gate * jax.nn.sigmoid(gate) * val).astype(BF16))
        acc = acc + jnp.dot(jnp.concatenate(acts, axis=1), wd_ref[lo:hi, :], preferred_element_type=F32)
    o_ref[...] = acc


def _conv_ffn_residual(x2, gain, w_up, conv_w, conv_b, w_down, seq, tm, tf, group):
    m, d = x2.shape
    d_ff = w_down.shape[0]
    assert d_ff % tf == 0
    halo_blocks = tm // CONV_HALO
    resident = dict(pipeline_mode=pl.Buffered(1))
    return pl.pallas_call(
        functools.partial(_ffn_kernel, tiles_per_seq=seq // tm, tf=tf, group=group),
        grid=(m // tm,),
        in_specs=[
            pl.BlockSpec((tm, d), lambda i: (i, 0)),
            pl.BlockSpec((CONV_HALO, d), lambda i: (jnp.maximum(i * halo_blocks - 1, 0), 0)),
            pl.BlockSpec((1, d), lambda i: (0, 0)),
            pl.BlockSpec((d, 2 * d_ff), lambda i: (0, 0), **resident),
            pl.BlockSpec((CONV_WIDTH, 2 * d_ff), lambda i: (0, 0)),
            pl.BlockSpec((1, 2 * d_ff), lambda i: (0, 0)),
            pl.BlockSpec((d_ff, d), lambda i: (0, 0), **resident),
        ],
        out_specs=pl.BlockSpec((tm, d), lambda i: (i, 0)),
        out_shape=jax.ShapeDtypeStruct((m, d), F32),
        compiler_params=_params(("parallel",)),
        name="conv_ffn_residual",
    )(x2, x2, gain, w_up, conv_w, conv_b, w_down)


def _rope_tables(seq):
    pos = jnp.arange(seq, dtype=F32)
    inv_freq = ROPE_THETA ** (-jnp.arange(0, ROPE_DIM, 2, dtype=F32) / ROPE_DIM)
    ang = pos[:, None] * inv_freq[None, :]
    cos, sin = jnp.cos(ang), jnp.sin(ang)
    half = ROPE_DIM // 2
    pad = jnp.zeros((seq, HEAD_DIM - ROPE_DIM), F32)
    zeros = jnp.zeros((seq, half), F32)
    cos_h = jnp.concatenate([cos, cos, pad + 1.0], axis=1)
    sa_h = jnp.concatenate([zeros, sin, pad], axis=1)
    sb_h = jnp.concatenate([-sin, zeros, pad], axis=1)
    rep = LANES // HEAD_DIM
    return tuple(jnp.tile(t, (1, rep)) for t in (cos_h, sa_h, sb_h))


def _block_diag_ones():
    r = jnp.arange(MXU_WIDTH) // HEAD_DIM
    return (r[:, None] == r[None, :]).astype(BF16)


def kernel(x, attn_norm, a_w_qkv, a_q_norm, a_k_norm, a_w_o, b_w_qkv, b_q_norm, b_k_norm, b_w_o,
           ffn_norm, ffn_w_up, ffn_conv_w, ffn_conv_b, ffn_w_down):
    batch, seq, d_model = x.shape
    depth = attn_norm.shape[0]
    hd_all = N_HEADS * HEAD_DIM
    rope = _rope_tables(seq)
    bd = _block_diag_ones()
    x2 = x.reshape(batch * seq, d_model)
    tm = 512
    q_scale = ATTN_SCALE * LOG2_E

    for layer in range(depth):
        j = layer // 2
        gain = attn_norm[layer][None, :]
        dilated = layer % 2 == 0
        if dilated:
            n_groups = N_DIL
            qg = jnp.tile(a_q_norm[j][:, None, :], (1, N_HEADS, 1)).reshape(-1) * q_scale
            kg = jnp.tile(a_k_norm[j][:, None, :], (1, N_HEADS, 1)).reshape(-1)
            w_qkv, w_o = a_w_qkv[j], a_w_o[j]
        else:
            n_groups = 1
            qg = jnp.tile(b_q_norm[j], N_HEADS) * q_scale
            kg = jnp.tile(b_k_norm[j], N_HEADS)
            w_qkv, w_o = b_w_qkv[j], b_w_o[j]
        n_qk = 2 * n_groups * hd_all
        colgain = jnp.concatenate([qg, kg])[None, :]
        w_qkv = w_qkv.astype(BF16)
        act_dtype = F32 if dilated else BF16
        qk = _norm_qk_proj(x2, gain, w_qkv[:, :n_qk], colgain, rope, bd, seq, act_dtype, tm, 512)
        v = _norm_v_proj(x2, gain, w_qkv[:, n_qk:], act_dtype, tm, 512)
        qk, v = qk.reshape(batch, seq, -1), v.reshape(batch, seq, -1)
        attn = _dilated_attention(qk, v, batch, seq) if dilated else _moba_attention(qk, v, batch, seq)
        x2 = _out_proj_residual(x2, attn.reshape(batch * seq, hd_all), w_o.astype(BF16), tm)
        x2 = _conv_ffn_residual(x2, ffn_norm[layer][None, :], ffn_w_up[layer].astype(BF16),
                                ffn_conv_w[layer], ffn_conv_b[layer][None, :],
                                ffn_w_down[layer].astype(BF16), seq, 512, 256, 4)
    return x2.reshape(batch, seq, d_model)
```

```python
import functools
import math

import jax
import jax.numpy as jnp
from jax import lax
from jax.experimental import pallas as pl
from jax.experimental.pallas import tpu as pltpu

N_HEADS = 16
HEAD_DIM = 64
ROPE_DIM = HEAD_DIM // 4
ROPE_THETA = 500000.0
ATTN_SCALE = HEAD_DIM ** -0.5
DILATED_PAIRS = ((128, 1), (512, 4), (2048, 16))
N_DIL = len(DILATED_PAIRS)
BAND_BLOCK = 128
MOBA_BLOCK = 256
MOBA_TOPK = 3
CONV_WIDTH = 3
RMS_EPS = 1e-6

LANES = 128
BF16_SUBLANES = 16
MXU_WIDTH = 256
HEADS_PER_SLAB = LANES // HEAD_DIM
MASK_VALUE = -1e30
MOBA_CHAINS = 4
BAND_UNROLL = 16
CONV_HALO = 8
VMEM_LIMIT = 56 * 1024 * 1024
LOG2_E = math.log2(math.e)

F32 = jnp.float32
BF16 = jnp.bfloat16
NT_DIMS = (((1,), (1,)), ((), ()))


def _params(semantics):
    return pltpu.CompilerParams(dimension_semantics=semantics, vmem_limit_bytes=VMEM_LIMIT)


def _rmsnorm_bf16(x_ref, g_ref):
    x = x_ref[...]
    ms = jnp.mean(x * x, axis=-1, keepdims=True)
    return (x * lax.rsqrt(ms + RMS_EPS) * g_ref[...]).astype(BF16)


def _qk_proj_kernel(x_ref, g_ref, w_ref, cg_ref, cos_ref, sa_ref, sb_ref, bd_ref, o_ref, *, tn):
    hn = _rmsnorm_bf16(x_ref, g_ref)
    bd = bd_ref[...]
    half = ROPE_DIM // 2
    for lo in range(0, w_ref.shape[1], tn):
        acc = jnp.dot(hn, w_ref[:, lo:lo + tn], preferred_element_type=F32)
        for c in range(tn // MXU_WIDTH):
            wide = acc[:, c * MXU_WIDTH:(c + 1) * MXU_WIDTH]
            ss_wide = jnp.dot((wide * wide).astype(BF16), bd, preferred_element_type=F32)
            for h in range(MXU_WIDTH // LANES):
                sl = slice(lo + c * MXU_WIDTH + h * LANES, lo + c * MXU_WIDTH + (h + 1) * LANES)
                a = wide[:, h * LANES:(h + 1) * LANES]
                ss = ss_wide[:, h * LANES:(h + 1) * LANES]
                y = a * lax.rsqrt(ss * (1.0 / HEAD_DIM) + RMS_EPS) * cg_ref[:, sl]
                y = (y * cos_ref[...]
                     + pltpu.roll(y, half, 1) * sa_ref[...]
                     + pltpu.roll(y, LANES - half, 1) * sb_ref[...])
                o_ref[:, sl] = y.astype(o_ref.dtype)


def _v_proj_kernel(x_ref, g_ref, w_ref, o_ref, *, tn):
    hn = _rmsnorm_bf16(x_ref, g_ref)
    for lo in range(0, w_ref.shape[1], tn):
        o_ref[:, lo:lo + tn] = jnp.dot(hn, w_ref[:, lo:lo + tn], preferred_element_type=F32).astype(o_ref.dtype)


def _norm_qk_proj(x2, gain, w, colgain, rope, bd, seq, out_dtype, tm, tn):
    m, d = x2.shape
    n = colgain.shape[1]
    cos_t, sa_t, sb_t = rope
    tiles_per_seq = seq // tm
    rope_spec = pl.BlockSpec((tm, LANES), lambda i: (i % tiles_per_seq, 0))
    return pl.pallas_call(
        functools.partial(_qk_proj_kernel, tn=tn),
        grid=(m // tm,),
        in_specs=[
            pl.BlockSpec((tm, d), lambda i: (i, 0)),
            pl.BlockSpec((1, d), lambda i: (0, 0)),
            pl.BlockSpec((d, n), lambda i: (0, 0), pipeline_mode=pl.Buffered(1)),
            pl.BlockSpec((1, n), lambda i: (0, 0)),
            rope_spec, rope_spec, rope_spec,
            pl.BlockSpec((MXU_WIDTH, MXU_WIDTH), lambda i: (0, 0)),
        ],
        out_specs=pl.BlockSpec((tm, n), lambda i: (i, 0)),
        out_shape=jax.ShapeDtypeStruct((m, n), out_dtype),
        compiler_params=_params(("parallel",)),
        name="norm_qk_proj",
    )(x2, gain, w, colgain, cos_t, sa_t, sb_t, bd)


def _norm_v_proj(x2, gain, w, n, out_dtype, tm, tn):
    m, d = x2.shape
    last = w.shape[1] // n - 1
    return pl.pallas_call(
        functools.partial(_v_proj_kernel, tn=tn),
        grid=(m // tm,),
        in_specs=[
            pl.BlockSpec((tm, d), lambda i: (i, 0)),
            pl.BlockSpec((1, d), lambda i: (0, 0)),
            pl.BlockSpec((d, n), lambda i: (0, last), pipeline_mode=pl.Buffered(1)),
        ],
        out_specs=pl.BlockSpec((tm, n), lambda i: (i, 0)),
        out_shape=jax.ShapeDtypeStruct((m, n), out_dtype),
        compiler_params=_params(("parallel",)),
        name="norm_v_proj",
    )(x2, gain, w)


def _dilated_kernel(q0, q1, q2, k0, k1, k2, v0, v1, v2, o_ref, num_ref, m_ref, den_ref, mask_ref, *, tq):
    t = pl.program_id(2)
    q_refs, k_refs, v_refs = (q0, q1, q2), (k0, k1, k2), (v0, v1, v2)
    blk = BAND_BLOCK

    lane = lax.broadcasted_iota(jnp.int32, (blk, LANES), 1)
    first_head = lane < HEAD_DIM
    ones_rhs = jnp.ones((2 * blk, LANES), BF16)

    reaches = sorted({w // d for w, d in DILATED_PAIRS})
    qi = lax.broadcasted_iota(jnp.int32, (2 * blk, 2 * blk), 0) % blk
    kj = lax.broadcasted_iota(jnp.int32, (2 * blk, 2 * blk), 1)
    dist = qi + blk - kj
    for n, reach in enumerate(reaches):
        band = (dist >= 0) & (dist <= reach)
        mask_ref[n, 0] = jnp.where(band, 0.0, MASK_VALUE)
        mask_ref[n, 1] = jnp.where(band & (kj >= blk), 0.0, MASK_VALUE)

    for g, (window, dil) in enumerate(DILATED_PAIRS):
        reach = window // dil
        span = blk * dil
        assert dil & (dil - 1) == 0 and reach <= blk
        qg, kg, vg = q_refs[g], k_refs[g], v_refs[g]
        mask_g = reaches.index(reach)

        def body(idx, carry, dil=dil, span=span, mask_g=mask_g, qg=qg, kg=kg, vg=vg, g=g):
            u = lax.shift_right_logical(idx, int(math.log2(dil)))
            r = lax.bitwise_and(idx, dil - 1)
            qbase = u * span + r
            kbase = t * tq + qbase
            pbase = kbase - span
            no_prev = (pbase < 0).astype(jnp.int32)
            pbase = jnp.maximum(pbase, 0)
            if dil == 1:
                qbase = pl.multiple_of(qbase, blk)
                kbase = pl.multiple_of(kbase, blk)
                pbase = pl.multiple_of(pbase, blk)
            q = qg[pl.ds(qbase, blk, stride=dil), :]
            kc = kg[pl.ds(kbase, blk, stride=dil), :]
            vc = vg[pl.ds(kbase, blk, stride=dil), :]
            kp = kg[pl.ds(pbase, blk, stride=dil), :]
            vp = vg[pl.ds(pbase, blk, stride=dil), :]
            zero = jnp.zeros_like(q)
            q2 = jnp.concatenate([jnp.where(first_head, q, zero),
                                  jnp.where(first_head, zero, q)], axis=0).astype(BF16)
            kcat = jnp.concatenate([kp, kc], axis=0).astype(BF16)
            vcat = jnp.concatenate([jnp.concatenate([vp, vc], axis=0).astype(BF16), ones_rhs], axis=1)
            s = lax.dot_general(q2, kcat, NT_DIMS, preferred_element_type=F32)
            s = s + mask_ref[mask_g, no_prev]
            m = jnp.max(s, axis=-1, keepdims=True)
            p = jnp.exp2(s - m).astype(BF16)
            ov = jnp.dot(p, vcat, preferred_element_type=F32)
            mb = jnp.broadcast_to(m, (2 * blk, LANES))
            rows = pl.ds(qbase, blk, stride=dil)
            num_ref[g, rows, :] = jnp.where(first_head, ov[:blk, :LANES], ov[blk:, :LANES])
            den_ref[g, rows, :] = jnp.where(first_head, ov[:blk, LANES:], ov[blk:, LANES:])
            m_ref[g, rows, :] = jnp.where(first_head, mb[:blk], mb[blk:])
            return carry

        lax.fori_loop(0, (tq // span) * dil, body, 0, unroll=BAND_UNROLL)

    rows = 2 * blk

    def merge(c, carry):
        sl = pl.ds(pl.multiple_of(c * rows, rows), rows)
        ms = [m_ref[g, sl, :] for g in range(N_DIL)]
        mx = functools.reduce(jnp.maximum, ms)
        ws = [jnp.exp2(mg - mx) for mg in ms]
        num = sum(w * num_ref[g, sl, :] for g, w in enumerate(ws))
        den = sum(w * den_ref[g, sl, :] for g, w in enumerate(ws))
        o_ref[sl, :] = (num / den).astype(o_ref.dtype)
        return carry

    lax.fori_loop(0, tq // rows, merge, 0)


def _dilated_attention(qk, v, batch, seq):
    span_max = BAND_BLOCK * max(d for _, d in DILATED_PAIRS)
    tq = span_max
    assert seq % tq == 0
    slabs = N_HEADS // HEADS_PER_SLAB
    n_reach = len({w // d for w, d in DILATED_PAIRS})

    q_specs = [pl.BlockSpec((None, tq, LANES), lambda b, s, t, g=g: (b, t, g * slabs + s)) for g in range(N_DIL)]
    k_specs = [pl.BlockSpec((None, seq, LANES), lambda b, s, t, g=g: (b, 0, (N_DIL + g) * slabs + s))
               for g in range(N_DIL)]
    v_specs = [pl.BlockSpec((None, seq, LANES), lambda b, s, t, g=g: (b, 0, g * slabs + s)) for g in range(N_DIL)]
    stat = pltpu.VMEM((N_DIL, tq, LANES), F32)
    return pl.pallas_call(
        functools.partial(_dilated_kernel, tq=tq),
        grid=(batch, slabs, seq // tq),
        in_specs=q_specs + k_specs + v_specs,
        out_specs=pl.BlockSpec((None, tq, LANES), lambda b, s, t: (b, t, s)),
        out_shape=jax.ShapeDtypeStruct((batch, seq, N_HEADS * HEAD_DIM), BF16),
        scratch_shapes=[stat, stat, stat, pltpu.VMEM((n_reach, 2, 2 * BAND_BLOCK, 2 * BAND_BLOCK), F32)],
        compiler_params=_params(("parallel", "parallel", "arbitrary")),
        name="dilated_attention",
    )(*([qk] * 6 + [v] * 3))


def _moba_kernel(q_ref, k_ref, v_ref, oh_ref, mask_ref, o_ref, km_ref, kmhl_ref, vaug_ref, s_ref, m_ref, acc_ref,
                 *, nblk, chains):
    i = pl.program_id(2)
    blk = MOBA_BLOCK
    nrow = km_ref.shape[1]
    sup = 2 * blk
    lane = lax.broadcasted_iota(jnp.int32, (blk, LANES), 1)
    first_head = lane < HEAD_DIM
    cs = range(chains)

    def slab(c):
        return slice(c * LANES, (c + 1) * LANES)

    @pl.when(i == 0)
    def _():
        for c in cs:
            km_ref[c] = jnp.zeros(km_ref.shape[1:], F32)
            for j in range(nblk):
                rows = slice(j * blk, (j + 1) * blk)
                km_ref[c, j:j + 1, :] = jnp.sum(k_ref[rows, slab(c)].astype(F32), axis=0,
                                                keepdims=True) * (1.0 / blk)
                vj = v_ref[rows, slab(c)].astype(F32)
                vaug_ref[c, 0, rows, :] = jnp.where(first_head, vj, 1.0).astype(BF16)
                vaug_ref[c, 1, rows, :] = jnp.where(first_head, 1.0, vj).astype(BF16)
            km = km_ref[c]
            hi = km.astype(BF16)
            kmhl_ref[c] = jnp.concatenate([hi, (km - hi.astype(F32)).astype(BF16)], axis=0)

    row = lax.broadcasted_iota(jnp.int32, (nrow, 2 * blk), 0)
    rowf = row.astype(F32)

    def augmented_queries(c):
        q = q_ref[:, slab(c)]
        zero = jnp.zeros_like(q)
        q2 = jnp.concatenate([jnp.where(first_head, q, zero), jnp.where(first_head, zero, q)], axis=0)
        gate2 = lax.dot_general(kmhl_ref[c], q2, NT_DIMS, preferred_element_type=F32)
        gate = gate2[:nrow] + gate2[nrow:]
        remaining = row < i
        sel = row >= i
        for _ in range(MOBA_TOPK):
            gm = jnp.max(jnp.where(remaining, gate, -jnp.inf), axis=0, keepdims=True)
            cand = remaining & (gate == gm)
            first = jnp.min(jnp.where(cand, rowf, float(LANES)), axis=0, keepdims=True)
            pick = rowf == first
            sel = sel | pick
            remaining = remaining & jnp.logical_not(pick)
        bias_t = jnp.concatenate([jnp.where(sel, 0.0, MASK_VALUE),
                                  jnp.zeros((LANES - nrow, 2 * blk), F32)], axis=0)
        return jnp.concatenate([q2, bias_t.T.astype(BF16)], axis=1)

    q_aug = [augmented_queries(c) for c in cs]

    own_pair = lax.shift_right_logical(i, 1)
    own_base = own_pair * sup

    def scores(c, base):
        rows = pl.ds(pl.multiple_of(base, sup), sup)
        k_aug = jnp.concatenate([k_ref[rows, slab(c)], oh_ref[rows, :]], axis=1)
        return lax.dot_general(q_aug[c], k_aug, NT_DIMS, preferred_element_type=F32)

    def accumulate(c, buf, tile):
        base = jnp.where(tile == 0, own_base, (tile - 1) * sup)
        s = s_ref[c, buf]
        m_old = m_ref[c]
        m_new = jnp.maximum(m_old, jnp.max(s, axis=-1, keepdims=True))
        alpha = jnp.exp2(m_old - m_new)
        p = jnp.exp2(s - jnp.concatenate([m_new] * (sup // LANES), axis=1)).astype(BF16)
        rows = pl.ds(pl.multiple_of(base, sup), sup)
        pv = jnp.concatenate([jnp.dot(p[:blk], vaug_ref[c, 0, rows, :], preferred_element_type=F32),
                              jnp.dot(p[blk:], vaug_ref[c, 1, rows, :], preferred_element_type=F32)], axis=0)
        acc_ref[c] = acc_ref[c] * alpha + pv
        m_ref[c] = m_new

    own_mask = mask_ref[lax.bitwise_and(i, 1)]
    for c in cs:
        s_ref[c, 0] = scores(c, own_base) + own_mask
        m_ref[c] = jnp.full(m_ref.shape[1:], MASK_VALUE, F32)
        acc_ref[c] = jnp.zeros(acc_ref.shape[1:], F32)

    def advance(dst, src, tile):
        for c in cs:
            s_ref[c, dst] = scores(c, (tile - 1) * sup)
        for c in cs:
            accumulate(c, src, tile - 1)

    def body(n, carry):
        advance(1, 0, 2 * n + 1)
        advance(0, 1, 2 * n + 2)
        return carry

    lax.fori_loop(0, lax.shift_right_logical(own_pair, 1), body, 0)

    @pl.when(lax.bitwise_and(own_pair, 1) == 1)
    def _():
        advance(1, 0, own_pair)
        for c in cs:
            accumulate(c, 1, own_pair)

    @pl.when(lax.bitwise_and(own_pair, 1) == 0)
    def _():
        for c in cs:
            accumulate(c, 0, own_pair)

    for c in cs:
        acc = acc_ref[c]
        o = acc / pltpu.roll(acc, HEAD_DIM, 1)
        o_ref[:, slab(c)] = jnp.where(first_head, o[:blk], o[blk:]).astype(o_ref.dtype)


def _moba_attention(qk, v, batch, seq):
    assert seq % (2 * MOBA_BLOCK) == 0
    nblk = seq // MOBA_BLOCK
    assert nblk <= LANES
    nrow = -(-nblk // BF16_SUBLANES) * BF16_SUBLANES
    slabs = N_HEADS * HEAD_DIM // LANES
    blk = MOBA_BLOCK
    block_id = jnp.arange(seq, dtype=jnp.int32) // blk
    onehot = (block_id[:, None] == jnp.arange(LANES, dtype=jnp.int32)[None, :]).astype(BF16)
    qi = jnp.arange(2 * blk, dtype=jnp.int32)[:, None] % blk
    kj = jnp.arange(2 * blk, dtype=jnp.int32)[None, :]
    own_mask = jnp.stack([jnp.where(kj - qi <= parity * blk, 0.0, MASK_VALUE) for parity in (0, 1)]).astype(F32)
    chains = MOBA_CHAINS
    assert slabs % chains == 0
    width = chains * LANES
    return pl.pallas_call(
        functools.partial(_moba_kernel, nblk=nblk, chains=chains),
        grid=(batch, slabs // chains, nblk),
        in_specs=[
            pl.BlockSpec((None, blk, width), lambda b, s, i: (b, i, s)),
            pl.BlockSpec((None, seq, width), lambda b, s, i: (b, 0, slabs // chains + s)),
            pl.BlockSpec((None, seq, width), lambda b, s, i: (b, 0, s)),
            pl.BlockSpec((seq, LANES), lambda b, s, i: (0, 0)),
            pl.BlockSpec((2, 2 * blk, 2 * blk), lambda b, s, i: (0, 0, 0)),
        ],
        out_specs=pl.BlockSpec((None, blk, width), lambda b, s, i: (b, i, s)),
        out_shape=jax.ShapeDtypeStruct((batch, seq, N_HEADS * HEAD_DIM), BF16),
        scratch_shapes=[
            pltpu.VMEM((chains, nrow, LANES), F32), pltpu.VMEM((chains, 2 * nrow, LANES), BF16),
            pltpu.VMEM((chains, HEADS_PER_SLAB, seq, LANES), BF16),
            pltpu.VMEM((chains, 2, 2 * blk, 2 * blk), F32),
            pltpu.VMEM((chains, 2 * blk, LANES), F32), pltpu.VMEM((chains, 2 * blk, LANES), F32),
        ],
        compiler_params=_params(("parallel", "parallel", "arbitrary")),
        name="moba_attention",
    )(qk, qk, v, onehot, own_mask)


def _ffn_kernel(x_ref, xh_ref, a_ref, ah_ref, wo_ref, g_ref, wu_ref, cw_ref, cb_ref, wd_ref, o_ref,
                *, tiles_per_seq, tf, group):
    i = pl.program_id(0)
    halo = CONV_HALO
    d_ff = wd_ref.shape[0]

    def rms(x):
        ms = jnp.mean(x * x, axis=-1, keepdims=True)
        return x * lax.rsqrt(ms + RMS_EPS) * g_ref[...]

    wo = wo_ref[...]
    x1 = x_ref[...] + jnp.dot(a_ref[...], wo, preferred_element_type=F32)
    xh1 = xh_ref[...] + jnp.dot(ah_ref[...], wo, preferred_element_type=F32)[ah_ref.shape[0] - halo:]

    keep = (i % tiles_per_seq != 0).astype(F32)
    hn = jnp.concatenate([(rms(xh1) * keep).astype(BF16), rms(x1).astype(BF16)], axis=0)

    def conv(col):
        u = jnp.dot(hn, wu_ref[:, col:col + tf], preferred_element_type=F32)
        cw = cw_ref[:, col:col + tf]
        return (cb_ref[:, col:col + tf]
                + cw[0:1, :] * pltpu.roll(u, 2, 0)[halo:]
                + cw[1:2, :] * pltpu.roll(u, 1, 0)[halo:]
                + cw[2:3, :] * u[halo:])

    acc = x1
    for lo in range(0, d_ff, group * tf):
        hi = min(lo + group * tf, d_ff)
        acts = []
        for col in range(lo, hi, tf):
            gate = conv(col)
            val = conv(d_ff + col)
            acts.append((gate * jax.nn.sigmoid(gate) * val).astype(BF16))
        acc = acc + jnp.dot(jnp.concatenate(acts, axis=1), wd_ref[lo:hi, :], preferred_element_type=F32)
    o_ref[...] = acc


def _attn_out_conv_ffn(x2, attn2, w_o, gain, w_up, conv_w, conv_b, w_down, seq, tm, tf, group):
    m, d = x2.shape
    da = attn2.shape[1]
    d_ff = w_down.shape[0]
    assert d_ff % tf == 0
    halo_blocks = tm // CONV_HALO
    attn_halo = BF16_SUBLANES
    attn_halo_blocks = tm // attn_halo
    resident = dict(pipeline_mode=pl.Buffered(1))
    return pl.pallas_call(
        functools.partial(_ffn_kernel, tiles_per_seq=seq // tm, tf=tf, group=group),
        grid=(m // tm,),
        in_specs=[
            pl.BlockSpec((tm, d), lambda i: (i, 0)),
            pl.BlockSpec((CONV_HALO, d), lambda i: (jnp.maximum(i * halo_blocks - 1, 0), 0)),
            pl.BlockSpec((tm, da), lambda i: (i, 0)),
            pl.BlockSpec((attn_halo, da), lambda i: (jnp.maximum(i * attn_halo_blocks - 1, 0), 0)),
            pl.BlockSpec((da, d), lambda i: (0, 0), **resident),
            pl.BlockSpec((1, d), lambda i: (0, 0)),
            pl.BlockSpec((d, 2 * d_ff), lambda i: (0, 0), **resident),
            pl.BlockSpec((CONV_WIDTH, 2 * d_ff), lambda i: (0, 0)),
            pl.BlockSpec((1, 2 * d_ff), lambda i: (0, 0)),
            pl.BlockSpec((d_ff, d), lambda i: (0, 0), **resident),
        ],
        out_specs=pl.BlockSpec((tm, d), lambda i: (i, 0)),
        out_shape=jax.ShapeDtypeStruct((m, d), F32),
        compiler_params=_params(("parallel",)),
        name="attn_out_conv_ffn",
    )(x2, x2, attn2, attn2, w_o, gain, w_up, conv_w, conv_b, w_down)


def _rope_tables(seq):
    pos = jnp.arange(seq, dtype=F32)
    inv_freq = ROPE_THETA ** (-jnp.arange(0, ROPE_DIM, 2, dtype=F32) / ROPE_DIM)
    ang = pos[:, None] * inv_freq[None, :]
    cos, sin = jnp.cos(ang), jnp.sin(ang)
    half = ROPE_DIM // 2
    pad = jnp.zeros((seq, HEAD_DIM - ROPE_DIM), F32)
    zeros = jnp.zeros((seq, half), F32)
    cos_h = jnp.concatenate([cos, cos, pad + 1.0], axis=1)
    sa_h = jnp.concatenate([zeros, sin, pad], axis=1)
    sb_h = jnp.concatenate([-sin, zeros, pad], axis=1)
    rep = LANES // HEAD_DIM
    return tuple(jnp.tile(t, (1, rep)) for t in (cos_h, sa_h, sb_h))


def _block_diag_ones():
    r = jnp.arange(MXU_WIDTH) // HEAD_DIM
    return (r[:, None] == r[None, :]).astype(BF16)


def kernel(x, attn_norm, a_w_qkv, a_q_norm, a_k_norm, a_w_o, b_w_qkv, b_q_norm, b_k_norm, b_w_o,
           ffn_norm, ffn_w_up, ffn_conv_w, ffn_conv_b, ffn_w_down):
    batch, seq, d_model = x.shape
    depth = attn_norm.shape[0]
    hd_all = N_HEADS * HEAD_DIM
    rope = _rope_tables(seq)
    bd = _block_diag_ones()
    x2 = x.reshape(batch * seq, d_model)
    tm = 512
    q_scale = ATTN_SCALE * LOG2_E

    for layer in range(depth):
        j = layer // 2
        gain = attn_norm[layer][None, :]
        dilated = layer % 2 == 0
        if dilated:
            n_groups = N_DIL
            qg = jnp.tile(a_q_norm[j][:, None, :], (1, N_HEADS, 1)).reshape(-1) * q_scale
            kg = jnp.tile(a_k_norm[j][:, None, :], (1, N_HEADS, 1)).reshape(-1)
            w_qkv, w_o = a_w_qkv[j], a_w_o[j]
        else:
            n_groups = 1
            qg = jnp.tile(b_q_norm[j], N_HEADS) * q_scale
            kg = jnp.tile(b_k_norm[j], N_HEADS)
            w_qkv, w_o = b_w_qkv[j], b_w_o[j]
        n_qk = 2 * n_groups * hd_all
        colgain = jnp.concatenate([qg, kg])[None, :]
        w_qkv = w_qkv.astype(BF16)
        act_dtype = F32 if dilated else BF16
        qk = _norm_qk_proj(x2, gain, w_qkv, colgain, rope, bd, seq, act_dtype, tm, 512)
        v = _norm_v_proj(x2, gain, w_qkv, n_qk // 2, act_dtype, tm, 512)
        qk, v = qk.reshape(batch, seq, -1), v.reshape(batch, seq, -1)
        attn = _dilated_attention(qk, v, batch, seq) if dilated else _moba_attention(qk, v, batch, seq)
        x2 = _attn_out_conv_ffn(x2, attn.reshape(batch * seq, hd_all), w_o.astype(BF16),
                                ffn_norm[layer][None, :], ffn_w_up[layer].astype(BF16),
                                ffn_conv_w[layer], ffn_conv_b[layer][None, :],
                                ffn_w_down[layer].astype(BF16), seq, 512, 256, 4)
    return x2.reshape(batch, seq, d_model)
```

```python
import functools
import math

import jax
import jax.numpy as jnp
from jax import lax
from jax.experimental import pallas as pl
from jax.experimental.pallas import tpu as pltpu

N_HEADS = 16
HEAD_DIM = 64
ROPE_DIM = HEAD_DIM // 4
ROPE_THETA = 500000.0
ATTN_SCALE = HEAD_DIM ** -0.5
DILATED_PAIRS = ((128, 1), (512, 4), (2048, 16))
N_DIL = len(DILATED_PAIRS)
BAND_BLOCK = 128
MOBA_BLOCK = 256
MOBA_TOPK = 3
CONV_WIDTH = 3
RMS_EPS = 1e-6

LANES = 128
BF16_SUBLANES = 16
MXU_WIDTH = 256
HEADS_PER_SLAB = LANES // HEAD_DIM
MASK_VALUE = -1e30
MOBA_CHAINS = 4
BAND_UNROLL = 16
CONV_HALO = 8
VMEM_LIMIT = 56 * 1024 * 1024
LOG2_E = math.log2(math.e)

F32 = jnp.float32
BF16 = jnp.bfloat16
NT_DIMS = (((1,), (1,)), ((), ()))


def _params(semantics):
    return pltpu.CompilerParams(dimension_semantics=semantics, vmem_limit_bytes=VMEM_LIMIT)


def _rmsnorm_bf16(x_ref, g_ref):
    x = x_ref[...]
    ms = jnp.mean(x * x, axis=-1, keepdims=True)
    return (x * lax.rsqrt(ms + RMS_EPS) * g_ref[...]).astype(BF16)


def _qk_proj_kernel(x_ref, g_ref, w_ref, cg_ref, cos_ref, sin_ref, bd_ref, o_ref, *, tn):
    hn = _rmsnorm_bf16(x_ref, g_ref)
    bd = bd_ref[...]
    for lo in range(0, w_ref.shape[1], tn):
        acc = jnp.dot(hn, w_ref[:, lo:lo + tn], preferred_element_type=F32)
        for c in range(tn // MXU_WIDTH):
            wide = acc[:, c * MXU_WIDTH:(c + 1) * MXU_WIDTH]
            ss_wide = jnp.dot((wide * wide).astype(BF16), bd, preferred_element_type=F32)
            for h in range(MXU_WIDTH // LANES):
                sl = slice(lo + c * MXU_WIDTH + h * LANES, lo + c * MXU_WIDTH + (h + 1) * LANES)
                a = wide[:, h * LANES:(h + 1) * LANES]
                ss = ss_wide[:, h * LANES:(h + 1) * LANES]
                y = a * lax.rsqrt(ss * (1.0 / HEAD_DIM) + RMS_EPS) * cg_ref[:, sl]
                y = y * cos_ref[...] + pltpu.roll(y, LANES // 2, 1) * sin_ref[...]
                o_ref[:, sl] = y.astype(o_ref.dtype)


def _v_proj_kernel(x_ref, g_ref, w_ref, o_ref, *, tn):
    hn = _rmsnorm_bf16(x_ref, g_ref)
    for lo in range(0, w_ref.shape[1], tn):
        o_ref[:, lo:lo + tn] = jnp.dot(hn, w_ref[:, lo:lo + tn], preferred_element_type=F32).astype(o_ref.dtype)


def _norm_qk_proj(x2, gain, w, colgain, rope, bd, seq, out_dtype, tm, tn):
    m, d = x2.shape
    n = colgain.shape[1]
    cos_t, sin_t = rope
    tiles_per_seq = seq // tm
    rope_spec = pl.BlockSpec((tm, LANES), lambda i: (i % tiles_per_seq, 0))
    return pl.pallas_call(
        functools.partial(_qk_proj_kernel, tn=tn),
        grid=(m // tm,),
        in_specs=[
            pl.BlockSpec((tm, d), lambda i: (i, 0)),
            pl.BlockSpec((1, d), lambda i: (0, 0)),
            pl.BlockSpec((d, n), lambda i: (0, 0), pipeline_mode=pl.Buffered(1)),
            pl.BlockSpec((1, n), lambda i: (0, 0)),
            rope_spec, rope_spec,
            pl.BlockSpec((MXU_WIDTH, MXU_WIDTH), lambda i: (0, 0)),
        ],
        out_specs=pl.BlockSpec((tm, n), lambda i: (i, 0)),
        out_shape=jax.ShapeDtypeStruct((m, n), out_dtype),
        compiler_params=_params(("parallel",)),
        name="norm_qk_proj",
    )(x2, gain, w, colgain, cos_t, sin_t, bd)


def _norm_v_proj(x2, gain, w, n, out_dtype, tm, tn):
    m, d = x2.shape
    last = w.shape[1] // n - 1
    return pl.pallas_call(
        functools.partial(_v_proj_kernel, tn=tn),
        grid=(m // tm,),
        in_specs=[
            pl.BlockSpec((tm, d), lambda i: (i, 0)),
            pl.BlockSpec((1, d), lambda i: (0, 0)),
            pl.BlockSpec((d, n), lambda i: (0, last), pipeline_mode=pl.Buffered(1)),
        ],
        out_specs=pl.BlockSpec((tm, n), lambda i: (i, 0)),
        out_shape=jax.ShapeDtypeStruct((m, n), out_dtype),
        compiler_params=_params(("parallel",)),
        name="norm_v_proj",
    )(x2, gain, w)


def _dilated_kernel(q0, q1, q2, k0, k1, k2, v0, v1, v2, o_ref, num_ref, m_ref, den_ref, mask_ref, *, tq):
    t = pl.program_id(2)
    q_refs, k_refs, v_refs = (q0, q1, q2), (k0, k1, k2), (v0, v1, v2)
    blk = BAND_BLOCK

    lane = lax.broadcasted_iota(jnp.int32, (blk, LANES), 1)
    first_head = lane < HEAD_DIM
    qk_first = _qk_first_head(lane)
    ones_rhs = jnp.ones((2 * blk, LANES), BF16)

    reaches = sorted({w // d for w, d in DILATED_PAIRS})
    qi = lax.broadcasted_iota(jnp.int32, (2 * blk, 2 * blk), 0) % blk
    kj = lax.broadcasted_iota(jnp.int32, (2 * blk, 2 * blk), 1)
    dist = qi + blk - kj
    for n, reach in enumerate(reaches):
        band = (dist >= 0) & (dist <= reach)
        mask_ref[n, 0] = jnp.where(band, 0.0, MASK_VALUE)
        mask_ref[n, 1] = jnp.where(band & (kj >= blk), 0.0, MASK_VALUE)

    for g, (window, dil) in enumerate(DILATED_PAIRS):
        reach = window // dil
        span = blk * dil
        assert dil & (dil - 1) == 0 and reach <= blk
        qg, kg, vg = q_refs[g], k_refs[g], v_refs[g]
        mask_g = reaches.index(reach)

        def body(idx, carry, dil=dil, span=span, mask_g=mask_g, qg=qg, kg=kg, vg=vg, g=g):
            u = lax.shift_right_logical(idx, int(math.log2(dil)))
            r = lax.bitwise_and(idx, dil - 1)
            qbase = u * span + r
            kbase = t * tq + qbase
            pbase = kbase - span
            no_prev = (pbase < 0).astype(jnp.int32)
            pbase = jnp.maximum(pbase, 0)
            if dil == 1:
                qbase = pl.multiple_of(qbase, blk)
                kbase = pl.multiple_of(kbase, blk)
                pbase = pl.multiple_of(pbase, blk)
            q = qg[pl.ds(qbase, blk, stride=dil), :]
            kc = kg[pl.ds(kbase, blk, stride=dil), :]
            vc = vg[pl.ds(kbase, blk, stride=dil), :]
            kp = kg[pl.ds(pbase, blk, stride=dil), :]
            vp = vg[pl.ds(pbase, blk, stride=dil), :]
            zero = jnp.zeros_like(q)
            q2 = jnp.concatenate([jnp.where(qk_first, q, zero),
                                  jnp.where(qk_first, zero, q)], axis=0).astype(BF16)
            kcat = jnp.concatenate([kp, kc], axis=0).astype(BF16)
            vcat = jnp.concatenate([jnp.concatenate([vp, vc], axis=0).astype(BF16), ones_rhs], axis=1)
            s = lax.dot_general(q2, kcat, NT_DIMS, preferred_element_type=F32)
            s = s + mask_ref[mask_g, no_prev]
            m = jnp.max(s, axis=-1, keepdims=True)
            p = jnp.exp2(s - m).astype(BF16)
            ov = jnp.dot(p, vcat, preferred_element_type=F32)
            mb = jnp.broadcast_to(m, (2 * blk, LANES))
            rows = pl.ds(qbase, blk, stride=dil)
            num_ref[g, rows, :] = jnp.where(first_head, ov[:blk, :LANES], ov[blk:, :LANES])
            den_ref[g, rows, :] = jnp.where(first_head, ov[:blk, LANES:], ov[blk:, LANES:])
            m_ref[g, rows, :] = jnp.where(first_head, mb[:blk], mb[blk:])
            return carry

        lax.fori_loop(0, (tq // span) * dil, body, 0, unroll=BAND_UNROLL)

    rows = 2 * blk

    def merge(c, carry):
        sl = pl.ds(pl.multiple_of(c * rows, rows), rows)
        ms = [m_ref[g, sl, :] for g in range(N_DIL)]
        mx = functools.reduce(jnp.maximum, ms)
        ws = [jnp.exp2(mg - mx) for mg in ms]
        num = sum(w * num_ref[g, sl, :] for g, w in enumerate(ws))
        den = sum(w * den_ref[g, sl, :] for g, w in enumerate(ws))
        o_ref[sl, :] = (num / den).astype(o_ref.dtype)
        return carry

    lax.fori_loop(0, tq // rows, merge, 0)


def _dilated_attention(qk, v, batch, seq):
    span_max = BAND_BLOCK * max(d for _, d in DILATED_PAIRS)
    tq = span_max
    assert seq % tq == 0
    slabs = N_HEADS // HEADS_PER_SLAB
    n_reach = len({w // d for w, d in DILATED_PAIRS})

    q_specs = [pl.BlockSpec((None, tq, LANES), lambda b, s, t, g=g: (b, t, g * slabs + s)) for g in range(N_DIL)]
    k_specs = [pl.BlockSpec((None, seq, LANES), lambda b, s, t, g=g: (b, 0, (N_DIL + g) * slabs + s))
               for g in range(N_DIL)]
    v_specs = [pl.BlockSpec((None, seq, LANES), lambda b, s, t, g=g: (b, 0, g * slabs + s)) for g in range(N_DIL)]
    stat = pltpu.VMEM((N_DIL, tq, LANES), F32)
    return pl.pallas_call(
        functools.partial(_dilated_kernel, tq=tq),
        grid=(batch, slabs, seq // tq),
        in_specs=q_specs + k_specs + v_specs,
        out_specs=pl.BlockSpec((None, tq, LANES), lambda b, s, t: (b, t, s)),
        out_shape=jax.ShapeDtypeStruct((batch, seq, N_HEADS * HEAD_DIM), BF16),
        scratch_shapes=[stat, stat, stat, pltpu.VMEM((n_reach, 2, 2 * BAND_BLOCK, 2 * BAND_BLOCK), F32)],
        compiler_params=_params(("parallel", "parallel", "arbitrary")),
        name="dilated_attention",
    )(*([qk] * 6 + [v] * 3))


def _moba_kernel(q_ref, k_ref, v_ref, oh_ref, mask_ref, o_ref, km_ref, kmhl_ref, vaug_ref, s_ref, m_ref, acc_ref,
                 *, nblk, chains):
    i = pl.program_id(2)
    blk = MOBA_BLOCK
    nrow = km_ref.shape[1]
    sup = 2 * blk
    lane = lax.broadcasted_iota(jnp.int32, (blk, LANES), 1)
    first_head = lane < HEAD_DIM
    qk_first = _qk_first_head(lane)
    cs = range(chains)

    def slab(c):
        return slice(c * LANES, (c + 1) * LANES)

    @pl.when(i == 0)
    def _():
        for c in cs:
            km_ref[c] = jnp.zeros(km_ref.shape[1:], F32)
            for j in range(nblk):
                rows = slice(j * blk, (j + 1) * blk)
                km_ref[c, j:j + 1, :] = jnp.sum(k_ref[rows, slab(c)].astype(F32), axis=0,
                                                keepdims=True) * (1.0 / blk)
                vj = v_ref[rows, slab(c)].astype(F32)
                vaug_ref[c, 0, rows, :] = jnp.where(first_head, vj, 1.0).astype(BF16)
                vaug_ref[c, 1, rows, :] = jnp.where(first_head, 1.0, vj).astype(BF16)
            km = km_ref[c]
            hi = km.astype(BF16)
            kmhl_ref[c] = jnp.concatenate([hi, (km - hi.astype(F32)).astype(BF16)], axis=0)

    row = lax.broadcasted_iota(jnp.int32, (nrow, 2 * blk), 0)
    rowf = row.astype(F32)

    def augmented_queries(c):
        q = q_ref[:, slab(c)]
        zero = jnp.zeros_like(q)
        q2 = jnp.concatenate([jnp.where(qk_first, q, zero), jnp.where(qk_first, zero, q)], axis=0)
        gate2 = lax.dot_general(kmhl_ref[c], q2, NT_DIMS, preferred_element_type=F32)
        gate = gate2[:nrow] + gate2[nrow:]
        remaining = row < i
        sel = row >= i
        for _ in range(MOBA_TOPK):
            gm = jnp.max(jnp.where(remaining, gate, -jnp.inf), axis=0, keepdims=True)
            cand = remaining & (gate == gm)
            first = jnp.min(jnp.where(cand, rowf, float(LANES)), axis=0, keepdims=True)
            pick = rowf == first
            sel = sel | pick
            remaining = remaining & jnp.logical_not(pick)
        bias_t = jnp.concatenate([jnp.where(sel, 0.0, MASK_VALUE),
                                  jnp.zeros((LANES - nrow, 2 * blk), F32)], axis=0)
        return jnp.concatenate([q2, bias_t.T.astype(BF16)], axis=1)

    q_aug = [augmented_queries(c) for c in cs]

    own_pair = lax.shift_right_logical(i, 1)
    own_base = own_pair * sup

    def scores(c, base):
        rows = pl.ds(pl.multiple_of(base, sup), sup)
        k_aug = jnp.concatenate([k_ref[rows, slab(c)], oh_ref[rows, :]], axis=1)
        return lax.dot_general(q_aug[c], k_aug, NT_DIMS, preferred_element_type=F32)

    def accumulate(c, buf, tile):
        base = jnp.where(tile == 0, own_base, (tile - 1) * sup)
        s = s_ref[c, buf]
        m_old = m_ref[c]
        m_new = jnp.maximum(m_old, jnp.max(s, axis=-1, keepdims=True))
        alpha = jnp.exp2(m_old - m_new)
        p = jnp.exp2(s - jnp.concatenate([m_new] * (sup // LANES), axis=1)).astype(BF16)
        rows = pl.ds(pl.multiple_of(base, sup), sup)
        pv = jnp.concatenate([jnp.dot(p[:blk], vaug_ref[c, 0, rows, :], preferred_element_type=F32),
                              jnp.dot(p[blk:], vaug_ref[c, 1, rows, :], preferred_element_type=F32)], axis=0)
        acc_ref[c] = acc_ref[c] * alpha + pv
        m_ref[c] = m_new

    own_mask = mask_ref[lax.bitwise_and(i, 1)]
    for c in cs:
        s_ref[c, 0] = scores(c, own_base) + own_mask
        m_ref[c] = jnp.full(m_ref.shape[1:], MASK_VALUE, F32)
        acc_ref[c] = jnp.zeros(acc_ref.shape[1:], F32)

    def advance(dst, src, tile):
        for c in cs:
            s_ref[c, dst] = scores(c, (tile - 1) * sup)
        for c in cs:
            accumulate(c, src, tile - 1)

    def body(n, carry):
        advance(1, 0, 2 * n + 1)
        advance(0, 1, 2 * n + 2)
        return carry

    lax.fori_loop(0, lax.shift_right_logical(own_pair, 1), body, 0)

    @pl.when(lax.bitwise_and(own_pair, 1) == 1)
    def _():
        advance(1, 0, own_pair)
        for c in cs:
            accumulate(c, 1, own_pair)

    @pl.when(lax.bitwise_and(own_pair, 1) == 0)
    def _():
        for c in cs:
            accumulate(c, 0, own_pair)

    for c in cs:
        acc = acc_ref[c]
        o = acc / pltpu.roll(acc, HEAD_DIM, 1)
        o_ref[:, slab(c)] = jnp.where(first_head, o[:blk], o[blk:]).astype(o_ref.dtype)


def _moba_attention(qk, v, batch, seq):
    assert seq % (2 * MOBA_BLOCK) == 0
    nblk = seq // MOBA_BLOCK
    assert nblk <= LANES
    nrow = -(-nblk // BF16_SUBLANES) * BF16_SUBLANES
    slabs = N_HEADS * HEAD_DIM // LANES
    blk = MOBA_BLOCK
    block_id = jnp.arange(seq, dtype=jnp.int32) // blk
    onehot = (block_id[:, None] == jnp.arange(LANES, dtype=jnp.int32)[None, :]).astype(BF16)
    qi = jnp.arange(2 * blk, dtype=jnp.int32)[:, None] % blk
    kj = jnp.arange(2 * blk, dtype=jnp.int32)[None, :]
    own_mask = jnp.stack([jnp.where(kj - qi <= parity * blk, 0.0, MASK_VALUE) for parity in (0, 1)]).astype(F32)
    chains = MOBA_CHAINS
    assert slabs % chains == 0
    width = chains * LANES
    return pl.pallas_call(
        functools.partial(_moba_kernel, nblk=nblk, chains=chains),
        grid=(batch, slabs // chains, nblk),
        in_specs=[
            pl.BlockSpec((None, blk, width), lambda b, s, i: (b, i, s)),
            pl.BlockSpec((None, seq, width), lambda b, s, i: (b, 0, slabs // chains + s)),
            pl.BlockSpec((None, seq, width), lambda b, s, i: (b, 0, s)),
            pl.BlockSpec((seq, LANES), lambda b, s, i: (0, 0)),
            pl.BlockSpec((2, 2 * blk, 2 * blk), lambda b, s, i: (0, 0, 0)),
        ],
        out_specs=pl.BlockSpec((None, blk, width), lambda b, s, i: (b, i, s)),
        out_shape=jax.ShapeDtypeStruct((batch, seq, N_HEADS * HEAD_DIM), BF16),
        scratch_shapes=[
            pltpu.VMEM((chains, nrow, LANES), F32), pltpu.VMEM((chains, 2 * nrow, LANES), BF16),
            pltpu.VMEM((chains, HEADS_PER_SLAB, seq, LANES), BF16),
            pltpu.VMEM((chains, 2, 2 * blk, 2 * blk), F32),
            pltpu.VMEM((chains, 2 * blk, LANES), F32), pltpu.VMEM((chains, 2 * blk, LANES), F32),
        ],
        compiler_params=_params(("parallel", "parallel", "arbitrary")),
        name="moba_attention",
    )(qk, qk, v, onehot, own_mask)


def _ffn_kernel(x_ref, xh_ref, a_ref, ah_ref, wo_ref, g_ref, wu_ref, cw_ref, cb_ref, wd_ref, o_ref,
                *, tiles_per_seq, tf, group):
    i = pl.program_id(0)
    halo = CONV_HALO
    d_ff = wd_ref.shape[0]

    def rms(x):
        ms = jnp.mean(x * x, axis=-1, keepdims=True)
        return x * lax.rsqrt(ms + RMS_EPS) * g_ref[...]

    wo = wo_ref[...]
    x1 = x_ref[...] + jnp.dot(a_ref[...], wo, preferred_element_type=F32)
    xh1 = xh_ref[...] + jnp.dot(ah_ref[...], wo, preferred_element_type=F32)[ah_ref.shape[0] - halo:]

    keep = (i % tiles_per_seq != 0).astype(F32)
    hn = jnp.concatenate([(rms(xh1) * keep).astype(BF16), rms(x1).astype(BF16)], axis=0)

    def conv(col):
        u = jnp.dot(hn, wu_ref[:, col:col + tf], preferred_element_type=F32)
        cw = cw_ref[:, col:col + tf]
        return (cb_ref[:, col:col + tf]
                + cw[0:1, :] * pltpu.roll(u, 2, 0)[halo:]
                + cw[1:2, :] * pltpu.roll(u, 1, 0)[halo:]
                + cw[2:3, :] * u[halo:])

    acc = x1
    for lo in range(0, d_ff, group * tf):
        hi = min(lo + group * tf, d_ff)
        acts = []
        for col in range(lo, hi, tf):
            gate = conv(col)
            val = conv(d_ff + col)
            acts.append((gate * jax.nn.sigmoid(gate) * val).astype(BF16))
        acc = acc + jnp.dot(jnp.concatenate(acts, axis=1), wd_ref[lo:hi, :], preferred_element_type=F32)
    o_ref[...] = acc


def _attn_out_conv_ffn(x2, attn2, w_o, gain, w_up, conv_w, conv_b, w_down, seq, tm, tf, group):
    m, d = x2.shape
    da = attn2.shape[1]
    d_ff = w_down.shape[0]
    assert d_ff % tf == 0
    halo_blocks = tm // CONV_HALO
    attn_halo = BF16_SUBLANES
    attn_halo_blocks = tm // attn_halo
    resident = dict(pipeline_mode=pl.Buffered(1))
    return pl.pallas_call(
        functools.partial(_ffn_kernel, tiles_per_seq=seq // tm, tf=tf, group=group),
        grid=(m // tm,),
        in_specs=[
            pl.BlockSpec((tm, d), lambda i: (i, 0)),
            pl.BlockSpec((CONV_HALO, d), lambda i: (jnp.maximum(i * halo_blocks - 1, 0), 0)),
            pl.BlockSpec((tm, da), lambda i: (i, 0)),
            pl.BlockSpec((attn_halo, da), lambda i: (jnp.maximum(i * attn_halo_blocks - 1, 0), 0)),
            pl.BlockSpec((da, d), lambda i: (0, 0), **resident),
            pl.BlockSpec((1, d), lambda i: (0, 0)),
            pl.BlockSpec((d, 2 * d_ff), lambda i: (0, 0), **resident),
            pl.BlockSpec((CONV_WIDTH, 2 * d_ff), lambda i: (0, 0)),
            pl.BlockSpec((1, 2 * d_ff), lambda i: (0, 0)),
            pl.BlockSpec((d_ff, d), lambda i: (0, 0), **resident),
        ],
        out_specs=pl.BlockSpec((tm, d), lambda i: (i, 0)),
        out_shape=jax.ShapeDtypeStruct((m, d), F32),
        compiler_params=_params(("parallel",)),
        name="attn_out_conv_ffn",
    )(x2, x2, attn2, attn2, w_o, gain, w_up, conv_w, conv_b, w_down)


def _qk_lane_layout():
    half = ROPE_DIM // 2
    rest = (HEAD_DIM - ROPE_DIM) // 2
    src, head, freq = [], [], []
    for part in range(2):
        for h in range(HEADS_PER_SLAB):
            for f in range(half):
                src.append(h * HEAD_DIM + part * half + f); head.append(h); freq.append(f)
        for h in range(HEADS_PER_SLAB):
            for n in range(rest):
                src.append(h * HEAD_DIM + ROPE_DIM + part * rest + n); head.append(h); freq.append(-1)
    assert sorted(src) == list(range(LANES))
    return src, head, freq


def _qk_first_head(lane):
    half = ROPE_DIM // 2
    rest = (HEAD_DIM - ROPE_DIM) // 2
    l = lane % (LANES // 2)
    return (l < half) | ((l >= HEADS_PER_SLAB * half) & (l < HEADS_PER_SLAB * half + rest))


def _rope_tables(seq):
    pos = jnp.arange(seq, dtype=F32)
    inv_freq = ROPE_THETA ** (-jnp.arange(0, ROPE_DIM, 2, dtype=F32) / ROPE_DIM)
    ang = pos[:, None] * inv_freq[None, :]
    cos, sin = jnp.cos(ang), jnp.sin(ang)
    _, _, freq = _qk_lane_layout()
    cos_cols, sin_cols = [], []
    for lane, f in enumerate(freq):
        if f < 0:
            cos_cols.append(jnp.ones((seq,), F32)); sin_cols.append(jnp.zeros((seq,), F32))
        else:
            sign = -1.0 if lane < LANES // 2 else 1.0
            cos_cols.append(cos[:, f]); sin_cols.append(sign * sin[:, f])
    return jnp.stack(cos_cols, axis=1), jnp.stack(sin_cols, axis=1)


def _block_diag_ones():
    _, head, _ = _qk_lane_layout()
    r = jnp.asarray([(n // LANES) * HEADS_PER_SLAB + head[n % LANES] for n in range(MXU_WIDTH)])
    return (r[:, None] == r[None, :]).astype(BF16)


def _qk_columns(n_qk):
    src, _, _ = _qk_lane_layout()
    return jnp.asarray([(n // LANES) * LANES + src[n % LANES] for n in range(n_qk)], jnp.int32)


def kernel(x, attn_norm, a_w_qkv, a_q_norm, a_k_norm, a_w_o, b_w_qkv, b_q_norm, b_k_norm, b_w_o,
           ffn_norm, ffn_w_up, ffn_conv_w, ffn_conv_b, ffn_w_down):
    batch, seq, d_model = x.shape
    depth = attn_norm.shape[0]
    hd_all = N_HEADS * HEAD_DIM
    rope = _rope_tables(seq)
    bd = _block_diag_ones()
    x2 = x.reshape(batch * seq, d_model)
    tm = 512
    q_scale = ATTN_SCALE * LOG2_E

    for layer in range(depth):
        j = layer // 2
        gain = attn_norm[layer][None, :]
        dilated = layer % 2 == 0
        if dilated:
            n_groups = N_DIL
            qg = jnp.tile(a_q_norm[j][:, None, :], (1, N_HEADS, 1)).reshape(-1) * q_scale
            kg = jnp.tile(a_k_norm[j][:, None, :], (1, N_HEADS, 1)).reshape(-1)
            w_qkv, w_o = a_w_qkv[j], a_w_o[j]
        else:
            n_groups = 1
            qg = jnp.tile(b_q_norm[j], N_HEADS) * q_scale
            kg = jnp.tile(b_k_norm[j], N_HEADS)
            w_qkv, w_o = b_w_qkv[j], b_w_o[j]
        n_qk = 2 * n_groups * hd_all
        cols = _qk_columns(n_qk)
        colgain = jnp.concatenate([qg, kg])[cols][None, :]
        all_cols = jnp.concatenate([cols, jnp.arange(n_qk, w_qkv.shape[1], dtype=jnp.int32)])
        w_qkv = jnp.take(w_qkv, all_cols, axis=1).astype(BF16)
        act_dtype = F32 if dilated else BF16
        qk = _norm_qk_proj(x2, gain, w_qkv, colgain, rope, bd, seq, act_dtype, tm, 512)
        v = _norm_v_proj(x2, gain, w_qkv, n_qk // 2, act_dtype, tm, 512)
        qk, v = qk.reshape(batch, seq, -1), v.reshape(batch, seq, -1)
        attn = _dilated_attention(qk, v, batch, seq) if dilated else _moba_attention(qk, v, batch, seq)
        x2 = _attn_out_conv_ffn(x2, attn.reshape(batch * seq, hd_all), w_o.astype(BF16),
                                ffn_norm[layer][None, :], ffn_w_up[layer].astype(BF16),
                                ffn_conv_w[layer], ffn_conv_b[layer][None, :],
                                ffn_w_down[layer].astype(BF16), seq, 1024, 256, 11)
    return x2.reshape(batch, seq, d_model)
```

```python
import functools
import math

import jax
import jax.numpy as jnp
from jax import lax
from jax.experimental import pallas as pl
from jax.experimental.pallas import tpu as pltpu

N_HEADS = 16
HEAD_DIM = 64
ROPE_DIM = HEAD_DIM // 4
ROPE_THETA = 500000.0
ATTN_SCALE = HEAD_DIM ** -0.5
DILATED_PAIRS = ((128, 1), (512, 4), (2048, 16))
N_DIL = len(DILATED_PAIRS)
BAND_BLOCK = 128
MOBA_BLOCK = 256
MOBA_TOPK = 3
CONV_WIDTH = 3
RMS_EPS = 1e-6

LANES = 128
BF16_SUBLANES = 16
MXU_WIDTH = 256
HEADS_PER_SLAB = LANES // HEAD_DIM
MASK_VALUE = -1e30
MOBA_CHAINS = 4
BAND_UNROLL = 16
CONV_HALO = 8
VMEM_LIMIT = 56 * 1024 * 1024
LOG2_E = math.log2(math.e)

F32 = jnp.float32
BF16 = jnp.bfloat16
NT_DIMS = (((1,), (1,)), ((), ()))


def _params(semantics):
    return pltpu.CompilerParams(dimension_semantics=semantics, vmem_limit_bytes=VMEM_LIMIT)


def _rmsnorm_bf16(x_ref, g_ref):
    x = x_ref[...]
    ms = jnp.mean(x * x, axis=-1, keepdims=True)
    return (x * lax.rsqrt(ms + RMS_EPS) * g_ref[...]).astype(BF16)


def _qk_proj_kernel(x_ref, g_ref, w_ref, cg_ref, cos_ref, sin_ref, bd_ref, o_ref, *, tn):
    hn = _rmsnorm_bf16(x_ref, g_ref)
    bd = bd_ref[...]
    for lo in range(0, w_ref.shape[1], tn):
        acc = jnp.dot(hn, w_ref[:, lo:lo + tn], preferred_element_type=F32)
        for c in range(tn // MXU_WIDTH):
            wide = acc[:, c * MXU_WIDTH:(c + 1) * MXU_WIDTH]
            ss_wide = jnp.dot((wide * wide).astype(BF16), bd, preferred_element_type=F32)
            for h in range(MXU_WIDTH // LANES):
                sl = slice(lo + c * MXU_WIDTH + h * LANES, lo + c * MXU_WIDTH + (h + 1) * LANES)
                a = wide[:, h * LANES:(h + 1) * LANES]
                ss = ss_wide[:, h * LANES:(h + 1) * LANES]
                y = a * lax.rsqrt(ss * (1.0 / HEAD_DIM) + RMS_EPS) * cg_ref[:, sl]
                y = y * cos_ref[...] + pltpu.roll(y, LANES // 2, 1) * sin_ref[...]
                o_ref[:, sl] = y.astype(o_ref.dtype)


def _v_proj_kernel(x_ref, g_ref, w_ref, o_ref, *, tn):
    hn = _rmsnorm_bf16(x_ref, g_ref)
    for lo in range(0, w_ref.shape[1], tn):
        o_ref[:, lo:lo + tn] = jnp.dot(hn, w_ref[:, lo:lo + tn], preferred_element_type=F32).astype(o_ref.dtype)


def _norm_qk_proj(x2, gain, w, colgain, rope, bd, seq, out_dtype, tm, tn):
    m, d = x2.shape
    n = colgain.shape[1]
    cos_t, sin_t = rope
    tiles_per_seq = seq // tm
    rope_spec = pl.BlockSpec((tm, LANES), lambda i: (i % tiles_per_seq, 0))
    return pl.pallas_call(
        functools.partial(_qk_proj_kernel, tn=tn),
        grid=(m // tm,),
        in_specs=[
            pl.BlockSpec((tm, d), lambda i: (i, 0)),
            pl.BlockSpec((1, d), lambda i: (0, 0)),
            pl.BlockSpec((d, n), lambda i: (0, 0), pipeline_mode=pl.Buffered(1)),
            pl.BlockSpec((1, n), lambda i: (0, 0)),
            rope_spec, rope_spec,
            pl.BlockSpec((MXU_WIDTH, MXU_WIDTH), lambda i: (0, 0)),
        ],
        out_specs=pl.BlockSpec((tm, n), lambda i: (i, 0)),
        out_shape=jax.ShapeDtypeStruct((m, n), out_dtype),
        compiler_params=_params(("parallel",)),
        name="norm_qk_proj",
    )(x2, gain, w, colgain, cos_t, sin_t, bd)


def _norm_v_proj(x2, gain, w, n, out_dtype, tm, tn):
    m, d = x2.shape
    last = w.shape[1] // n - 1
    return pl.pallas_call(
        functools.partial(_v_proj_kernel, tn=tn),
        grid=(m // tm,),
        in_specs=[
            pl.BlockSpec((tm, d), lambda i: (i, 0)),
            pl.BlockSpec((1, d), lambda i: (0, 0)),
            pl.BlockSpec((d, n), lambda i: (0, last), pipeline_mode=pl.Buffered(1)),
        ],
        out_specs=pl.BlockSpec((tm, n), lambda i: (i, 0)),
        out_shape=jax.ShapeDtypeStruct((m, n), out_dtype),
        compiler_params=_params(("parallel",)),
        name="norm_v_proj",
    )(x2, gain, w)


def _dilated_kernel(q0, q1, q2, k0, k1, k2, v0, v1, v2, o_ref, num_ref, m_ref, den_ref, mask_ref, *, tq):
    t = pl.program_id(2)
    q_refs, k_refs, v_refs = (q0, q1, q2), (k0, k1, k2), (v0, v1, v2)
    blk = BAND_BLOCK

    lane = lax.broadcasted_iota(jnp.int32, (blk, LANES), 1)
    first_head = lane < HEAD_DIM
    qk_first = _qk_first_head(lane)
    ones_rhs = jnp.ones((2 * blk, LANES), BF16)

    reaches = sorted({w // d for w, d in DILATED_PAIRS})
    qi = lax.broadcasted_iota(jnp.int32, (2 * blk, 2 * blk), 0) % blk
    kj = lax.broadcasted_iota(jnp.int32, (2 * blk, 2 * blk), 1)
    dist = qi + blk - kj
    for n, reach in enumerate(reaches):
        band = (dist >= 0) & (dist <= reach)
        mask_ref[n, 0] = jnp.where(band, 0.0, MASK_VALUE)
        mask_ref[n, 1] = jnp.where(band & (kj >= blk), 0.0, MASK_VALUE)

    for g, (window, dil) in enumerate(DILATED_PAIRS):
        reach = window // dil
        span = blk * dil
        assert dil & (dil - 1) == 0 and reach <= blk
        qg, kg, vg = q_refs[g], k_refs[g], v_refs[g]
        mask_g = reaches.index(reach)

        def body(idx, carry, dil=dil, span=span, mask_g=mask_g, qg=qg, kg=kg, vg=vg, g=g):
            u = lax.shift_right_logical(idx, int(math.log2(dil)))
            r = lax.bitwise_and(idx, dil - 1)
            qbase = u * span + r
            kbase = t * tq + qbase
            pbase = kbase - span
            no_prev = (pbase < 0).astype(jnp.int32)
            pbase = jnp.maximum(pbase, 0)
            if dil == 1:
                qbase = pl.multiple_of(qbase, blk)
                kbase = pl.multiple_of(kbase, blk)
                pbase = pl.multiple_of(pbase, blk)
            q = qg[pl.ds(qbase, blk, stride=dil), :]
            kc = kg[pl.ds(kbase, blk, stride=dil), :]
            vc = vg[pl.ds(kbase, blk, stride=dil), :]
            kp = kg[pl.ds(pbase, blk, stride=dil), :]
            vp = vg[pl.ds(pbase, blk, stride=dil), :]
            zero = jnp.zeros_like(q)
            q2 = jnp.concatenate([jnp.where(qk_first, q, zero),
                                  jnp.where(qk_first, zero, q)], axis=0).astype(BF16)
            kcat = jnp.concatenate([kp, kc], axis=0).astype(BF16)
            vcat = jnp.concatenate([jnp.concatenate([vp, vc], axis=0).astype(BF16), ones_rhs], axis=1)
            s = lax.dot_general(q2, kcat, NT_DIMS, preferred_element_type=F32)
            s = s + mask_ref[mask_g, no_prev]
            m = jnp.max(s, axis=-1, keepdims=True)
            p = jnp.exp2(s - m).astype(BF16)
            ov = jnp.dot(p, vcat, preferred_element_type=F32)
            mb = jnp.broadcast_to(m, (2 * blk, LANES))
            rows = pl.ds(qbase, blk, stride=dil)
            num_ref[g, rows, :] = jnp.where(first_head, ov[:blk, :LANES], ov[blk:, :LANES])
            den_ref[g, rows, :] = jnp.where(first_head, ov[:blk, LANES:], ov[blk:, LANES:])
            m_ref[g, rows, :] = jnp.where(first_head, mb[:blk], mb[blk:])
            return carry

        lax.fori_loop(0, (tq // span) * dil, body, 0, unroll=BAND_UNROLL)

    rows = 2 * blk

    def merge(c, carry):
        sl = pl.ds(pl.multiple_of(c * rows, rows), rows)
        ms = [m_ref[g, sl, :] for g in range(N_DIL)]
        mx = functools.reduce(jnp.maximum, ms)
        ws = [jnp.exp2(mg - mx) for mg in ms]
        num = sum(w * num_ref[g, sl, :] for g, w in enumerate(ws))
        den = sum(w * den_ref[g, sl, :] for g, w in enumerate(ws))
        o_ref[sl, :] = (num / den).astype(o_ref.dtype)
        return carry

    lax.fori_loop(0, tq // rows, merge, 0)


def _dilated_attention(qk, v, batch, seq):
    span_max = BAND_BLOCK * max(d for _, d in DILATED_PAIRS)
    tq = span_max
    assert seq % tq == 0
    slabs = N_HEADS // HEADS_PER_SLAB
    n_reach = len({w // d for w, d in DILATED_PAIRS})

    q_specs = [pl.BlockSpec((None, tq, LANES), lambda b, s, t, g=g: (b, t, g * slabs + s)) for g in range(N_DIL)]
    k_specs = [pl.BlockSpec((None, seq, LANES), lambda b, s, t, g=g: (b, 0, (N_DIL + g) * slabs + s))
               for g in range(N_DIL)]
    v_specs = [pl.BlockSpec((None, seq, LANES), lambda b, s, t, g=g: (b, 0, g * slabs + s)) for g in range(N_DIL)]
    stat = pltpu.VMEM((N_DIL, tq, LANES), F32)
    return pl.pallas_call(
        functools.partial(_dilated_kernel, tq=tq),
        grid=(batch, slabs, seq // tq),
        in_specs=q_specs + k_specs + v_specs,
        out_specs=pl.BlockSpec((None, tq, LANES), lambda b, s, t: (b, t, s)),
        out_shape=jax.ShapeDtypeStruct((batch, seq, N_HEADS * HEAD_DIM), BF16),
        scratch_shapes=[stat, stat, stat, pltpu.VMEM((n_reach, 2, 2 * BAND_BLOCK, 2 * BAND_BLOCK), F32)],
        compiler_params=_params(("parallel", "parallel", "arbitrary")),
        name="dilated_attention",
    )(*([qk] * 6 + [v] * 3))


def _moba_kernel(q_ref, k_ref, v_ref, oh_ref, mask_ref, o_ref, km_ref, kmhl_ref, vaug_ref, s_ref, m_ref, acc_ref,
                 *, nblk, chains):
    i = pl.program_id(2)
    blk = MOBA_BLOCK
    nrow = km_ref.shape[1]
    sup = 2 * blk
    lane = lax.broadcasted_iota(jnp.int32, (blk, LANES), 1)
    first_head = lane < HEAD_DIM
    qk_first = _qk_first_head(lane)
    cs = range(chains)

    def slab(c):
        return slice(c * LANES, (c + 1) * LANES)

    @pl.when(i == 0)
    def _():
        for c in cs:
            km_ref[c] = jnp.zeros(km_ref.shape[1:], F32)
            for j in range(nblk):
                rows = slice(j * blk, (j + 1) * blk)
                km_ref[c, j:j + 1, :] = jnp.sum(k_ref[rows, slab(c)].astype(F32), axis=0,
                                                keepdims=True) * (1.0 / blk)
                vj = v_ref[rows, slab(c)].astype(F32)
                vaug_ref[c, 0, rows, :] = jnp.where(first_head, vj, 1.0).astype(BF16)
                vaug_ref[c, 1, rows, :] = jnp.where(first_head, 1.0, vj).astype(BF16)
            km = km_ref[c]
            hi = km.astype(BF16)
            kmhl_ref[c] = jnp.concatenate([hi, (km - hi.astype(F32)).astype(BF16)], axis=0)

    row = lax.broadcasted_iota(jnp.int32, (nrow, 2 * blk), 0)
    rowf = row.astype(F32)

    def augmented_queries(c):
        q = q_ref[:, slab(c)]
        zero = jnp.zeros_like(q)
        q2 = jnp.concatenate([jnp.where(qk_first, q, zero), jnp.where(qk_first, zero, q)], axis=0)
        gate2 = lax.dot_general(kmhl_ref[c], q2, NT_DIMS, preferred_element_type=F32)
        gate = gate2[:nrow] + gate2[nrow:]
        remaining = row < i
        sel = row >= i
        for _ in range(MOBA_TOPK):
            gm = jnp.max(jnp.where(remaining, gate, -jnp.inf), axis=0, keepdims=True)
            cand = remaining & (gate == gm)
            first = jnp.min(jnp.where(cand, rowf, float(LANES)), axis=0, keepdims=True)
            pick = rowf == first
            sel = sel | pick
            remaining = remaining & jnp.logical_not(pick)
        bias_t = jnp.concatenate([jnp.where(sel, 0.0, MASK_VALUE),
                                  jnp.zeros((LANES - nrow, 2 * blk), F32)], axis=0)
        return jnp.concatenate([q2, bias_t.T.astype(BF16)], axis=1)

    q_aug = [augmented_queries(c) for c in cs]

    own_pair = lax.shift_right_logical(i, 1)
    own_base = own_pair * sup

    def scores(c, base):
        rows = pl.ds(pl.multiple_of(base, sup), sup)
        k_aug = jnp.concatenate([k_ref[rows, slab(c)], oh_ref[rows, :]], axis=1)
        return lax.dot_general(q_aug[c], k_aug, NT_DIMS, preferred_element_type=F32)

    def accumulate(c, buf, tile):
        base = jnp.where(tile == 0, own_base, (tile - 1) * sup)
        s = s_ref[c, buf]
        m_old = m_ref[c]
        m_new = jnp.maximum(m_old, jnp.max(s, axis=-1, keepdims=True))
        alpha = jnp.exp2(m_old - m_new)
        p = jnp.exp2(s - jnp.concatenate([m_new] * (sup // LANES), axis=1)).astype(BF16)
        rows = pl.ds(pl.multiple_of(base, sup), sup)
        pv = jnp.concatenate([jnp.dot(p[:blk], vaug_ref[c, 0, rows, :], preferred_element_type=F32),
                              jnp.dot(p[blk:], vaug_ref[c, 1, rows, :], preferred_element_type=F32)], axis=0)
        acc_ref[c] = acc_ref[c] * alpha + pv
        m_ref[c] = m_new

    own_mask = mask_ref[lax.bitwise_and(i, 1)]
    for c in cs:
        s_ref[c, 0] = scores(c, own_base) + own_mask
        m_ref[c] = jnp.full(m_ref.shape[1:], MASK_VALUE, F32)
        acc_ref[c] = jnp.zeros(acc_ref.shape[1:], F32)

    def advance(dst, src, tile):
        for c in cs:
            s_ref[c, dst] = scores(c, (tile - 1) * sup)
        for c in cs:
            accumulate(c, src, tile - 1)

    def body(n, carry):
        advance(1, 0, 2 * n + 1)
        advance(0, 1, 2 * n + 2)
        return carry

    lax.fori_loop(0, lax.shift_right_logical(own_pair, 1), body, 0)

    @pl.when(lax.bitwise_and(own_pair, 1) == 1)
    def _():
        advance(1, 0, own_pair)
        for c in cs:
            accumulate(c, 1, own_pair)

    @pl.when(lax.bitwise_and(own_pair, 1) == 0)
    def _():
        for c in cs:
            accumulate(c, 0, own_pair)

    for c in cs:
        acc = acc_ref[c]
        o = acc / pltpu.roll(acc, HEAD_DIM, 1)
        o_ref[:, slab(c)] = jnp.where(first_head, o[:blk], o[blk:]).astype(o_ref.dtype)


def _moba_attention(qk, v, batch, seq):
    assert seq % (2 * MOBA_BLOCK) == 0
    nblk = seq // MOBA_BLOCK
    assert nblk <= LANES
    nrow = -(-nblk // BF16_SUBLANES) * BF16_SUBLANES
    slabs = N_HEADS * HEAD_DIM // LANES
    blk = MOBA_BLOCK
    block_id = jnp.arange(seq, dtype=jnp.int32) // blk
    onehot = (block_id[:, None] == jnp.arange(LANES, dtype=jnp.int32)[None, :]).astype(BF16)
    qi = jnp.arange(2 * blk, dtype=jnp.int32)[:, None] % blk
    kj = jnp.arange(2 * blk, dtype=jnp.int32)[None, :]
    own_mask = jnp.stack([jnp.where(kj - qi <= parity * blk, 0.0, MASK_VALUE) for parity in (0, 1)]).astype(F32)
    chains = MOBA_CHAINS
    assert slabs % chains == 0
    width = chains * LANES
    return pl.pallas_call(
        functools.partial(_moba_kernel, nblk=nblk, chains=chains),
        grid=(batch, slabs // chains, nblk),
        in_specs=[
            pl.BlockSpec((None, blk, width), lambda b, s, i: (b, i, s)),
            pl.BlockSpec((None, seq, width), lambda b, s, i: (b, 0, slabs // chains + s)),
            pl.BlockSpec((None, seq, width), lambda b, s, i: (b, 0, s)),
            pl.BlockSpec((seq, LANES), lambda b, s, i: (0, 0)),
            pl.BlockSpec((2, 2 * blk, 2 * blk), lambda b, s, i: (0, 0, 0)),
        ],
        out_specs=pl.BlockSpec((None, blk, width), lambda b, s, i: (b, i, s)),
        out_shape=jax.ShapeDtypeStruct((batch, seq, N_HEADS * HEAD_DIM), BF16),
        scratch_shapes=[
            pltpu.VMEM((chains, nrow, LANES), F32), pltpu.VMEM((chains, 2 * nrow, LANES), BF16),
            pltpu.VMEM((chains, HEADS_PER_SLAB, seq, LANES), BF16),
            pltpu.VMEM((chains, 2, 2 * blk, 2 * blk), F32),
            pltpu.VMEM((chains, 2 * blk, LANES), F32), pltpu.VMEM((chains, 2 * blk, LANES), F32),
        ],
        compiler_params=_params(("parallel", "parallel", "arbitrary")),
        name="moba_attention",
    )(qk, qk, v, onehot, own_mask)


def _ffn_kernel(x_ref, xh_ref, a_ref, ah_ref, wo_ref, g_ref, wu_ref, cw_ref, cb_ref, wd_ref, o_ref,
                *, tiles_per_seq, tf, group):
    i = pl.program_id(0)
    halo = CONV_HALO
    d_ff = wd_ref.shape[0]

    def rms(x):
        ms = jnp.mean(x * x, axis=-1, keepdims=True)
        return x * lax.rsqrt(ms + RMS_EPS) * g_ref[...]

    wo = wo_ref[...]
    x1 = x_ref[...] + jnp.dot(a_ref[...], wo, preferred_element_type=F32)
    xh1 = xh_ref[...] + jnp.dot(ah_ref[...], wo, preferred_element_type=F32)[ah_ref.shape[0] - halo:]

    keep = (i % tiles_per_seq != 0).astype(F32)
    hn = jnp.concatenate([(rms(xh1) * keep).astype(BF16), rms(x1).astype(BF16)], axis=0)

    def conv(col):
        u = jnp.dot(hn, wu_ref[:, col:col + tf], preferred_element_type=F32)
        cw = cw_ref[:, col:col + tf]
        return (cb_ref[:, col:col + tf]
                + cw[0:1, :] * pltpu.roll(u, 2, 0)[halo:]
                + cw[1:2, :] * pltpu.roll(u, 1, 0)[halo:]
                + cw[2:3, :] * u[halo:])

    acc = x1
    for lo in range(0, d_ff, group * tf):
        hi = min(lo + group * tf, d_ff)
        acts = []
        for col in range(lo, hi, tf):
            gate = conv(col)
            val = conv(d_ff + col)
            acts.append((gate * jax.nn.sigmoid(gate) * val).astype(BF16))
        acc = acc + jnp.dot(jnp.concatenate(acts, axis=1), wd_ref[lo:hi, :], preferred_element_type=F32)
    o_ref[...] = acc


def _attn_out_conv_ffn(x2, attn2, w_o, gain, w_up, conv_w, conv_b, w_down, seq, tm, tf, group):
    m, d = x2.shape
    da = attn2.shape[1]
    d_ff = w_down.shape[0]
    assert d_ff % tf == 0
    halo_blocks = tm // CONV_HALO
    attn_halo = BF16_SUBLANES
    attn_halo_blocks = tm // attn_halo
    resident = dict(pipeline_mode=pl.Buffered(1))
    return pl.pallas_call(
        functools.partial(_ffn_kernel, tiles_per_seq=seq // tm, tf=tf, group=group),
        grid=(m // tm,),
        in_specs=[
            pl.BlockSpec((tm, d), lambda i: (i, 0)),
            pl.BlockSpec((CONV_HALO, d), lambda i: (jnp.maximum(i * halo_blocks - 1, 0), 0)),
            pl.BlockSpec((tm, da), lambda i: (i, 0)),
            pl.BlockSpec((attn_halo, da), lambda i: (jnp.maximum(i * attn_halo_blocks - 1, 0), 0)),
            pl.BlockSpec((da, d), lambda i: (0, 0), **resident),
            pl.BlockSpec((1, d), lambda i: (0, 0)),
            pl.BlockSpec((d, 2 * d_ff), lambda i: (0, 0), **resident),
            pl.BlockSpec((CONV_WIDTH, 2 * d_ff), lambda i: (0, 0)),
            pl.BlockSpec((1, 2 * d_ff), lambda i: (0, 0)),
            pl.BlockSpec((d_ff, d), lambda i: (0, 0), **resident),
        ],
        out_specs=pl.BlockSpec((tm, d), lambda i: (i, 0)),
        out_shape=jax.ShapeDtypeStruct((m, d), F32),
        compiler_params=_params(("parallel",)),
        name="attn_out_conv_ffn",
    )(x2, x2, attn2, attn2, w_o, gain, w_up, conv_w, conv_b, w_down)


def _qk_lane_layout():
    half = ROPE_DIM // 2
    rest = (HEAD_DIM - ROPE_DIM) // 2
    src, head, freq = [], [], []
    for part in range(2):
        for h in range(HEADS_PER_SLAB):
            for f in range(half):
                src.append(h * HEAD_DIM + part * half + f); head.append(h); freq.append(f)
        for h in range(HEADS_PER_SLAB):
            for n in range(rest):
                src.append(h * HEAD_DIM + ROPE_DIM + part * rest + n); head.append(h); freq.append(-1)
    assert sorted(src) == list(range(LANES))
    return src, head, freq


def _qk_first_head(lane):
    half = ROPE_DIM // 2
    rest = (HEAD_DIM - ROPE_DIM) // 2
    l = lane % (LANES // 2)
    return (l < half) | ((l >= HEADS_PER_SLAB * half) & (l < HEADS_PER_SLAB * half + rest))


def _rope_tables(seq):
    pos = jnp.arange(seq, dtype=F32)
    inv_freq = ROPE_THETA ** (-jnp.arange(0, ROPE_DIM, 2, dtype=F32) / ROPE_DIM)
    ang = pos[:, None] * inv_freq[None, :]
    cos, sin = jnp.cos(ang), jnp.sin(ang)
    rest = LANES // 2 - HEADS_PER_SLAB * (ROPE_DIM // 2)
    ones, zeros = jnp.ones((seq, rest), F32), jnp.zeros((seq, rest), F32)
    cos_rot = jnp.tile(cos, (1, HEADS_PER_SLAB))
    sin_rot = jnp.tile(sin, (1, HEADS_PER_SLAB))
    return (jnp.concatenate([cos_rot, ones, cos_rot, ones], axis=1),
            jnp.concatenate([-sin_rot, zeros, sin_rot, zeros], axis=1))


def _block_diag_ones():
    _, head, _ = _qk_lane_layout()
    r = jnp.asarray([(n // LANES) * HEADS_PER_SLAB + head[n % LANES] for n in range(MXU_WIDTH)])
    return (r[:, None] == r[None, :]).astype(BF16)


def _to_qk_lane_layout(t):
    half = ROPE_DIM // 2
    mid = ROPE_DIM + (HEAD_DIM - ROPE_DIM) // 2
    lead = t.shape[:-1]
    h = t.reshape(lead + (-1, HEADS_PER_SLAB, HEAD_DIM))

    def flat(a):
        return a.reshape(lead + (h.shape[-3], -1))

    out = jnp.concatenate([flat(h[..., :half]), flat(h[..., ROPE_DIM:mid]),
                           flat(h[..., half:ROPE_DIM]), flat(h[..., mid:])], axis=-1)
    return out.reshape(t.shape)


def kernel(x, attn_norm, a_w_qkv, a_q_norm, a_k_norm, a_w_o, b_w_qkv, b_q_norm, b_k_norm, b_w_o,
           ffn_norm, ffn_w_up, ffn_conv_w, ffn_conv_b, ffn_w_down):
    batch, seq, d_model = x.shape
    depth = attn_norm.shape[0]
    hd_all = N_HEADS * HEAD_DIM
    rope = _rope_tables(seq)
    bd = _block_diag_ones()
    x2 = x.reshape(batch * seq, d_model)
    tm = 512
    q_scale = ATTN_SCALE * LOG2_E

    for layer in range(depth):
        j = layer // 2
        gain = attn_norm[layer][None, :]
        dilated = layer % 2 == 0
        if dilated:
            n_groups = N_DIL
            qg = jnp.tile(a_q_norm[j][:, None, :], (1, N_HEADS, 1)).reshape(-1) * q_scale
            kg = jnp.tile(a_k_norm[j][:, None, :], (1, N_HEADS, 1)).reshape(-1)
            w_qkv, w_o = a_w_qkv[j], a_w_o[j]
        else:
            n_groups = 1
            qg = jnp.tile(b_q_norm[j], N_HEADS) * q_scale
            kg = jnp.tile(b_k_norm[j], N_HEADS)
            w_qkv, w_o = b_w_qkv[j], b_w_o[j]
        n_qk = 2 * n_groups * hd_all
        colgain = _to_qk_lane_layout(jnp.concatenate([qg, kg]))[None, :]
        w_qkv = jnp.concatenate([_to_qk_lane_layout(w_qkv[:, :n_qk]), w_qkv[:, n_qk:]], axis=1).astype(BF16)
        act_dtype = F32 if dilated else BF16
        qk = _norm_qk_proj(x2, gain, w_qkv, colgain, rope, bd, seq, act_dtype, tm, 512)
        v = _norm_v_proj(x2, gain, w_qkv, n_qk // 2, act_dtype, tm, 512)
        qk, v = qk.reshape(batch, seq, -1), v.reshape(batch, seq, -1)
        attn = _dilated_attention(qk, v, batch, seq) if dilated else _moba_attention(qk, v, batch, seq)
        x2 = _attn_out_conv_ffn(x2, attn.reshape(batch * seq, hd_all), w_o.astype(BF16),
                                ffn_norm[layer][None, :], ffn_w_up[layer].astype(BF16),
                                ffn_conv_w[layer], ffn_conv_b[layer][None, :],
                                ffn_w_down[layer].astype(BF16), seq, 1024, 256, 11)
    return x2.reshape(batch, seq, d_model)
```

```python
import functools
import math

import jax
import jax.numpy as jnp
from jax import lax
from jax.experimental import pallas as pl
from jax.experimental.pallas import tpu as pltpu

N_HEADS = 16
HEAD_DIM = 64
ROPE_DIM = HEAD_DIM // 4
ROPE_THETA = 500000.0
ATTN_SCALE = HEAD_DIM ** -0.5
DILATED_PAIRS = ((128, 1), (512, 4), (2048, 16))
N_DIL = len(DILATED_PAIRS)
BAND_BLOCK = 128
MOBA_BLOCK = 256
MOBA_TOPK = 3
CONV_WIDTH = 3
RMS_EPS = 1e-6

LANES = 128
BF16_SUBLANES = 16
MXU_WIDTH = 256
HEADS_PER_SLAB = LANES // HEAD_DIM
MASK_VALUE = -1e30
MOBA_CHAINS = 4
CONV_HALO = 8
VMEM_LIMIT = 56 * 1024 * 1024
LOG2_E = math.log2(math.e)

F32 = jnp.float32
BF16 = jnp.bfloat16
NT_DIMS = (((1,), (1,)), ((), ()))


def _params(semantics):
    return pltpu.CompilerParams(dimension_semantics=semantics, vmem_limit_bytes=VMEM_LIMIT)


def _rmsnorm_bf16(x_ref, g_ref):
    x = x_ref[...]
    ms = jnp.mean(x * x, axis=-1, keepdims=True)
    return (x * lax.rsqrt(ms + RMS_EPS) * g_ref[...]).astype(BF16)


def _qk_proj_kernel(x_ref, g_ref, w_ref, cg_ref, cos_ref, sa_ref, sb_ref, bd_ref, o_ref, *, tn):
    hn = _rmsnorm_bf16(x_ref, g_ref)
    bd = bd_ref[...]
    half = ROPE_DIM // 2
    for lo in range(0, w_ref.shape[1], tn):
        acc = jnp.dot(hn, w_ref[:, lo:lo + tn], preferred_element_type=F32)
        for c in range(tn // MXU_WIDTH):
            wide = acc[:, c * MXU_WIDTH:(c + 1) * MXU_WIDTH]
            ss_wide = jnp.dot((wide * wide).astype(BF16), bd, preferred_element_type=F32)
            for h in range(MXU_WIDTH // LANES):
                sl = slice(lo + c * MXU_WIDTH + h * LANES, lo + c * MXU_WIDTH + (h + 1) * LANES)
                a = wide[:, h * LANES:(h + 1) * LANES]
                ss = ss_wide[:, h * LANES:(h + 1) * LANES]
                y = a * lax.rsqrt(ss * (1.0 / HEAD_DIM) + RMS_EPS) * cg_ref[:, sl]
                y = (y * cos_ref[...]
                     + pltpu.roll(y, half, 1) * sa_ref[...]
                     + pltpu.roll(y, LANES - half, 1) * sb_ref[...])
                o_ref[:, sl] = y.astype(o_ref.dtype)


def _v_proj_kernel(x_ref, g_ref, w_ref, o_ref, *, tn):
    hn = _rmsnorm_bf16(x_ref, g_ref)
    for lo in range(0, w_ref.shape[1], tn):
        o_ref[:, lo:lo + tn] = jnp.dot(hn, w_ref[:, lo:lo + tn], preferred_element_type=F32).astype(o_ref.dtype)


def _norm_qk_proj(x2, gain, w, colgain, rope, bd, seq, out_dtype, tm, tn):
    m, d = x2.shape
    n = colgain.shape[1]
    cos_t, sa_t, sb_t = rope
    tiles_per_seq = seq // tm
    rope_spec = pl.BlockSpec((tm, LANES), lambda i: (i % tiles_per_seq, 0))
    return pl.pallas_call(
        functools.partial(_qk_proj_kernel, tn=tn),
        grid=(m // tm,),
        in_specs=[
            pl.BlockSpec((tm, d), lambda i: (i, 0)),
            pl.BlockSpec((1, d), lambda i: (0, 0)),
            pl.BlockSpec((d, n), lambda i: (0, 0), pipeline_mode=pl.Buffered(1)),
            pl.BlockSpec((1, n), lambda i: (0, 0)),
            rope_spec, rope_spec, rope_spec,
            pl.BlockSpec((MXU_WIDTH, MXU_WIDTH), lambda i: (0, 0)),
        ],
        out_specs=pl.BlockSpec((tm, n), lambda i: (i, 0)),
        out_shape=jax.ShapeDtypeStruct((m, n), out_dtype),
        compiler_params=_params(("parallel",)),
        name="norm_qk_proj",
    )(x2, gain, w, colgain, cos_t, sa_t, sb_t, bd)


def _norm_v_proj(x2, gain, w, n, out_dtype, tm, tn):
    m, d = x2.shape
    last = w.shape[1] // n - 1
    return pl.pallas_call(
        functools.partial(_v_proj_kernel, tn=tn),
        grid=(m // tm,),
        in_specs=[
            pl.BlockSpec((tm, d), lambda i: (i, 0)),
            pl.BlockSpec((1, d), lambda i: (0, 0)),
            pl.BlockSpec((d, n), lambda i: (0, last), pipeline_mode=pl.Buffered(1)),
        ],
        out_specs=pl.BlockSpec((tm, n), lambda i: (i, 0)),
        out_shape=jax.ShapeDtypeStruct((m, n), out_dtype),
        compiler_params=_params(("parallel",)),
        name="norm_v_proj",
    )(x2, gain, w)


def _dilated_kernel(q0, q1, q2, k0, k1, k2, v0, v1, v2, o_ref, num_ref, m_ref, den_ref, mask_ref, *, tq):
    t = pl.program_id(2)
    q_refs, k_refs, v_refs = (q0, q1, q2), (k0, k1, k2), (v0, v1, v2)
    blk = BAND_BLOCK

    lane = lax.broadcasted_iota(jnp.int32, (blk, LANES), 1)
    first_head = lane < HEAD_DIM
    ones_rhs = jnp.ones((2 * blk, LANES), BF16)

    reaches = sorted({w // d for w, d in DILATED_PAIRS})
    qi = lax.broadcasted_iota(jnp.int32, (2 * blk, 2 * blk), 0) % blk
    kj = lax.broadcasted_iota(jnp.int32, (2 * blk, 2 * blk), 1)
    dist = qi + blk - kj
    for n, reach in enumerate(reaches):
        band = (dist >= 0) & (dist <= reach)
        mask_ref[n, 0] = jnp.where(band, 0.0, MASK_VALUE)
        mask_ref[n, 1] = jnp.where(band & (kj >= blk), 0.0, MASK_VALUE)

    for g, (window, dil) in enumerate(DILATED_PAIRS):
        reach = window // dil
        span = blk * dil
        assert dil & (dil - 1) == 0 and reach <= blk
        qg, kg, vg = q_refs[g], k_refs[g], v_refs[g]
        mask_g = reaches.index(reach)

        def load_kv(base):
            if dil == 1:
                base = pl.multiple_of(base, blk)
            rows = pl.ds(base, blk, stride=dil)
            return kg[rows, :].astype(BF16), vg[rows, :].astype(BF16)

        for r in range(dil):
            prev = None
            for u in range(tq // span):
                qbase = u * span + r
                kbase = t * tq + qbase
                kc, vc = load_kv(kbase)
                if prev is None:
                    pbase = kbase - span
                    no_prev = (pbase < 0).astype(jnp.int32)
                    kp, vp = load_kv(jnp.maximum(pbase, 0))
                else:
                    no_prev = 0
                    kp, vp = prev
                prev = (kc, vc)
                q = qg[pl.ds(qbase, blk, stride=dil), :]
                zero = jnp.zeros_like(q)
                q2 = jnp.concatenate([jnp.where(first_head, q, zero),
                                      jnp.where(first_head, zero, q)], axis=0).astype(BF16)
                kcat = jnp.concatenate([kp, kc], axis=0)
                vcat = jnp.concatenate([jnp.concatenate([vp, vc], axis=0), ones_rhs], axis=1)
                s = lax.dot_general(q2, kcat, NT_DIMS, preferred_element_type=F32)
                s = s + mask_ref[mask_g, no_prev]
                m = jnp.max(s, axis=-1, keepdims=True)
                p = jnp.exp2(s - m).astype(BF16)
                ov = jnp.dot(p, vcat, preferred_element_type=F32)
                mb = jnp.broadcast_to(m, (2 * blk, LANES))
                rows = pl.ds(qbase, blk, stride=dil)
                num_ref[g, rows, :] = jnp.where(first_head, ov[:blk, :LANES], ov[blk:, :LANES])
                den_ref[g, rows, :] = jnp.where(first_head, ov[:blk, LANES:], ov[blk:, LANES:])
                m_ref[g, rows, :] = jnp.where(first_head, mb[:blk], mb[blk:])

    rows = 2 * blk

    def merge(c, carry):
        sl = pl.ds(pl.multiple_of(c * rows, rows), rows)
        ms = [m_ref[g, sl, :] for g in range(N_DIL)]
        mx = functools.reduce(jnp.maximum, ms)
        ws = [jnp.exp2(mg - mx) for mg in ms]
        num = sum(w * num_ref[g, sl, :] for g, w in enumerate(ws))
        den = sum(w * den_ref[g, sl, :] for g, w in enumerate(ws))
        o_ref[sl, :] = (num / den).astype(o_ref.dtype)
        return carry

    lax.fori_loop(0, tq // rows, merge, 0)


def _dilated_attention(qk, v, batch, seq):
    span_max = BAND_BLOCK * max(d for _, d in DILATED_PAIRS)
    tq = span_max
    assert seq % tq == 0
    slabs = N_HEADS // HEADS_PER_SLAB
    n_reach = len({w // d for w, d in DILATED_PAIRS})

    q_specs = [pl.BlockSpec((None, tq, LANES), lambda b, s, t, g=g: (b, t, g * slabs + s)) for g in range(N_DIL)]
    k_specs = [pl.BlockSpec((None, seq, LANES), lambda b, s, t, g=g: (b, 0, (N_DIL + g) * slabs + s))
               for g in range(N_DIL)]
    v_specs = [pl.BlockSpec((None, seq, LANES), lambda b, s, t, g=g: (b, 0, g * slabs + s)) for g in range(N_DIL)]
    stat = pltpu.VMEM((N_DIL, tq, LANES), F32)
    return pl.pallas_call(
        functools.partial(_dilated_kernel, tq=tq),
        grid=(batch, slabs, seq // tq),
        in_specs=q_specs + k_specs + v_specs,
        out_specs=pl.BlockSpec((None, tq, LANES), lambda b, s, t: (b, t, s)),
        out_shape=jax.ShapeDtypeStruct((batch, seq, N_HEADS * HEAD_DIM), BF16),
        scratch_shapes=[stat, stat, stat, pltpu.VMEM((n_reach, 2, 2 * BAND_BLOCK, 2 * BAND_BLOCK), F32)],
        compiler_params=_params(("parallel", "parallel", "arbitrary")),
        name="dilated_attention",
    )(*([qk] * 6 + [v] * 3))


def _moba_kernel(q_ref, k_ref, v_ref, oh_ref, mask_ref, o_ref, km_ref, kmhl_ref, vaug_ref, s_ref, m_ref, acc_ref,
                 *, nblk, chains):
    i = pl.program_id(2)
    blk = MOBA_BLOCK
    nrow = km_ref.shape[1]
    sup = 2 * blk
    lane = lax.broadcasted_iota(jnp.int32, (blk, LANES), 1)
    first_head = lane < HEAD_DIM
    cs = range(chains)

    def slab(c):
        return slice(c * LANES, (c + 1) * LANES)

    @pl.when(i == 0)
    def _():
        for c in cs:
            km_ref[c] = jnp.zeros(km_ref.shape[1:], F32)
            for j in range(nblk):
                rows = slice(j * blk, (j + 1) * blk)
                km_ref[c, j:j + 1, :] = jnp.sum(k_ref[rows, slab(c)].astype(F32), axis=0,
                                                keepdims=True) * (1.0 / blk)
                vj = v_ref[rows, slab(c)].astype(F32)
                vaug_ref[c, 0, rows, :] = jnp.where(first_head, vj, 1.0).astype(BF16)
                vaug_ref[c, 1, rows, :] = jnp.where(first_head, 1.0, vj).astype(BF16)
            km = km_ref[c]
            hi = km.astype(BF16)
            kmhl_ref[c] = jnp.concatenate([hi, (km - hi.astype(F32)).astype(BF16)], axis=0)

    row = lax.broadcasted_iota(jnp.int32, (nrow, 2 * blk), 0)
    rowf = row.astype(F32)

    def augmented_queries(c):
        q = q_ref[:, slab(c)]
        zero = jnp.zeros_like(q)
        q2 = jnp.concatenate([jnp.where(first_head, q, zero), jnp.where(first_head, zero, q)], axis=0)
        gate2 = lax.dot_general(kmhl_ref[c], q2, NT_DIMS, preferred_element_type=F32)
        gate = gate2[:nrow] + gate2[nrow:]
        remaining = row < i
        sel = row >= i
        for _ in range(MOBA_TOPK):
            gm = jnp.max(jnp.where(remaining, gate, -jnp.inf), axis=0, keepdims=True)
            cand = remaining & (gate == gm)
            first = jnp.min(jnp.where(cand, rowf, float(LANES)), axis=0, keepdims=True)
            pick = rowf == first
            sel = sel | pick
            remaining = remaining & jnp.logical_not(pick)
        bias_t = jnp.concatenate([jnp.where(sel, 0.0, MASK_VALUE),
                                  jnp.zeros((LANES - nrow, 2 * blk), F32)], axis=0)
        return jnp.concatenate([q2, bias_t.T.astype(BF16)], axis=1)

    q_aug = [augmented_queries(c) for c in cs]

    own_pair = lax.shift_right_logical(i, 1)
    own_base = own_pair * sup

    def scores(c, base):
        rows = pl.ds(pl.multiple_of(base, sup), sup)
        k_aug = jnp.concatenate([k_ref[rows, slab(c)], oh_ref[rows, :]], axis=1)
        return lax.dot_general(q_aug[c], k_aug, NT_DIMS, preferred_element_type=F32)

    def accumulate(c, buf, tile):
        base = jnp.where(tile == 0, own_base, (tile - 1) * sup)
        s = s_ref[c, buf]
        m_old = m_ref[c]
        m_new = jnp.maximum(m_old, jnp.max(s, axis=-1, keepdims=True))
        alpha = jnp.exp2(m_old - m_new)
        p = jnp.exp2(s - jnp.concatenate([m_new] * (sup // LANES), axis=1)).astype(BF16)
        rows = pl.ds(pl.multiple_of(base, sup), sup)
        pv = jnp.concatenate([jnp.dot(p[:blk], vaug_ref[c, 0, rows, :], preferred_element_type=F32),
                              jnp.dot(p[blk:], vaug_ref[c, 1, rows, :], preferred_element_type=F32)], axis=0)
        acc_ref[c] = acc_ref[c] * alpha + pv
        m_ref[c] = m_new

    own_mask = mask_ref[lax.bitwise_and(i, 1)]
    for c in cs:
        s_ref[c, 0] = scores(c, own_base) + own_mask
        m_ref[c] = jnp.full(m_ref.shape[1:], MASK_VALUE, F32)
        acc_ref[c] = jnp.zeros(acc_ref.shape[1:], F32)

    def advance(dst, src, tile):
        for c in cs:
            s_ref[c, dst] = scores(c, (tile - 1) * sup)
        for c in cs:
            accumulate(c, src, tile - 1)

    def body(n, carry):
        advance(1, 0, 2 * n + 1)
        advance(0, 1, 2 * n + 2)
        return carry

    lax.fori_loop(0, lax.shift_right_logical(own_pair, 1), body, 0)

    @pl.when(lax.bitwise_and(own_pair, 1) == 1)
    def _():
        advance(1, 0, own_pair)
        for c in cs:
            accumulate(c, 1, own_pair)

    @pl.when(lax.bitwise_and(own_pair, 1) == 0)
    def _():
        for c in cs:
            accumulate(c, 0, own_pair)

    for c in cs:
        acc = acc_ref[c]
        o = acc / pltpu.roll(acc, HEAD_DIM, 1)
        o_ref[:, slab(c)] = jnp.where(first_head, o[:blk], o[blk:]).astype(o_ref.dtype)


def _moba_attention(qk, v, batch, seq):
    assert seq % (2 * MOBA_BLOCK) == 0
    nblk = seq // MOBA_BLOCK
    assert nblk <= LANES
    nrow = -(-nblk // BF16_SUBLANES) * BF16_SUBLANES
    slabs = N_HEADS * HEAD_DIM // LANES
    blk = MOBA_BLOCK
    block_id = jnp.arange(seq, dtype=jnp.int32) // blk
    onehot = (block_id[:, None] == jnp.arange(LANES, dtype=jnp.int32)[None, :]).astype(BF16)
    qi = jnp.arange(2 * blk, dtype=jnp.int32)[:, None] % blk
    kj = jnp.arange(2 * blk, dtype=jnp.int32)[None, :]
    own_mask = jnp.stack([jnp.where(kj - qi <= parity * blk, 0.0, MASK_VALUE) for parity in (0, 1)]).astype(F32)
    chains = MOBA_CHAINS
    assert slabs % chains == 0
    width = chains * LANES
    return pl.pallas_call(
        functools.partial(_moba_kernel, nblk=nblk, chains=chains),
        grid=(batch, slabs // chains, nblk),
        in_specs=[
            pl.BlockSpec((None, blk, width), lambda b, s, i: (b, i, s)),
            pl.BlockSpec((None, seq, width), lambda b, s, i: (b, 0, slabs // chains + s)),
            pl.BlockSpec((None, seq, width), lambda b, s, i: (b, 0, s)),
            pl.BlockSpec((seq, LANES), lambda b, s, i: (0, 0)),
            pl.BlockSpec((2, 2 * blk, 2 * blk), lambda b, s, i: (0, 0, 0)),
        ],
        out_specs=pl.BlockSpec((None, blk, width), lambda b, s, i: (b, i, s)),
        out_shape=jax.ShapeDtypeStruct((batch, seq, N_HEADS * HEAD_DIM), BF16),
        scratch_shapes=[
            pltpu.VMEM((chains, nrow, LANES), F32), pltpu.VMEM((chains, 2 * nrow, LANES), BF16),
            pltpu.VMEM((chains, HEADS_PER_SLAB, seq, LANES), BF16),
            pltpu.VMEM((chains, 2, 2 * blk, 2 * blk), F32),
            pltpu.VMEM((chains, 2 * blk, LANES), F32), pltpu.VMEM((chains, 2 * blk, LANES), F32),
        ],
        compiler_params=_params(("parallel", "parallel", "arbitrary")),
        name="moba_attention",
    )(qk, qk, v, onehot, own_mask)


def _ffn_kernel(x_ref, xh_ref, a_ref, ah_ref, wo_ref, g_ref, wu_ref, cw_ref, cb_ref, wd_ref, o_ref,
                *, tiles_per_seq, tf, group):
    i = pl.program_id(0)
    halo = CONV_HALO
    d_ff = wd_ref.shape[0]

    def rms(x):
        ms = jnp.mean(x * x, axis=-1, keepdims=True)
        return x * lax.rsqrt(ms + RMS_EPS) * g_ref[...]

    wo = wo_ref[...]
    x1 = x_ref[...] + jnp.dot(a_ref[...], wo, preferred_element_type=F32)
    xh1 = xh_ref[...] + jnp.dot(ah_ref[...], wo, preferred_element_type=F32)[ah_ref.shape[0] - halo:]

    keep = (i % tiles_per_seq != 0).astype(F32)
    hn = jnp.concatenate([(rms(xh1) * keep).astype(BF16), rms(x1).astype(BF16)], axis=0)

    def conv(col):
        u = jnp.dot(hn, wu_ref[:, col:col + tf], preferred_element_type=F32)
        cw = cw_ref[:, col:col + tf]
        return (cb_ref[:, col:col + tf]
                + cw[0:1, :] * pltpu.roll(u, 2, 0)[halo:]
                + cw[1:2, :] * pltpu.roll(u, 1, 0)[halo:]
                + cw[2:3, :] * u[halo:])

    acc = x1
    for lo in range(0, d_ff, group * tf):
        hi = min(lo + group * tf, d_ff)
        acts = []
        for col in range(lo, hi, tf):
            gate = conv(col)
            val = conv(d_ff + col)
            acts.append((gate * jax.nn.sigmoid(gate) * val).astype(BF16))
        acc = acc + jnp.dot(jnp.concatenate(acts, axis=1), wd_ref[lo:hi, :], preferred_element_type=F32)
    o_ref[...] = acc


def _attn_out_conv_ffn(x2, attn2, w_o, gain, w_up, conv_w, conv_b, w_down, seq, tm, tf, group):
    m, d = x2.shape
    da = attn2.shape[1]
    d_ff = w_down.shape[0]
    assert d_ff % tf == 0
    halo_blocks = tm // CONV_HALO
    attn_halo = BF16_SUBLANES
    attn_halo_blocks = tm // attn_halo
    resident = dict(pipeline_mode=pl.Buffered(1))
    return pl.pallas_call(
        functools.partial(_ffn_kernel, tiles_per_seq=seq // tm, tf=tf, group=group),
        grid=(m // tm,),
        in_specs=[
            pl.BlockSpec((tm, d), lambda i: (i, 0)),
            pl.BlockSpec((CONV_HALO, d), lambda i: (jnp.maximum(i * halo_blocks - 1, 0), 0)),
            pl.BlockSpec((tm, da), lambda i: (i, 0)),
            pl.BlockSpec((attn_halo, da), lambda i: (jnp.maximum(i * attn_halo_blocks - 1, 0), 0)),
            pl.BlockSpec((da, d), lambda i: (0, 0), **resident),
            pl.BlockSpec((1, d), lambda i: (0, 0)),
            pl.BlockSpec((d, 2 * d_ff), lambda i: (0, 0), **resident),
            pl.BlockSpec((CONV_WIDTH, 2 * d_ff), lambda i: (0, 0)),
            pl.BlockSpec((1, 2 * d_ff), lambda i: (0, 0)),
            pl.BlockSpec((d_ff, d), lambda i: (0, 0), **resident),
        ],
        out_specs=pl.BlockSpec((tm, d), lambda i: (i, 0)),
        out_shape=jax.ShapeDtypeStruct((m, d), F32),
        compiler_params=_params(("parallel",)),
        name="attn_out_conv_ffn",
    )(x2, x2, attn2, attn2, w_o, gain, w_up, conv_w, conv_b, w_down)


def _rope_tables(seq):
    pos = jnp.arange(seq, dtype=F32)
    inv_freq = ROPE_THETA ** (-jnp.arange(0, ROPE_DIM, 2, dtype=F32) / ROPE_DIM)
    ang = pos[:, None] * inv_freq[None, :]
    cos, sin = jnp.cos(ang), jnp.sin(ang)
    half = ROPE_DIM // 2
    pad = jnp.zeros((seq, HEAD_DIM - ROPE_DIM), F32)
    zeros = jnp.zeros((seq, half), F32)
    cos_h = jnp.concatenate([cos, cos, pad + 1.0], axis=1)
    sa_h = jnp.concatenate([zeros, sin, pad], axis=1)
    sb_h = jnp.concatenate([-sin, zeros, pad], axis=1)
    rep = LANES // HEAD_DIM
    return tuple(jnp.tile(t, (1, rep)) for t in (cos_h, sa_h, sb_h))


def _block_diag_ones():
    r = jnp.arange(MXU_WIDTH) // HEAD_DIM
    return (r[:, None] == r[None, :]).astype(BF16)


def kernel(x, attn_norm, a_w_qkv, a_q_norm, a_k_norm, a_w_o, b_w_qkv, b_q_norm, b_k_norm, b_w_o,
           ffn_norm, ffn_w_up, ffn_conv_w, ffn_conv_b, ffn_w_down):
    batch, seq, d_model = x.shape
    depth = attn_norm.shape[0]
    hd_all = N_HEADS * HEAD_DIM
    rope = _rope_tables(seq)
    bd = _block_diag_ones()
    x2 = x.reshape(batch * seq, d_model)
    tm = 512
    q_scale = ATTN_SCALE * LOG2_E

    for layer in range(depth):
        j = layer // 2
        gain = attn_norm[layer][None, :]
        dilated = layer % 2 == 0
        if dilated:
            n_groups = N_DIL
            qg = jnp.tile(a_q_norm[j][:, None, :], (1, N_HEADS, 1)).reshape(-1) * q_scale
            kg = jnp.tile(a_k_norm[j][:, None, :], (1, N_HEADS, 1)).reshape(-1)
            w_qkv, w_o = a_w_qkv[j], a_w_o[j]
        else:
            n_groups = 1
            qg = jnp.tile(b_q_norm[j], N_HEADS) * q_scale
            kg = jnp.tile(b_k_norm[j], N_HEADS)
            w_qkv, w_o = b_w_qkv[j], b_w_o[j]
        n_qk = 2 * n_groups * hd_all
        colgain = jnp.concatenate([qg, kg])[None, :]
        w_qkv = w_qkv.astype(BF16)
        act_dtype = F32 if dilated else BF16
        qk = _norm_qk_proj(x2, gain, w_qkv, colgain, rope, bd, seq, act_dtype, tm, 512)
        v = _norm_v_proj(x2, gain, w_qkv, n_qk // 2, act_dtype, tm, 512)
        qk, v = qk.reshape(batch, seq, -1), v.reshape(batch, seq, -1)
        attn = _dilated_attention(qk, v, batch, seq) if dilated else _moba_attention(qk, v, batch, seq)
        x2 = _attn_out_conv_ffn(x2, attn.reshape(batch * seq, hd_all), w_o.astype(BF16),
                                ffn_norm[layer][None, :], ffn_w_up[layer].astype(BF16),
                                ffn_conv_w[layer], ffn_conv_b[layer][None, :],
                                ffn_w_down[layer].astype(BF16), seq, 1024, 256, 11)
    return x2.reshape(batch, seq, d_model)
```

```python
import functools
import math

import jax
import jax.numpy as jnp
from jax import lax
from jax.experimental import pallas as pl
from jax.experimental.pallas import tpu as pltpu

N_HEADS = 16
HEAD_DIM = 64
ROPE_DIM = HEAD_DIM // 4
ROPE_THETA = 500000.0
ATTN_SCALE = HEAD_DIM ** -0.5
DILATED_PAIRS = ((128, 1), (512, 4), (2048, 16))
N_DIL = len(DILATED_PAIRS)
BAND_BLOCK = 128
MOBA_BLOCK = 256
MOBA_TOPK = 3
CONV_WIDTH = 3
RMS_EPS = 1e-6

LANES = 128
BF16_SUBLANES = 16
MXU_WIDTH = 256
HEADS_PER_SLAB = LANES // HEAD_DIM
MASK_VALUE = -1e30
MOBA_CHAINS = 4
CONV_HALO = 8
VMEM_LIMIT = 56 * 1024 * 1024
LOG2_E = math.log2(math.e)

F32 = jnp.float32
BF16 = jnp.bfloat16
NT_DIMS = (((1,), (1,)), ((), ()))


def _params(semantics):
    return pltpu.CompilerParams(dimension_semantics=semantics, vmem_limit_bytes=VMEM_LIMIT)


def _rmsnorm_bf16(x_ref, g_ref):
    x = x_ref[...]
    ms = jnp.mean(x * x, axis=-1, keepdims=True)
    return (x * lax.rsqrt(ms + RMS_EPS) * g_ref[...]).astype(BF16)


def _qk_proj_kernel(x_ref, g_ref, w_ref, cg_ref, cos_ref, sa_ref, sb_ref, bd_ref, o_ref, *, tn):
    hn = _rmsnorm_bf16(x_ref, g_ref)
    bd = bd_ref[...]
    half = ROPE_DIM // 2
    for lo in range(0, w_ref.shape[1], tn):
        acc = jnp.dot(hn, w_ref[:, lo:lo + tn], preferred_element_type=F32)
        for c in range(tn // MXU_WIDTH):
            wide = acc[:, c * MXU_WIDTH:(c + 1) * MXU_WIDTH]
            ss_wide = jnp.dot((wide * wide).astype(BF16), bd, preferred_element_type=F32)
            for h in range(MXU_WIDTH // LANES):
                sl = slice(lo + c * MXU_WIDTH + h * LANES, lo + c * MXU_WIDTH + (h + 1) * LANES)
                a = wide[:, h * LANES:(h + 1) * LANES]
                ss = ss_wide[:, h * LANES:(h + 1) * LANES]
                y = a * lax.rsqrt(ss * (1.0 / HEAD_DIM) + RMS_EPS) * cg_ref[:, sl]
                y = (y * cos_ref[...]
                     + pltpu.roll(y, half, 1) * sa_ref[...]
                     + pltpu.roll(y, LANES - half, 1) * sb_ref[...])
                o_ref[:, sl] = y.astype(o_ref.dtype)


def _v_proj_kernel(x_ref, g_ref, w_ref, o_ref, *, tn):
    hn = _rmsnorm_bf16(x_ref, g_ref)
    for lo in range(0, w_ref.shape[1], tn):
        o_ref[:, lo:lo + tn] = jnp.dot(hn, w_ref[:, lo:lo + tn], preferred_element_type=F32).astype(o_ref.dtype)


def _norm_qk_proj(x2, gain, w, colgain, rope, bd, seq, out_dtype, tm, tn):
    m, d = x2.shape
    n = colgain.shape[1]
    cos_t, sa_t, sb_t = rope
    tiles_per_seq = seq // tm
    rope_spec = pl.BlockSpec((tm, LANES), lambda i: (i % tiles_per_seq, 0))
    return pl.pallas_call(
        functools.partial(_qk_proj_kernel, tn=tn),
        grid=(m // tm,),
        in_specs=[
            pl.BlockSpec((tm, d), lambda i: (i, 0)),
            pl.BlockSpec((1, d), lambda i: (0, 0)),
            pl.BlockSpec((d, n), lambda i: (0, 0), pipeline_mode=pl.Buffered(1)),
            pl.BlockSpec((1, n), lambda i: (0, 0)),
            rope_spec, rope_spec, rope_spec,
            pl.BlockSpec((MXU_WIDTH, MXU_WIDTH), lambda i: (0, 0)),
        ],
        out_specs=pl.BlockSpec((tm, n), lambda i: (i, 0)),
        out_shape=jax.ShapeDtypeStruct((m, n), out_dtype),
        compiler_params=_params(("parallel",)),
        name="norm_qk_proj",
    )(x2, gain, w, colgain, cos_t, sa_t, sb_t, bd)


def _norm_v_proj(x2, gain, w, n, out_dtype, tm, tn):
    m, d = x2.shape
    last = w.shape[1] // n - 1
    return pl.pallas_call(
        functools.partial(_v_proj_kernel, tn=tn),
        grid=(m // tm,),
        in_specs=[
            pl.BlockSpec((tm, d), lambda i: (i, 0)),
            pl.BlockSpec((1, d), lambda i: (0, 0)),
            pl.BlockSpec((d, n), lambda i: (0, last), pipeline_mode=pl.Buffered(1)),
        ],
        out_specs=pl.BlockSpec((tm, n), lambda i: (i, 0)),
        out_shape=jax.ShapeDtypeStruct((m, n), out_dtype),
        compiler_params=_params(("parallel",)),
        name="norm_v_proj",
    )(x2, gain, w)


def _dilated_kernel(q0, q1, q2, k0, k1, k2, v0, v1, v2, o_ref, num_ref, m_ref, den_ref, mask_ref,
                    qf_ref, kf_ref, vf_ref, *, tq):
    t = pl.program_id(2)
    q_refs, k_refs, v_refs = (q0, q1, q2), (k0, k1, k2), (v0, v1, v2)
    blk = BAND_BLOCK

    strided = [g for g, (_, d) in enumerate(DILATED_PAIRS) if d > 1]
    chunk = 2 * blk

    def widen(dst_ref, n, src_ref):
        def body(c, carry):
            sl = pl.ds(pl.multiple_of(c * chunk, chunk), chunk)
            dst_ref[n, sl, :] = src_ref[sl, :].astype(F32)
            return carry
        lax.fori_loop(0, src_ref.shape[0] // chunk, body, 0)

    @pl.when(t == 0)
    def _():
        for n, g in enumerate(strided):
            widen(kf_ref, n, k_refs[g])
            widen(vf_ref, n, v_refs[g])

    for n, g in enumerate(strided):
        widen(qf_ref, n, q_refs[g])

    lane = lax.broadcasted_iota(jnp.int32, (blk, LANES), 1)
    first_head = lane < HEAD_DIM
    ones_rhs = jnp.ones((2 * blk, LANES), BF16)

    reaches = sorted({w // d for w, d in DILATED_PAIRS})
    qi = lax.broadcasted_iota(jnp.int32, (2 * blk, 2 * blk), 0) % blk
    kj = lax.broadcasted_iota(jnp.int32, (2 * blk, 2 * blk), 1)
    dist = qi + blk - kj
    for n, reach in enumerate(reaches):
        band = (dist >= 0) & (dist <= reach)
        mask_ref[n, 0] = jnp.where(band, 0.0, MASK_VALUE)
        mask_ref[n, 1] = jnp.where(band & (kj >= blk), 0.0, MASK_VALUE)

    for g, (window, dil) in enumerate(DILATED_PAIRS):
        reach = window // dil
        span = blk * dil
        assert dil & (dil - 1) == 0 and reach <= blk
        mask_g = reaches.index(reach)
        if dil == 1:
            def load_q(base):
                return q_refs[g][pl.ds(base, blk), :]

            def load_kv(base):
                rows = pl.ds(pl.multiple_of(base, blk), blk)
                return k_refs[g][rows, :], v_refs[g][rows, :]
        else:
            n = strided.index(g)

            def load_q(base, n=n, dil=dil):
                return qf_ref[n, pl.ds(base, blk, stride=dil), :]

            def load_kv(base, n=n, dil=dil):
                rows = pl.ds(base, blk, stride=dil)
                return kf_ref[n, rows, :].astype(BF16), vf_ref[n, rows, :].astype(BF16)

        for r in range(dil):
            prev = None
            for u in range(tq // span):
                qbase = u * span + r
                kbase = t * tq + qbase
                kc, vc = load_kv(kbase)
                if prev is None:
                    pbase = kbase - span
                    no_prev = (pbase < 0).astype(jnp.int32)
                    kp, vp = load_kv(jnp.maximum(pbase, 0))
                else:
                    no_prev = 0
                    kp, vp = prev
                prev = (kc, vc)
                q = load_q(qbase)
                zero = jnp.zeros_like(q)
                q2 = jnp.concatenate([jnp.where(first_head, q, zero),
                                      jnp.where(first_head, zero, q)], axis=0).astype(BF16)
                kcat = jnp.concatenate([kp, kc], axis=0)
                vcat = jnp.concatenate([jnp.concatenate([vp, vc], axis=0), ones_rhs], axis=1)
                s = lax.dot_general(q2, kcat, NT_DIMS, preferred_element_type=F32)
                s = s + mask_ref[mask_g, no_prev]
                m = jnp.max(s, axis=-1, keepdims=True)
                p = jnp.exp2(s - m).astype(BF16)
                ov = jnp.dot(p, vcat, preferred_element_type=F32)
                mb = jnp.broadcast_to(m, (2 * blk, LANES))
                rows = pl.ds(qbase, blk, stride=dil)
                num_ref[g, rows, :] = jnp.where(first_head, ov[:blk, :LANES], ov[blk:, :LANES])
                den_ref[g, rows, :] = jnp.where(first_head, ov[:blk, LANES:], ov[blk:, LANES:])
                m_ref[g, rows, :] = jnp.where(first_head, mb[:blk], mb[blk:])

    rows = 2 * blk

    def merge(c, carry):
        sl = pl.ds(pl.multiple_of(c * rows, rows), rows)
        ms = [m_ref[g, sl, :] for g in range(N_DIL)]
        mx = functools.reduce(jnp.maximum, ms)
        ws = [jnp.exp2(mg - mx) for mg in ms]
        num = sum(w * num_ref[g, sl, :] for g, w in enumerate(ws))
        den = sum(w * den_ref[g, sl, :] for g, w in enumerate(ws))
        o_ref[sl, :] = (num / den).astype(o_ref.dtype)
        return carry

    lax.fori_loop(0, tq // rows, merge, 0)


def _dilated_attention(qk, v, batch, seq):
    span_max = BAND_BLOCK * max(d for _, d in DILATED_PAIRS)
    tq = span_max
    assert seq % tq == 0
    slabs = N_HEADS // HEADS_PER_SLAB
    n_reach = len({w // d for w, d in DILATED_PAIRS})
    n_strided = sum(d > 1 for _, d in DILATED_PAIRS)

    q_specs = [pl.BlockSpec((None, tq, LANES), lambda b, s, t, g=g: (b, t, g * slabs + s)) for g in range(N_DIL)]
    k_specs = [pl.BlockSpec((None, seq, LANES), lambda b, s, t, g=g: (b, 0, (N_DIL + g) * slabs + s))
               for g in range(N_DIL)]
    v_specs = [pl.BlockSpec((None, seq, LANES), lambda b, s, t, g=g: (b, 0, g * slabs + s)) for g in range(N_DIL)]
    stat = pltpu.VMEM((N_DIL, tq, LANES), F32)
    return pl.pallas_call(
        functools.partial(_dilated_kernel, tq=tq),
        grid=(batch, slabs, seq // tq),
        in_specs=q_specs + k_specs + v_specs,
        out_specs=pl.BlockSpec((None, tq, LANES), lambda b, s, t: (b, t, s)),
        out_shape=jax.ShapeDtypeStruct((batch, seq, N_HEADS * HEAD_DIM), BF16),
        scratch_shapes=[stat, stat, stat, pltpu.VMEM((n_reach, 2, 2 * BAND_BLOCK, 2 * BAND_BLOCK), F32),
                        pltpu.VMEM((n_strided, tq, LANES), F32),
                        pltpu.VMEM((n_strided, seq, LANES), F32), pltpu.VMEM((n_strided, seq, LANES), F32)],
        compiler_params=_params(("parallel", "parallel", "arbitrary")),
        name="dilated_attention",
    )(*([qk] * 6 + [v] * 3))


def _moba_kernel(q_ref, k_ref, v_ref, oh_ref, mask_ref, o_ref, km_ref, kmhl_ref, vaug_ref, s_ref, m_ref, acc_ref,
                 *, nblk, chains):
    i = pl.program_id(2)
    blk = MOBA_BLOCK
    nrow = km_ref.shape[1]
    sup = 2 * blk
    lane = lax.broadcasted_iota(jnp.int32, (blk, LANES), 1)
    first_head = lane < HEAD_DIM
    cs = range(chains)

    def slab(c):
        return slice(c * LANES, (c + 1) * LANES)

    @pl.when(i == 0)
    def _():
        for c in cs:
            km_ref[c] = jnp.zeros(km_ref.shape[1:], F32)
            for j in range(nblk):
                rows = slice(j * blk, (j + 1) * blk)
                km_ref[c, j:j + 1, :] = jnp.sum(k_ref[rows, slab(c)].astype(F32), axis=0,
                                                keepdims=True) * (1.0 / blk)
                vj = v_ref[rows, slab(c)].astype(F32)
                vaug_ref[c, 0, rows, :] = jnp.where(first_head, vj, 1.0).astype(BF16)
                vaug_ref[c, 1, rows, :] = jnp.where(first_head, 1.0, vj).astype(BF16)
            km = km_ref[c]
            hi = km.astype(BF16)
            kmhl_ref[c] = jnp.concatenate([hi, (km - hi.astype(F32)).astype(BF16)], axis=0)

    row = lax.broadcasted_iota(jnp.int32, (nrow, 2 * blk), 0)
    rowf = row.astype(F32)

    def augmented_queries(c):
        q = q_ref[:, slab(c)]
        zero = jnp.zeros_like(q)
        q2 = jnp.concatenate([jnp.where(first_head, q, zero), jnp.where(first_head, zero, q)], axis=0)
        gate2 = lax.dot_general(kmhl_ref[c], q2, NT_DIMS, preferred_element_type=F32)
        gate = gate2[:nrow] + gate2[nrow:]
        remaining = row < i
        sel = row >= i
        for _ in range(MOBA_TOPK):
            gm = jnp.max(jnp.where(remaining, gate, -jnp.inf), axis=0, keepdims=True)
            cand = remaining & (gate == gm)
            first = jnp.min(jnp.where(cand, rowf, float(LANES)), axis=0, keepdims=True)
            pick = rowf == first
            sel = sel | pick
            remaining = remaining & jnp.logical_not(pick)
        bias_t = jnp.concatenate([jnp.where(sel, 0.0, MASK_VALUE),
                                  jnp.zeros((LANES - nrow, 2 * blk), F32)], axis=0)
        return jnp.concatenate([q2, bias_t.T.astype(BF16)], axis=1)

    q_aug = [augmented_queries(c) for c in cs]

    own_pair = lax.shift_right_logical(i, 1)
    own_base = own_pair * sup

    def scores(c, base):
        rows = pl.ds(pl.multiple_of(base, sup), sup)
        k_aug = jnp.concatenate([k_ref[rows, slab(c)], oh_ref[rows, :]], axis=1)
        return lax.dot_general(q_aug[c], k_aug, NT_DIMS, preferred_element_type=F32)

    def accumulate(c, buf, tile):
        base = jnp.where(tile == 0, own_base, (tile - 1) * sup)
        s = s_ref[c, buf]
        m_old = m_ref[c]
        m_new = jnp.maximum(m_old, jnp.max(s, axis=-1, keepdims=True))
        alpha = jnp.exp2(m_old - m_new)
        p = jnp.exp2(s - jnp.concatenate([m_new] * (sup // LANES), axis=1)).astype(BF16)
        rows = pl.ds(pl.multiple_of(base, sup), sup)
        pv = jnp.concatenate([jnp.dot(p[:blk], vaug_ref[c, 0, rows, :], preferred_element_type=F32),
                              jnp.dot(p[blk:], vaug_ref[c, 1, rows, :], preferred_element_type=F32)], axis=0)
        acc_ref[c] = acc_ref[c] * alpha + pv
        m_ref[c] = m_new

    own_mask = mask_ref[lax.bitwise_and(i, 1)]
    for c in cs:
        s_ref[c, 0] = scores(c, own_base) + own_mask
        m_ref[c] = jnp.full(m_ref.shape[1:], MASK_VALUE, F32)
        acc_ref[c] = jnp.zeros(acc_ref.shape[1:], F32)

    def advance(dst, src, tile):
        for c in cs:
            s_ref[c, dst] = scores(c, (tile - 1) * sup)
        for c in cs:
            accumulate(c, src, tile - 1)

    def body(n, carry):
        advance(1, 0, 2 * n + 1)
        advance(0, 1, 2 * n + 2)
        return carry

    lax.fori_loop(0, lax.shift_right_logical(own_pair, 1), body, 0)

    @pl.when(lax.bitwise_and(own_pair, 1) == 1)
    def _():
        advance(1, 0, own_pair)
        for c in cs:
            accumulate(c, 1, own_pair)

    @pl.when(lax.bitwise_and(own_pair, 1) == 0)
    def _():
        for c in cs:
            accumulate(c, 0, own_pair)

    for c in cs:
        acc = acc_ref[c]
        o = acc / pltpu.roll(acc, HEAD_DIM, 1)
        o_ref[:, slab(c)] = jnp.where(first_head, o[:blk], o[blk:]).astype(o_ref.dtype)


def _moba_attention(qk, v, batch, seq):
    assert seq % (2 * MOBA_BLOCK) == 0
    nblk = seq // MOBA_BLOCK
    assert nblk <= LANES
    nrow = -(-nblk // BF16_SUBLANES) * BF16_SUBLANES
    slabs = N_HEADS * HEAD_DIM // LANES
    blk = MOBA_BLOCK
    block_id = jnp.arange(seq, dtype=jnp.int32) // blk
    onehot = (block_id[:, None] == jnp.arange(LANES, dtype=jnp.int32)[None, :]).astype(BF16)
    qi = jnp.arange(2 * blk, dtype=jnp.int32)[:, None] % blk
    kj = jnp.arange(2 * blk, dtype=jnp.int32)[None, :]
    own_mask = jnp.stack([jnp.where(kj - qi <= parity * blk, 0.0, MASK_VALUE) for parity in (0, 1)]).astype(F32)
    chains = MOBA_CHAINS
    assert slabs % chains == 0
    width = chains * LANES
    return pl.pallas_call(
        functools.partial(_moba_kernel, nblk=nblk, chains=chains),
        grid=(batch, slabs // chains, nblk),
        in_specs=[
            pl.BlockSpec((None, blk, width), lambda b, s, i: (b, i, s)),
            pl.BlockSpec((None, seq, width), lambda b, s, i: (b, 0, slabs // chains + s)),
            pl.BlockSpec((None, seq, width), lambda b, s, i: (b, 0, s)),
            pl.BlockSpec((seq, LANES), lambda b, s, i: (0, 0)),
            pl.BlockSpec((2, 2 * blk, 2 * blk), lambda b, s, i: (0, 0, 0)),
        ],
        out_specs=pl.BlockSpec((None, blk, width), lambda b, s, i: (b, i, s)),
        out_shape=jax.ShapeDtypeStruct((batch, seq, N_HEADS * HEAD_DIM), BF16),
        scratch_shapes=[
            pltpu.VMEM((chains, nrow, LANES), F32), pltpu.VMEM((chains, 2 * nrow, LANES), BF16),
            pltpu.VMEM((chains, HEADS_PER_SLAB, seq, LANES), BF16),
            pltpu.VMEM((chains, 2, 2 * blk, 2 * blk), F32),
            pltpu.VMEM((chains, 2 * blk, LANES), F32), pltpu.VMEM((chains, 2 * blk, LANES), F32),
        ],
        compiler_params=_params(("parallel", "parallel", "arbitrary")),
        name="moba_attention",
    )(qk, qk, v, onehot, own_mask)


def _ffn_kernel(x_ref, xh_ref, a_ref, ah_ref, wo_ref, g_ref, wu_ref, cw_ref, cb_ref, wd_ref, o_ref,
                *, tiles_per_seq, tf, group):
    i = pl.program_id(0)
    halo = CONV_HALO
    d_ff = wd_ref.shape[0]

    def rms(x):
        ms = jnp.mean(x * x, axis=-1, keepdims=True)
        return x * lax.rsqrt(ms + RMS_EPS) * g_ref[...]

    wo = wo_ref[...]
    x1 = x_ref[...] + jnp.dot(a_ref[...], wo, preferred_element_type=F32)
    xh1 = xh_ref[...] + jnp.dot(ah_ref[...], wo, preferred_element_type=F32)[ah_ref.shape[0] - halo:]

    keep = (i % tiles_per_seq != 0).astype(F32)
    hn = jnp.concatenate([(rms(xh1) * keep).astype(BF16), rms(x1).astype(BF16)], axis=0)

    def conv(col):
        u = jnp.dot(hn, wu_ref[:, col:col + tf], preferred_element_type=F32)
        cw = cw_ref[:, col:col + tf]
        return (cb_ref[:, col:col + tf]
                + cw[0:1, :] * pltpu.roll(u, 2, 0)[halo:]
                + cw[1:2, :] * pltpu.roll(u, 1, 0)[halo:]
                + cw[2:3, :] * u[halo:])

    acc = x1
    for lo in range(0, d_ff, group * tf):
        hi = min(lo + group * tf, d_ff)
        acts = []
        for col in range(lo, hi, tf):
            gate = conv(col)
            val = conv(d_ff + col)
            acts.append((gate * jax.nn.sigmoid(gate) * val).astype(BF16))
        acc = acc + jnp.dot(jnp.concatenate(acts, axis=1), wd_ref[lo:hi, :], preferred_element_type=F32)
    o_ref[...] = acc


def _attn_out_conv_ffn(x2, attn2, w_o, gain, w_up, conv_w, conv_b, w_down, seq, tm, tf, group):
    m, d = x2.shape
    da = attn2.shape[1]
    d_ff = w_down.shape[0]
    assert d_ff % tf == 0
    halo_blocks = tm // CONV_HALO
    attn_halo = BF16_SUBLANES
    attn_halo_blocks = tm // attn_halo
    resident = dict(pipeline_mode=pl.Buffered(1))
    return pl.pallas_call(
        functools.partial(_ffn_kernel, tiles_per_seq=seq // tm, tf=tf, group=group),
        grid=(m // tm,),
        in_specs=[
            pl.BlockSpec((tm, d), lambda i: (i, 0)),
            pl.BlockSpec((CONV_HALO, d), lambda i: (jnp.maximum(i * halo_blocks - 1, 0), 0)),
            pl.BlockSpec((tm, da), lambda i: (i, 0)),
            pl.BlockSpec((attn_halo, da), lambda i: (jnp.maximum(i * attn_halo_blocks - 1, 0), 0)),
            pl.BlockSpec((da, d), lambda i: (0, 0), **resident),
            pl.BlockSpec((1, d), lambda i: (0, 0)),
            pl.BlockSpec((d, 2 * d_ff), lambda i: (0, 0), **resident),
            pl.BlockSpec((CONV_WIDTH, 2 * d_ff), lambda i: (0, 0)),
            pl.BlockSpec((1, 2 * d_ff), lambda i: (0, 0)),
            pl.BlockSpec((d_ff, d), lambda i: (0, 0), **resident),
        ],
        out_specs=pl.BlockSpec((tm, d), lambda i: (i, 0)),
        out_shape=jax.ShapeDtypeStruct((m, d), F32),
        compiler_params=_params(("parallel",)),
        name="attn_out_conv_ffn",
    )(x2, x2, attn2, attn2, w_o, gain, w_up, conv_w, conv_b, w_down)


def _rope_tables(seq):
    pos = jnp.arange(seq, dtype=F32)
    inv_freq = ROPE_THETA ** (-jnp.arange(0, ROPE_DIM, 2, dtype=F32) / ROPE_DIM)
    ang = pos[:, None] * inv_freq[None, :]
    cos, sin = jnp.cos(ang), jnp.sin(ang)
    half = ROPE_DIM // 2
    pad = jnp.zeros((seq, HEAD_DIM - ROPE_DIM), F32)
    zeros = jnp.zeros((seq, half), F32)
    cos_h = jnp.concatenate([cos, cos, pad + 1.0], axis=1)
    sa_h = jnp.concatenate([zeros, sin, pad], axis=1)
    sb_h = jnp.concatenate([-sin, zeros, pad], axis=1)
    rep = LANES // HEAD_DIM
    return tuple(jnp.tile(t, (1, rep)) for t in (cos_h, sa_h, sb_h))


def _block_diag_ones():
    r = jnp.arange(MXU_WIDTH) // HEAD_DIM
    return (r[:, None] == r[None, :]).astype(BF16)


def kernel(x, attn_norm, a_w_qkv, a_q_norm, a_k_norm, a_w_o, b_w_qkv, b_q_norm, b_k_norm, b_w_o,
           ffn_norm, ffn_w_up, ffn_conv_w, ffn_conv_b, ffn_w_down):
    batch, seq, d_model = x.shape
    depth = attn_norm.shape[0]
    hd_all = N_HEADS * HEAD_DIM
    rope = _rope_tables(seq)
    bd = _block_diag_ones()
    x2 = x.reshape(batch * seq, d_model)
    tm = 512
    q_scale = ATTN_SCALE * LOG2_E

    for layer in range(depth):
        j = layer // 2
        gain = attn_norm[layer][None, :]
        dilated = layer % 2 == 0
        if dilated:
            n_groups = N_DIL
            qg = jnp.tile(a_q_norm[j][:, None, :], (1, N_HEADS, 1)).reshape(-1) * q_scale
            kg = jnp.tile(a_k_norm[j][:, None, :], (1, N_HEADS, 1)).reshape(-1)
            w_qkv, w_o = a_w_qkv[j], a_w_o[j]
        else:
            n_groups = 1
            qg = jnp.tile(b_q_norm[j], N_HEADS) * q_scale
            kg = jnp.tile(b_k_norm[j], N_HEADS)
            w_qkv, w_o = b_w_qkv[j], b_w_o[j]
        n_qk = 2 * n_groups * hd_all
        colgain = jnp.concatenate([qg, kg])[None, :]
        w_qkv = w_qkv.astype(BF16)
        qk = _norm_qk_proj(x2, gain, w_qkv, colgain, rope, bd, seq, BF16, tm, 512)
        v = _norm_v_proj(x2, gain, w_qkv, n_qk // 2, BF16, tm, 512)
        qk, v = qk.reshape(batch, seq, -1), v.reshape(batch, seq, -1)
        attn = _dilated_attention(qk, v, batch, seq) if dilated else _moba_attention(qk, v, batch, seq)
        x2 = _attn_out_conv_ffn(x2, attn.reshape(batch * seq, hd_all), w_o.astype(BF16),
                                ffn_norm[layer][None, :], ffn_w_up[layer].astype(BF16),
                                ffn_conv_w[layer], ffn_conv_b[layer][None, :],
                                ffn_w_down[layer].astype(BF16), seq, 1024, 256, 11)
    return x2.reshape(batch, seq, d_model)
```

```python
import functools
import math

import jax
import jax.numpy as jnp
from jax import lax
from jax.experimental import pallas as pl
from jax.experimental.pallas import tpu as pltpu

N_HEADS = 16
HEAD_DIM = 64
ROPE_DIM = HEAD_DIM // 4
ROPE_THETA = 500000.0
ATTN_SCALE = HEAD_DIM ** -0.5
DILATED_PAIRS = ((128, 1), (512, 4), (2048, 16))
N_DIL = len(DILATED_PAIRS)
BAND_BLOCK = 128
MOBA_BLOCK = 256
MOBA_TOPK = 3
CONV_WIDTH = 3
RMS_EPS = 1e-6

LANES = 128
BF16_SUBLANES = 16
MXU_WIDTH = 256
HEADS_PER_SLAB = LANES // HEAD_DIM
MASK_VALUE = -1e30
MOBA_CHAINS = 4
CONV_HALO = 8
VMEM_LIMIT = 56 * 1024 * 1024
LOG2_E = math.log2(math.e)

F32 = jnp.float32
BF16 = jnp.bfloat16
NT_DIMS = (((1,), (1,)), ((), ()))


def _params(semantics):
    return pltpu.CompilerParams(dimension_semantics=semantics, vmem_limit_bytes=VMEM_LIMIT)


def _rmsnorm_bf16(x_ref, g_ref):
    x = x_ref[...]
    ms = jnp.mean(x * x, axis=-1, keepdims=True)
    return (x * lax.rsqrt(ms + RMS_EPS) * g_ref[...]).astype(BF16)


def _qk_proj_kernel(x_ref, g_ref, w_ref, cg_ref, cos_ref, sa_ref, sb_ref, bd_ref, o_ref, *, tn):
    hn = _rmsnorm_bf16(x_ref, g_ref)
    bd = bd_ref[...]
    half = ROPE_DIM // 2
    for lo in range(0, w_ref.shape[1], tn):
        acc = jnp.dot(hn, w_ref[:, lo:lo + tn], preferred_element_type=F32)
        for c in range(tn // MXU_WIDTH):
            wide = acc[:, c * MXU_WIDTH:(c + 1) * MXU_WIDTH]
            ss_wide = jnp.dot((wide * wide).astype(BF16), bd, preferred_element_type=F32)
            for h in range(MXU_WIDTH // LANES):
                sl = slice(lo + c * MXU_WIDTH + h * LANES, lo + c * MXU_WIDTH + (h + 1) * LANES)
                a = wide[:, h * LANES:(h + 1) * LANES]
                ss = ss_wide[:, h * LANES:(h + 1) * LANES]
                y = a * lax.rsqrt(ss * (1.0 / HEAD_DIM) + RMS_EPS) * cg_ref[:, sl]
                y = (y * cos_ref[...]
                     + pltpu.roll(y, half, 1) * sa_ref[...]
                     + pltpu.roll(y, LANES - half, 1) * sb_ref[...])
                o_ref[:, sl] = y.astype(o_ref.dtype)


def _v_proj_kernel(x_ref, g_ref, w_ref, o_ref, *, tn):
    hn = _rmsnorm_bf16(x_ref, g_ref)
    for lo in range(0, w_ref.shape[1], tn):
        o_ref[:, lo:lo + tn] = jnp.dot(hn, w_ref[:, lo:lo + tn], preferred_element_type=F32).astype(o_ref.dtype)


def _norm_qk_proj(x2, gain, w, colgain, rope, bd, seq, out_dtype, tm, tn):
    m, d = x2.shape
    n = colgain.shape[1]
    cos_t, sa_t, sb_t = rope
    tiles_per_seq = seq // tm
    rope_spec = pl.BlockSpec((tm, LANES), lambda i: (i % tiles_per_seq, 0))
    return pl.pallas_call(
        functools.partial(_qk_proj_kernel, tn=tn),
        grid=(m // tm,),
        in_specs=[
            pl.BlockSpec((tm, d), lambda i: (i, 0)),
            pl.BlockSpec((1, d), lambda i: (0, 0)),
            pl.BlockSpec((d, n), lambda i: (0, 0), pipeline_mode=pl.Buffered(1)),
            pl.BlockSpec((1, n), lambda i: (0, 0)),
            rope_spec, rope_spec, rope_spec,
            pl.BlockSpec((MXU_WIDTH, MXU_WIDTH), lambda i: (0, 0)),
        ],
        out_specs=pl.BlockSpec((tm, n), lambda i: (i, 0)),
        out_shape=jax.ShapeDtypeStruct((m, n), out_dtype),
        compiler_params=_params(("parallel",)),
        name="norm_qk_proj",
    )(x2, gain, w, colgain, cos_t, sa_t, sb_t, bd)


def _norm_v_proj(x2, gain, w, n, out_dtype, tm, tn):
    m, d = x2.shape
    last = w.shape[1] // n - 1
    return pl.pallas_call(
        functools.partial(_v_proj_kernel, tn=tn),
        grid=(m // tm,),
        in_specs=[
            pl.BlockSpec((tm, d), lambda i: (i, 0)),
            pl.BlockSpec((1, d), lambda i: (0, 0)),
            pl.BlockSpec((d, n), lambda i: (0, last), pipeline_mode=pl.Buffered(1)),
        ],
        out_specs=pl.BlockSpec((tm, n), lambda i: (i, 0)),
        out_shape=jax.ShapeDtypeStruct((m, n), out_dtype),
        compiler_params=_params(("parallel",)),
        name="norm_v_proj",
    )(x2, gain, w)


def _dilated_kernel(q0, q1, q2, k0, k1, k2, v0, v1, v2, o_ref, num_ref, m_ref, den_ref, mask_ref, *, tq):
    t = pl.program_id(2)
    q_refs, k_refs, v_refs = (q0, q1, q2), (k0, k1, k2), (v0, v1, v2)
    blk = BAND_BLOCK

    lane = lax.broadcasted_iota(jnp.int32, (blk, LANES), 1)
    first_head = lane < HEAD_DIM
    ones_rhs = jnp.ones((2 * blk, LANES), BF16)

    reaches = sorted({w // d for w, d in DILATED_PAIRS})
    qi = lax.broadcasted_iota(jnp.int32, (2 * blk, 2 * blk), 0) % blk
    kj = lax.broadcasted_iota(jnp.int32, (2 * blk, 2 * blk), 1)
    dist = qi + blk - kj
    for n, reach in enumerate(reaches):
        band = (dist >= 0) & (dist <= reach)
        mask_ref[n, 0] = jnp.where(band, 0.0, MASK_VALUE)
        mask_ref[n, 1] = jnp.where(band & (kj >= blk), 0.0, MASK_VALUE)

    for g, (window, dil) in enumerate(DILATED_PAIRS):
        reach = window // dil
        span = blk * dil
        assert dil & (dil - 1) == 0 and reach <= blk
        qg, kg, vg = q_refs[g], k_refs[g], v_refs[g]
        mask_g = reaches.index(reach)

        def load_kv(base):
            if dil == 1:
                base = pl.multiple_of(base, blk)
            rows = pl.ds(base, blk, stride=dil)
            return kg[rows, :].astype(BF16), vg[rows, :].astype(BF16)

        for r in range(dil):
            prev = None
            for u in range(tq // span):
                qbase = u * span + r
                kbase = t * tq + qbase
                kc, vc = load_kv(kbase)
                if prev is None:
                    pbase = kbase - span
                    no_prev = (pbase < 0).astype(jnp.int32)
                    kp, vp = load_kv(jnp.maximum(pbase, 0))
                else:
                    no_prev = 0
                    kp, vp = prev
                prev = (kc, vc)
                q = qg[pl.ds(qbase, blk, stride=dil), :]
                zero = jnp.zeros_like(q)
                q2 = jnp.concatenate([jnp.where(first_head, q, zero),
                                      jnp.where(first_head, zero, q)], axis=0).astype(BF16)
                kcat = jnp.concatenate([kp, kc], axis=0)
                vcat = jnp.concatenate([jnp.concatenate([vp, vc], axis=0), ones_rhs], axis=1)
                s = lax.dot_general(q2, kcat, NT_DIMS, preferred_element_type=F32)
                s = s + mask_ref[mask_g, no_prev]
                m = jnp.max(s, axis=-1, keepdims=True)
                p = jnp.exp2(s - m).astype(BF16)
                ov = jnp.dot(p, vcat, preferred_element_type=F32)
                mb = jnp.broadcast_to(m, (2 * blk, LANES))
                rows = pl.ds(qbase, blk, stride=dil)
                num_ref[g, rows, :] = jnp.where(first_head, ov[:blk, :LANES], ov[blk:, :LANES])
                den_ref[g, rows, :] = jnp.where(first_head, ov[:blk, LANES:], ov[blk:, LANES:])
                m_ref[g, rows, :] = jnp.where(first_head, mb[:blk], mb[blk:])

    rows = 2 * blk

    def merge(c, carry):
        sl = pl.ds(pl.multiple_of(c * rows, rows), rows)
        ms = [m_ref[g, sl, :] for g in range(N_DIL)]
        mx = functools.reduce(jnp.maximum, ms)
        ws = [jnp.exp2(mg - mx) for mg in ms]
        num = sum(w * num_ref[g, sl, :] for g, w in enumerate(ws))
        den = sum(w * den_ref[g, sl, :] for g, w in enumerate(ws))
        o_ref[sl, :] = (num / den).astype(o_ref.dtype)
        return carry

    lax.fori_loop(0, tq // rows, merge, 0)


def _dilated_attention(qk, v, batch, seq):
    span_max = BAND_BLOCK * max(d for _, d in DILATED_PAIRS)
    tq = span_max
    assert seq % tq == 0
    slabs = N_HEADS // HEADS_PER_SLAB
    n_reach = len({w // d for w, d in DILATED_PAIRS})

    q_specs = [pl.BlockSpec((None, tq, LANES), lambda b, s, t, g=g: (b, t, g * slabs + s)) for g in range(N_DIL)]
    k_specs = [pl.BlockSpec((None, seq, LANES), lambda b, s, t, g=g: (b, 0, (N_DIL + g) * slabs + s))
               for g in range(N_DIL)]
    v_specs = [pl.BlockSpec((None, seq, LANES), lambda b, s, t, g=g: (b, 0, g * slabs + s)) for g in range(N_DIL)]
    stat = pltpu.VMEM((N_DIL, tq, LANES), F32)
    return pl.pallas_call(
        functools.partial(_dilated_kernel, tq=tq),
        grid=(batch, slabs, seq // tq),
        in_specs=q_specs + k_specs + v_specs,
        out_specs=pl.BlockSpec((None, tq, LANES), lambda b, s, t: (b, t, s)),
        out_shape=jax.ShapeDtypeStruct((batch, seq, N_HEADS * HEAD_DIM), BF16),
        scratch_shapes=[stat, stat, stat, pltpu.VMEM((n_reach, 2, 2 * BAND_BLOCK, 2 * BAND_BLOCK), F32)],
        compiler_params=_params(("parallel", "parallel", "arbitrary")),
        name="dilated_attention",
    )(*([qk] * 6 + [v] * 3))


def _moba_kernel(q_ref, k_ref, v_ref, oh_ref, mask_ref, o_ref, km_ref, kmhl_ref, vaug_ref, s_ref, m_ref, acc_ref,
                 *, nblk, chains):
    own_pair = pl.program_id(2)
    blk = MOBA_BLOCK
    both = range(2)
    nrow = km_ref.shape[1]
    sup = 2 * blk
    lane = lax.broadcasted_iota(jnp.int32, (blk, LANES), 1)
    first_head = lane < HEAD_DIM
    cs = range(chains)

    def slab(c):
        return slice(c * LANES, (c + 1) * LANES)

    @pl.when(own_pair == 0)
    def _():
        for c in cs:
            km_ref[c] = jnp.zeros(km_ref.shape[1:], F32)
            for j in range(nblk):
                rows = slice(j * blk, (j + 1) * blk)
                km_ref[c, j:j + 1, :] = jnp.sum(k_ref[rows, slab(c)].astype(F32), axis=0,
                                                keepdims=True) * (1.0 / blk)
                vj = v_ref[rows, slab(c)].astype(F32)
                vaug_ref[c, 0, rows, :] = jnp.where(first_head, vj, 1.0).astype(BF16)
                vaug_ref[c, 1, rows, :] = jnp.where(first_head, 1.0, vj).astype(BF16)
            km = km_ref[c]
            hi = km.astype(BF16)
            kmhl_ref[c] = jnp.concatenate([hi, (km - hi.astype(F32)).astype(BF16)], axis=0)

    row = lax.broadcasted_iota(jnp.int32, (nrow, 2 * blk), 0)
    rowf = row.astype(F32)

    def augmented_queries(j, c):
        i = 2 * own_pair + j
        q = q_ref[j * blk:(j + 1) * blk, slab(c)]
        zero = jnp.zeros_like(q)
        q2 = jnp.concatenate([jnp.where(first_head, q, zero), jnp.where(first_head, zero, q)], axis=0)
        gate2 = lax.dot_general(kmhl_ref[c], q2, NT_DIMS, preferred_element_type=F32)
        gate = gate2[:nrow] + gate2[nrow:]
        remaining = row < i
        sel = row >= i
        for _ in range(MOBA_TOPK):
            gm = jnp.max(jnp.where(remaining, gate, -jnp.inf), axis=0, keepdims=True)
            cand = remaining & (gate == gm)
            first = jnp.min(jnp.where(cand, rowf, float(LANES)), axis=0, keepdims=True)
            pick = rowf == first
            sel = sel | pick
            remaining = remaining & jnp.logical_not(pick)
        bias_t = jnp.concatenate([jnp.where(sel, 0.0, MASK_VALUE),
                                  jnp.zeros((LANES - nrow, 2 * blk), F32)], axis=0)
        return jnp.concatenate([q2, bias_t.T.astype(BF16)], axis=1)

    q_aug = [jnp.concatenate([augmented_queries(j, c) for j in both], axis=0) for c in cs]

    own_base = own_pair * sup

    def scores(c, base):
        rows = pl.ds(pl.multiple_of(base, sup), sup)
        k_aug = jnp.concatenate([k_ref[rows, slab(c)], oh_ref[rows, :]], axis=1)
        return lax.dot_general(q_aug[c], k_aug, NT_DIMS, preferred_element_type=F32)

    def store_scores(c, buf, s, masked=False):
        for j in both:
            sj = s[j * sup:(j + 1) * sup]
            s_ref[j, c, buf] = sj + mask_ref[j] if masked else sj

    def accumulate(j, c, buf, tile):
        base = jnp.where(tile == 0, own_base, (tile - 1) * sup)
        s = s_ref[j, c, buf]
        m_old = m_ref[j, c]
        m_new = jnp.maximum(m_old, jnp.max(s, axis=-1, keepdims=True))
        alpha = jnp.exp2(m_old - m_new)
        p = jnp.exp2(s - jnp.concatenate([m_new] * (sup // LANES), axis=1)).astype(BF16)
        rows = pl.ds(pl.multiple_of(base, sup), sup)
        pv = jnp.concatenate([jnp.dot(p[:blk], vaug_ref[c, 0, rows, :], preferred_element_type=F32),
                              jnp.dot(p[blk:], vaug_ref[c, 1, rows, :], preferred_element_type=F32)], axis=0)
        acc_ref[j, c] = acc_ref[j, c] * alpha + pv
        m_ref[j, c] = m_new

    def accumulate_all(buf, tile):
        for c in cs:
            for j in both:
                accumulate(j, c, buf, tile)

    for c in cs:
        store_scores(c, 0, scores(c, own_base), masked=True)
        for j in both:
            m_ref[j, c] = jnp.full(m_ref.shape[2:], MASK_VALUE, F32)
            acc_ref[j, c] = jnp.zeros(acc_ref.shape[2:], F32)

    def advance(dst, src, tile):
        for c in cs:
            store_scores(c, dst, scores(c, (tile - 1) * sup))
        accumulate_all(src, tile - 1)

    def body(n, carry):
        advance(1, 0, 2 * n + 1)
        advance(0, 1, 2 * n + 2)
        return carry

    lax.fori_loop(0, lax.shift_right_logical(own_pair, 1), body, 0)

    @pl.when(lax.bitwise_and(own_pair, 1) == 1)
    def _():
        advance(1, 0, own_pair)
        accumulate_all(1, own_pair)

    @pl.when(lax.bitwise_and(own_pair, 1) == 0)
    def _():
        accumulate_all(0, own_pair)

    for c in cs:
        for j in both:
            acc = acc_ref[j, c]
            o = acc / pltpu.roll(acc, HEAD_DIM, 1)
            o_ref[j * blk:(j + 1) * blk, slab(c)] = jnp.where(first_head, o[:blk], o[blk:]).astype(o_ref.dtype)


def _moba_attention(qk, v, batch, seq):
    assert seq % (2 * MOBA_BLOCK) == 0
    nblk = seq // MOBA_BLOCK
    assert nblk <= LANES
    nrow = -(-nblk // BF16_SUBLANES) * BF16_SUBLANES
    slabs = N_HEADS * HEAD_DIM // LANES
    blk = MOBA_BLOCK
    block_id = jnp.arange(seq, dtype=jnp.int32) // blk
    onehot = (block_id[:, None] == jnp.arange(LANES, dtype=jnp.int32)[None, :]).astype(BF16)
    qi = jnp.arange(2 * blk, dtype=jnp.int32)[:, None] % blk
    kj = jnp.arange(2 * blk, dtype=jnp.int32)[None, :]
    own_mask = jnp.stack([jnp.where(kj - qi <= parity * blk, 0.0, MASK_VALUE) for parity in (0, 1)]).astype(F32)
    chains = MOBA_CHAINS
    assert slabs % chains == 0
    width = chains * LANES
    once = dict(pipeline_mode=pl.Buffered(1))
    return pl.pallas_call(
        functools.partial(_moba_kernel, nblk=nblk, chains=chains),
        grid=(batch, slabs // chains, nblk // 2),
        in_specs=[
            pl.BlockSpec((None, 2 * blk, width), lambda b, s, i: (b, i, s)),
            pl.BlockSpec((None, seq, width), lambda b, s, i: (b, 0, slabs // chains + s), **once),
            pl.BlockSpec((None, seq, width), lambda b, s, i: (b, 0, s), **once),
            pl.BlockSpec((seq, LANES), lambda b, s, i: (0, 0), **once),
            pl.BlockSpec((2, 2 * blk, 2 * blk), lambda b, s, i: (0, 0, 0), **once),
        ],
        out_specs=pl.BlockSpec((None, 2 * blk, width), lambda b, s, i: (b, i, s)),
        out_shape=jax.ShapeDtypeStruct((batch, seq, N_HEADS * HEAD_DIM), BF16),
        scratch_shapes=[
            pltpu.VMEM((chains, nrow, LANES), F32), pltpu.VMEM((chains, 2 * nrow, LANES), BF16),
            pltpu.VMEM((chains, HEADS_PER_SLAB, seq, LANES), BF16),
            pltpu.VMEM((2, chains, 2, 2 * blk, 2 * blk), F32),
            pltpu.VMEM((2, chains, 2 * blk, LANES), F32), pltpu.VMEM((2, chains, 2 * blk, LANES), F32),
        ],
        compiler_params=_params(("parallel", "parallel", "arbitrary")),
        name="moba_attention",
    )(qk, qk, v, onehot, own_mask)


def _ffn_kernel(x_ref, xh_ref, a_ref, ah_ref, wo_ref, g_ref, wu_ref, cw_ref, cb_ref, wd_ref, o_ref,
                *, tiles_per_seq, tf, group):
    i = pl.program_id(0)
    halo = CONV_HALO
    d_ff = wd_ref.shape[0]

    def rms(x):
        ms = jnp.mean(x * x, axis=-1, keepdims=True)
        return x * lax.rsqrt(ms + RMS_EPS) * g_ref[...]

    wo = wo_ref[...]
    x1 = x_ref[...] + jnp.dot(a_ref[...], wo, preferred_element_type=F32)
    xh1 = xh_ref[...] + jnp.dot(ah_ref[...], wo, preferred_element_type=F32)[ah_ref.shape[0] - halo:]

    keep = (i % tiles_per_seq != 0).astype(F32)
    hn = jnp.concatenate([(rms(xh1) * keep).astype(BF16), rms(x1).astype(BF16)], axis=0)

    def conv(col):
        u = jnp.dot(hn, wu_ref[:, col:col + tf], preferred_element_type=F32)
        cw = cw_ref[:, col:col + tf]
        return (cb_ref[:, col:col + tf]
                + cw[0:1, :] * pltpu.roll(u, 2, 0)[halo:]
                + cw[1:2, :] * pltpu.roll(u, 1, 0)[halo:]
                + cw[2:3, :] * u[halo:])

    acc = x1
    for lo in range(0, d_ff, group * tf):
        hi = min(lo + group * tf, d_ff)
        acts = []
        for col in range(lo, hi, tf):
            gate = conv(col)
            val = conv(d_ff + col)
            acts.append((gate * jax.nn.sigmoid(gate) * val).astype(BF16))
        acc = acc + jnp.dot(jnp.concatenate(acts, axis=1), wd_ref[lo:hi, :], preferred_element_type=F32)
    o_ref[...] = acc


def _attn_out_conv_ffn(x2, attn2, w_o, gain, w_up, conv_w, conv_b, w_down, seq, tm, tf, group):
    m, d = x2.shape
    da = attn2.shape[1]
    d_ff = w_down.shape[0]
    assert d_ff % tf == 0
    halo_blocks = tm // CONV_HALO
    attn_halo = BF16_SUBLANES
    attn_halo_blocks = tm // attn_halo
    resident = dict(pipeline_mode=pl.Buffered(1))
    return pl.pallas_call(
        functools.partial(_ffn_kernel, tiles_per_seq=seq // tm, tf=tf, group=group),
        grid=(m // tm,),
        in_specs=[
            pl.BlockSpec((tm, d), lambda i: (i, 0)),
            pl.BlockSpec((CONV_HALO, d), lambda i: (jnp.maximum(i * halo_blocks - 1, 0), 0)),
            pl.BlockSpec((tm, da), lambda i: (i, 0)),
            pl.BlockSpec((attn_halo, da), lambda i: (jnp.maximum(i * attn_halo_blocks - 1, 0), 0)),
            pl.BlockSpec((da, d), lambda i: (0, 0), **resident),
            pl.BlockSpec((1, d), lambda i: (0, 0)),
            pl.BlockSpec((d, 2 * d_ff), lambda i: (0, 0), **resident),
            pl.BlockSpec((CONV_WIDTH, 2 * d_ff), lambda i: (0, 0)),
            pl.BlockSpec((1, 2 * d_ff), lambda i: (0, 0)),
            pl.BlockSpec((d_ff, d), lambda i: (0, 0), **resident),
        ],
        out_specs=pl.BlockSpec((tm, d), lambda i: (i, 0)),
        out_shape=jax.ShapeDtypeStruct((m, d), F32),
        compiler_params=_params(("parallel",)),
        name="attn_out_conv_ffn",
    )(x2, x2, attn2, attn2, w_o, gain, w_up, conv_w, conv_b, w_down)


def _rope_tables(seq):
    pos = jnp.arange(seq, dtype=F32)
    inv_freq = ROPE_THETA ** (-jnp.arange(0, ROPE_DIM, 2, dtype=F32) / ROPE_DIM)
    ang = pos[:, None] * inv_freq[None, :]
    cos, sin = jnp.cos(ang), jnp.sin(ang)
    half = ROPE_DIM // 2
    pad = jnp.zeros((seq, HEAD_DIM - ROPE_DIM), F32)
    zeros = jnp.zeros((seq, half), F32)
    cos_h = jnp.concatenate([cos, cos, pad + 1.0], axis=1)
    sa_h = jnp.concatenate([zeros, sin, pad], axis=1)
    sb_h = jnp.concatenate([-sin, zeros, pad], axis=1)
    rep = LANES // HEAD_DIM
    return tuple(jnp.tile(t, (1, rep)) for t in (cos_h, sa_h, sb_h))


def _block_diag_ones():
    r = jnp.arange(MXU_WIDTH) // HEAD_DIM
    return (r[:, None] == r[None, :]).astype(BF16)


def kernel(x, attn_norm, a_w_qkv, a_q_norm, a_k_norm, a_w_o, b_w_qkv, b_q_norm, b_k_norm, b_w_o,
           ffn_norm, ffn_w_up, ffn_conv_w, ffn_conv_b, ffn_w_down):
    batch, seq, d_model = x.shape
    depth = attn_norm.shape[0]
    hd_all = N_HEADS * HEAD_DIM
    rope = _rope_tables(seq)
    bd = _block_diag_ones()
    x2 = x.reshape(batch * seq, d_model)
    tm = 512
    q_scale = ATTN_SCALE * LOG2_E

    for layer in range(depth):
        j = layer // 2
        gain = attn_norm[layer][None, :]
        dilated = layer % 2 == 0
        if dilated:
            n_groups = N_DIL
            qg = jnp.tile(a_q_norm[j][:, None, :], (1, N_HEADS, 1)).reshape(-1) * q_scale
            kg = jnp.tile(a_k_norm[j][:, None, :], (1, N_HEADS, 1)).reshape(-1)
            w_qkv, w_o = a_w_qkv[j], a_w_o[j]
        else:
            n_groups = 1
            qg = jnp.tile(b_q_norm[j], N_HEADS) * q_scale
            kg = jnp.tile(b_k_norm[j], N_HEADS)
            w_qkv, w_o = b_w_qkv[j], b_w_o[j]
        n_qk = 2 * n_groups * hd_all
        colgain = jnp.concatenate([qg, kg])[None, :]
        w_qkv = w_qkv.astype(BF16)
        act_dtype = F32 if dilated else BF16
        qk = _norm_qk_proj(x2, gain, w_qkv, colgain, rope, bd, seq, act_dtype, tm, 512)
        v = _norm_v_proj(x2, gain, w_qkv, n_qk // 2, act_dtype, tm, 512)
        qk, v = qk.reshape(batch, seq, -1), v.reshape(batch, seq, -1)
        attn = _dilated_attention(qk, v, batch, seq) if dilated else _moba_attention(qk, v, batch, seq)
        x2 = _attn_out_conv_ffn(x2, attn.reshape(batch * seq, hd_all), w_o.astype(BF16),
                                ffn_norm[layer][None, :], ffn_w_up[layer].astype(BF16),
                                ffn_conv_w[layer], ffn_conv_b[layer][None, :],
                                ffn_w_down[layer].astype(BF16), seq, 1024, 256, 11)
    return x2.reshape(batch, seq, d_model)
```

```python
import functools
import math

import jax
import jax.numpy as jnp
from jax import lax
from jax.experimental import pallas as pl
from jax.experimental.pallas import tpu as pltpu

N_HEADS = 16
HEAD_DIM = 64
ROPE_DIM = HEAD_DIM // 4
ROPE_THETA = 500000.0
ATTN_SCALE = HEAD_DIM ** -0.5
DILATED_PAIRS = ((128, 1), (512, 4), (2048, 16))
N_DIL = len(DILATED_PAIRS)
BAND_BLOCK = 128
MOBA_BLOCK = 256
MOBA_TOPK = 3
CONV_WIDTH = 3
RMS_EPS = 1e-6

LANES = 128
BF16_SUBLANES = 16
MXU_WIDTH = 256
HEADS_PER_SLAB = LANES // HEAD_DIM
MASK_VALUE = -1e30
MOBA_CHAINS = 4
CONV_HALO = 8
VMEM_LIMIT = 56 * 1024 * 1024
LOG2_E = math.log2(math.e)

F32 = jnp.float32
BF16 = jnp.bfloat16
NT_DIMS = (((1,), (1,)), ((), ()))


def _params(semantics):
    return pltpu.CompilerParams(dimension_semantics=semantics, vmem_limit_bytes=VMEM_LIMIT)


def _rmsnorm_bf16(x_ref, g_ref):
    x = x_ref[...]
    ms = jnp.mean(x * x, axis=-1, keepdims=True)
    return (x * lax.rsqrt(ms + RMS_EPS) * g_ref[...]).astype(BF16)


def _qk_proj_kernel(x_ref, g_ref, w_ref, cg_ref, cos_ref, sa_ref, sb_ref, bd_ref, o_ref, *, tn):
    hn = _rmsnorm_bf16(x_ref, g_ref)
    bd = bd_ref[...]
    half = ROPE_DIM // 2
    for lo in range(0, w_ref.shape[1], tn):
        acc = jnp.dot(hn, w_ref[:, lo:lo + tn], preferred_element_type=F32)
        for c in range(tn // MXU_WIDTH):
            wide = acc[:, c * MXU_WIDTH:(c + 1) * MXU_WIDTH]
            ss_wide = jnp.dot((wide * wide).astype(BF16), bd, preferred_element_type=F32)
            for h in range(MXU_WIDTH // LANES):
                sl = slice(lo + c * MXU_WIDTH + h * LANES, lo + c * MXU_WIDTH + (h + 1) * LANES)
                a = wide[:, h * LANES:(h + 1) * LANES]
                ss = ss_wide[:, h * LANES:(h + 1) * LANES]
                y = a * lax.rsqrt(ss * (1.0 / HEAD_DIM) + RMS_EPS) * cg_ref[:, sl]
                y = (y * cos_ref[...]
                     + pltpu.roll(y, half, 1) * sa_ref[...]
                     + pltpu.roll(y, LANES - half, 1) * sb_ref[...])
                o_ref[:, sl] = y.astype(o_ref.dtype)


def _v_proj_kernel(x_ref, g_ref, w_ref, o_ref, *, tn):
    hn = _rmsnorm_bf16(x_ref, g_ref)
    for lo in range(0, w_ref.shape[1], tn):
        o_ref[:, lo:lo + tn] = jnp.dot(hn, w_ref[:, lo:lo + tn], preferred_element_type=F32).astype(o_ref.dtype)


def _norm_qk_proj(x2, gain, w, colgain, rope, bd, seq, out_dtype, tm, tn):
    m, d = x2.shape
    n = colgain.shape[1]
    cos_t, sa_t, sb_t = rope
    tiles_per_seq = seq // tm
    rope_spec = pl.BlockSpec((tm, LANES), lambda i: (i % tiles_per_seq, 0))
    return pl.pallas_call(
        functools.partial(_qk_proj_kernel, tn=tn),
        grid=(m // tm,),
        in_specs=[
            pl.BlockSpec((tm, d), lambda i: (i, 0)),
            pl.BlockSpec((1, d), lambda i: (0, 0)),
            pl.BlockSpec((d, n), lambda i: (0, 0), pipeline_mode=pl.Buffered(1)),
            pl.BlockSpec((1, n), lambda i: (0, 0)),
            rope_spec, rope_spec, rope_spec,
            pl.BlockSpec((MXU_WIDTH, MXU_WIDTH), lambda i: (0, 0)),
        ],
        out_specs=pl.BlockSpec((tm, n), lambda i: (i, 0)),
        out_shape=jax.ShapeDtypeStruct((m, n), out_dtype),
        compiler_params=_params(("parallel",)),
        name="norm_qk_proj",
    )(x2, gain, w, colgain, cos_t, sa_t, sb_t, bd)


def _norm_v_proj(x2, gain, w, n, out_dtype, tm, tn):
    m, d = x2.shape
    last = w.shape[1] // n - 1
    return pl.pallas_call(
        functools.partial(_v_proj_kernel, tn=tn),
        grid=(m // tm,),
        in_specs=[
            pl.BlockSpec((tm, d), lambda i: (i, 0)),
            pl.BlockSpec((1, d), lambda i: (0, 0)),
            pl.BlockSpec((d, n), lambda i: (0, last), pipeline_mode=pl.Buffered(1)),
        ],
        out_specs=pl.BlockSpec((tm, n), lambda i: (i, 0)),
        out_shape=jax.ShapeDtypeStruct((m, n), out_dtype),
        compiler_params=_params(("parallel",)),
        name="norm_v_proj",
    )(x2, gain, w)


def _dilated_kernel(q0, q1, q2, k0, k1, k2, v0, v1, v2, o_ref, num_ref, m_ref, den_ref, mask_ref, *, tq):
    t = pl.program_id(2)
    q_refs, k_refs, v_refs = (q0, q1, q2), (k0, k1, k2), (v0, v1, v2)
    blk = BAND_BLOCK

    lane = lax.broadcasted_iota(jnp.int32, (blk, LANES), 1)
    first_head = lane < HEAD_DIM
    ones_rhs = jnp.ones((2 * blk, LANES), BF16)

    reaches = sorted({w // d for w, d in DILATED_PAIRS})
    qi = lax.broadcasted_iota(jnp.int32, (2 * blk, 2 * blk), 0) % blk
    kj = lax.broadcasted_iota(jnp.int32, (2 * blk, 2 * blk), 1)
    dist = qi + blk - kj
    for n, reach in enumerate(reaches):
        band = (dist >= 0) & (dist <= reach)
        mask_ref[n, 0] = jnp.where(band, 0.0, MASK_VALUE)
        mask_ref[n, 1] = jnp.where(band & (kj >= blk), 0.0, MASK_VALUE)

    def band_block(g, r, u, prev):
        window, dil = DILATED_PAIRS[g]
        reach = window // dil
        span = blk * dil
        assert dil & (dil - 1) == 0 and reach <= blk
        qg, kg, vg = q_refs[g], k_refs[g], v_refs[g]

        def load_kv(base):
            if dil == 1:
                base = pl.multiple_of(base, blk)
            rows = pl.ds(base, blk, stride=dil)
            return kg[rows, :].astype(BF16), vg[rows, :].astype(BF16)

        qbase = u * span + r
        kbase = t * tq + qbase
        kc, vc = load_kv(kbase)
        if prev is None:
            pbase = kbase - span
            no_prev = (pbase < 0).astype(jnp.int32)
            kp, vp = load_kv(jnp.maximum(pbase, 0))
        else:
            no_prev = 0
            kp, vp = prev
        q = qg[pl.ds(qbase, blk, stride=dil), :]
        zero = jnp.zeros_like(q)
        q2 = jnp.concatenate([jnp.where(first_head, q, zero),
                              jnp.where(first_head, zero, q)], axis=0).astype(BF16)
        kcat = jnp.concatenate([kp, kc], axis=0)
        vcat = jnp.concatenate([jnp.concatenate([vp, vc], axis=0), ones_rhs], axis=1)
        s = lax.dot_general(q2, kcat, NT_DIMS, preferred_element_type=F32)
        s = s + mask_ref[reaches.index(reach), no_prev]
        m = jnp.max(s, axis=-1, keepdims=True)
        p = jnp.exp2(s - m).astype(BF16)
        ov = jnp.dot(p, vcat, preferred_element_type=F32)
        mb = jnp.broadcast_to(m, (2 * blk, LANES))
        rows = pl.ds(qbase, blk, stride=dil)
        num_ref[g, rows, :] = jnp.where(first_head, ov[:blk, :LANES], ov[blk:, :LANES])
        den_ref[g, rows, :] = jnp.where(first_head, ov[:blk, LANES:], ov[blk:, LANES:])
        m_ref[g, rows, :] = jnp.where(first_head, mb[:blk], mb[blk:])
        return kc, vc

    orders = [[(r, u) for r in range(d) for u in range(tq // (blk * d))] for _, d in DILATED_PAIRS]
    prev = [dict() for _ in DILATED_PAIRS]
    for n in range(max(len(o) for o in orders)):
        for g, order in enumerate(orders):
            if n < len(order):
                r, u = order[n]
                prev[g][r] = band_block(g, r, u, prev[g].get(r) if u > 0 else None)

    rows = 2 * blk

    def merge(c, carry):
        sl = pl.ds(pl.multiple_of(c * rows, rows), rows)
        ms = [m_ref[g, sl, :] for g in range(N_DIL)]
        mx = functools.reduce(jnp.maximum, ms)
        ws = [jnp.exp2(mg - mx) for mg in ms]
        num = sum(w * num_ref[g, sl, :] for g, w in enumerate(ws))
        den = sum(w * den_ref[g, sl, :] for g, w in enumerate(ws))
        o_ref[sl, :] = (num / den).astype(o_ref.dtype)
        return carry

    lax.fori_loop(0, tq // rows, merge, 0)


def _dilated_attention(qk, v, batch, seq):
    span_max = BAND_BLOCK * max(d for _, d in DILATED_PAIRS)
    tq = span_max
    assert seq % tq == 0
    slabs = N_HEADS // HEADS_PER_SLAB
    n_reach = len({w // d for w, d in DILATED_PAIRS})

    q_specs = [pl.BlockSpec((None, tq, LANES), lambda b, s, t, g=g: (b, t, g * slabs + s)) for g in range(N_DIL)]
    k_specs = [pl.BlockSpec((None, seq, LANES), lambda b, s, t, g=g: (b, 0, (N_DIL + g) * slabs + s))
               for g in range(N_DIL)]
    v_specs = [pl.BlockSpec((None, seq, LANES), lambda b, s, t, g=g: (b, 0, g * slabs + s)) for g in range(N_DIL)]
    stat = pltpu.VMEM((N_DIL, tq, LANES), F32)
    return pl.pallas_call(
        functools.partial(_dilated_kernel, tq=tq),
        grid=(batch, slabs, seq // tq),
        in_specs=q_specs + k_specs + v_specs,
        out_specs=pl.BlockSpec((None, tq, LANES), lambda b, s, t: (b, t, s)),
        out_shape=jax.ShapeDtypeStruct((batch, seq, N_HEADS * HEAD_DIM), BF16),
        scratch_shapes=[stat, stat, stat, pltpu.VMEM((n_reach, 2, 2 * BAND_BLOCK, 2 * BAND_BLOCK), F32)],
        compiler_params=_params(("parallel", "parallel", "arbitrary")),
        name="dilated_attention",
    )(*([qk] * 6 + [v] * 3))


def _moba_kernel(q_ref, k_ref, v_ref, oh_ref, mask_ref, o_ref, km_ref, kmhl_ref, vaug_ref, s_ref, m_ref, acc_ref,
                 *, nblk, chains):
    own_pair = pl.program_id(2)
    blk = MOBA_BLOCK
    both = range(2)
    nrow = km_ref.shape[1]
    sup = 2 * blk
    lane = lax.broadcasted_iota(jnp.int32, (blk, LANES), 1)
    first_head = lane < HEAD_DIM
    cs = range(chains)

    def slab(c):
        return slice(c * LANES, (c + 1) * LANES)

    @pl.when(own_pair == 0)
    def _():
        for c in cs:
            km_ref[c] = jnp.zeros(km_ref.shape[1:], F32)
            for j in range(nblk):
                rows = slice(j * blk, (j + 1) * blk)
                km_ref[c, j:j + 1, :] = jnp.sum(k_ref[rows, slab(c)].astype(F32), axis=0,
                                                keepdims=True) * (1.0 / blk)
                vj = v_ref[rows, slab(c)].astype(F32)
                vaug_ref[c, 0, rows, :] = jnp.where(first_head, vj, 1.0).astype(BF16)
                vaug_ref[c, 1, rows, :] = jnp.where(first_head, 1.0, vj).astype(BF16)
            km = km_ref[c]
            hi = km.astype(BF16)
            kmhl_ref[c] = jnp.concatenate([hi, (km - hi.astype(F32)).astype(BF16)], axis=0)

    row = lax.broadcasted_iota(jnp.int32, (nrow, 2 * blk), 0)
    rowf = row.astype(F32)

    def augmented_queries(j, c):
        i = 2 * own_pair + j
        q = q_ref[j * blk:(j + 1) * blk, slab(c)]
        zero = jnp.zeros_like(q)
        q2 = jnp.concatenate([jnp.where(first_head, q, zero), jnp.where(first_head, zero, q)], axis=0)
        gate2 = lax.dot_general(kmhl_ref[c], q2, NT_DIMS, preferred_element_type=F32)
        gate = gate2[:nrow] + gate2[nrow:]
        remaining = row < i
        sel = row >= i
        for _ in range(MOBA_TOPK):
            gm = jnp.max(jnp.where(remaining, gate, -jnp.inf), axis=0, keepdims=True)
            cand = remaining & (gate == gm)
            first = jnp.min(jnp.where(cand, rowf, float(LANES)), axis=0, keepdims=True)
            pick = rowf == first
            sel = sel | pick
            remaining = remaining & jnp.logical_not(pick)
        bias_t = jnp.concatenate([jnp.where(sel, 0.0, MASK_VALUE),
                                  jnp.zeros((LANES - nrow, 2 * blk), F32)], axis=0)
        return jnp.concatenate([q2, bias_t.T.astype(BF16)], axis=1)

    q_aug = [jnp.concatenate([augmented_queries(j, c) for j in both], axis=0) for c in cs]

    own_base = own_pair * sup

    def scores(c, base):
        rows = pl.ds(pl.multiple_of(base, sup), sup)
        k_aug = jnp.concatenate([k_ref[rows, slab(c)], oh_ref[rows, :]], axis=1)
        return lax.dot_general(q_aug[c], k_aug, NT_DIMS, preferred_element_type=F32)

    def store_scores(c, buf, s, masked=False):
        for j in both:
            sj = s[j * sup:(j + 1) * sup]
            s_ref[j, c, buf] = sj + mask_ref[j] if masked else sj

    def accumulate(j, c, buf, tile):
        base = jnp.where(tile == 0, own_base, (tile - 1) * sup)
        s = s_ref[j, c, buf]
        m_old = m_ref[j, c]
        m_new = jnp.maximum(m_old, jnp.max(s, axis=-1, keepdims=True))
        alpha = jnp.exp2(m_old - m_new)
        p = jnp.exp2(s - jnp.concatenate([m_new] * (sup // LANES), axis=1)).astype(BF16)
        rows = pl.ds(pl.multiple_of(base, sup), sup)
        pv = jnp.concatenate([jnp.dot(p[:blk], vaug_ref[c, 0, rows, :], preferred_element_type=F32),
                              jnp.dot(p[blk:], vaug_ref[c, 1, rows, :], preferred_element_type=F32)], axis=0)
        acc_ref[j, c] = acc_ref[j, c] * alpha + pv
        m_ref[j, c] = m_new

    def accumulate_all(buf, tile):
        for c in cs:
            for j in both:
                accumulate(j, c, buf, tile)

    for c in cs:
        store_scores(c, 0, scores(c, own_base), masked=True)
        for j in both:
            m_ref[j, c] = jnp.full(m_ref.shape[2:], MASK_VALUE, F32)
            acc_ref[j, c] = jnp.zeros(acc_ref.shape[2:], F32)

    def advance(dst, src, tile):
        for c in cs:
            store_scores(c, dst, scores(c, (tile - 1) * sup))
        accumulate_all(src, tile - 1)

    def body(n, carry):
        advance(1, 0, 2 * n + 1)
        advance(0, 1, 2 * n + 2)
        return carry

    lax.fori_loop(0, lax.shift_right_logical(own_pair, 1), body, 0)

    @pl.when(lax.bitwise_and(own_pair, 1) == 1)
    def _():
        advance(1, 0, own_pair)
        accumulate_all(1, own_pair)

    @pl.when(lax.bitwise_and(own_pair, 1) == 0)
    def _():
        accumulate_all(0, own_pair)

    for c in cs:
        for j in both:
            acc = acc_ref[j, c]
            o = acc / pltpu.roll(acc, HEAD_DIM, 1)
            o_ref[j * blk:(j + 1) * blk, slab(c)] = jnp.where(first_head, o[:blk], o[blk:]).astype(o_ref.dtype)


def _moba_attention(qk, v, batch, seq):
    assert seq % (2 * MOBA_BLOCK) == 0
    nblk = seq // MOBA_BLOCK
    assert nblk <= LANES
    nrow = -(-nblk // BF16_SUBLANES) * BF16_SUBLANES
    slabs = N_HEADS * HEAD_DIM // LANES
    blk = MOBA_BLOCK
    block_id = jnp.arange(seq, dtype=jnp.int32) // blk
    onehot = (block_id[:, None] == jnp.arange(LANES, dtype=jnp.int32)[None, :]).astype(BF16)
    qi = jnp.arange(2 * blk, dtype=jnp.int32)[:, None] % blk
    kj = jnp.arange(2 * blk, dtype=jnp.int32)[None, :]
    own_mask = jnp.stack([jnp.where(kj - qi <= parity * blk, 0.0, MASK_VALUE) for parity in (0, 1)]).astype(F32)
    chains = MOBA_CHAINS
    assert slabs % chains == 0
    width = chains * LANES
    once = dict(pipeline_mode=pl.Buffered(1))
    return pl.pallas_call(
        functools.partial(_moba_kernel, nblk=nblk, chains=chains),
        grid=(batch, slabs // chains, nblk // 2),
        in_specs=[
            pl.BlockSpec((None, 2 * blk, width), lambda b, s, i: (b, i, s)),
            pl.BlockSpec((None, seq, width), lambda b, s, i: (b, 0, slabs // chains + s), **once),
            pl.BlockSpec((None, seq, width), lambda b, s, i: (b, 0, s), **once),
            pl.BlockSpec((seq, LANES), lambda b, s, i: (0, 0), **once),
            pl.BlockSpec((2, 2 * blk, 2 * blk), lambda b, s, i: (0, 0, 0), **once),
        ],
        out_specs=pl.BlockSpec((None, 2 * blk, width), lambda b, s, i: (b, i, s)),
        out_shape=jax.ShapeDtypeStruct((batch, seq, N_HEADS * HEAD_DIM), BF16),
        scratch_shapes=[
            pltpu.VMEM((chains, nrow, LANES), F32), pltpu.VMEM((chains, 2 * nrow, LANES), BF16),
            pltpu.VMEM((chains, HEADS_PER_SLAB, seq, LANES), BF16),
            pltpu.VMEM((2, chains, 2, 2 * blk, 2 * blk), F32),
            pltpu.VMEM((2, chains, 2 * blk, LANES), F32), pltpu.VMEM((2, chains, 2 * blk, LANES), F32),
        ],
        compiler_params=_params(("parallel", "parallel", "arbitrary")),
        name="moba_attention",
    )(qk, qk, v, onehot, own_mask)


def _ffn_kernel(x_ref, xh_ref, a_ref, ah_ref, wo_ref, g_ref, wu_ref, cw_ref, cb_ref, wd_ref, o_ref,
                *, tiles_per_seq, tf, group):
    i = pl.program_id(0)
    halo = CONV_HALO
    d_ff = wd_ref.shape[0]

    def rms(x):
        ms = jnp.mean(x * x, axis=-1, keepdims=True)
        return x * lax.rsqrt(ms + RMS_EPS) * g_ref[...]

    wo = wo_ref[...]
    x1 = x_ref[...] + jnp.dot(a_ref[...], wo, preferred_element_type=F32)
    xh1 = xh_ref[...] + jnp.dot(ah_ref[...], wo, preferred_element_type=F32)[ah_ref.shape[0] - halo:]

    keep = (i % tiles_per_seq != 0).astype(F32)
    hn = jnp.concatenate([(rms(xh1) * keep).astype(BF16), rms(x1).astype(BF16)], axis=0)

    def conv(col):
        u = jnp.dot(hn, wu_ref[:, col:col + tf], preferred_element_type=F32)
        cw = cw_ref[:, col:col + tf]
        return (cb_ref[:, col:col + tf]
                + cw[0:1, :] * pltpu.roll(u, 2, 0)[halo:]
                + cw[1:2, :] * pltpu.roll(u, 1, 0)[halo:]
                + cw[2:3, :] * u[halo:])

    acc = x1
    for lo in range(0, d_ff, group * tf):
        hi = min(lo + group * tf, d_ff)
        acts = []
        for col in range(lo, hi, tf):
            gate = conv(col)
            val = conv(d_ff + col)
            acts.append((gate * jax.nn.sigmoid(gate) * val).astype(BF16))
        acc = acc + jnp.dot(jnp.concatenate(acts, axis=1), wd_ref[lo:hi, :], preferred_element_type=F32)
    o_ref[...] = acc


def _attn_out_conv_ffn(x2, attn2, w_o, gain, w_up, conv_w, conv_b, w_down, seq, tm, tf, group):
    m, d = x2.shape
    da = attn2.shape[1]
    d_ff = w_down.shape[0]
    assert d_ff % tf == 0
    halo_blocks = tm // CONV_HALO
    attn_halo = BF16_SUBLANES
    attn_halo_blocks = tm // attn_halo
    resident = dict(pipeline_mode=pl.Buffered(1))
    return pl.pallas_call(
        functools.partial(_ffn_kernel, tiles_per_seq=seq // tm, tf=tf, group=group),
        grid=(m // tm,),
        in_specs=[
            pl.BlockSpec((tm, d), lambda i: (i, 0)),
            pl.BlockSpec((CONV_HALO, d), lambda i: (jnp.maximum(i * halo_blocks - 1, 0), 0)),
            pl.BlockSpec((tm, da), lambda i: (i, 0)),
            pl.BlockSpec((attn_halo, da), lambda i: (jnp.maximum(i * attn_halo_blocks - 1, 0), 0)),
            pl.BlockSpec((da, d), lambda i: (0, 0), **resident),
            pl.BlockSpec((1, d), lambda i: (0, 0)),
            pl.BlockSpec((d, 2 * d_ff), lambda i: (0, 0), **resident),
            pl.BlockSpec((CONV_WIDTH, 2 * d_ff), lambda i: (0, 0)),
            pl.BlockSpec((1, 2 * d_ff), lambda i: (0, 0)),
            pl.BlockSpec((d_ff, d), lambda i: (0, 0), **resident),
        ],
        out_specs=pl.BlockSpec((tm, d), lambda i: (i, 0)),
        out_shape=jax.ShapeDtypeStruct((m, d), F32),
        compiler_params=_params(("parallel",)),
        name="attn_out_conv_ffn",
    )(x2, x2, attn2, attn2, w_o, gain, w_up, conv_w, conv_b, w_down)


def _rope_tables(seq):
    pos = jnp.arange(seq, dtype=F32)
    inv_freq = ROPE_THETA ** (-jnp.arange(0, ROPE_DIM, 2, dtype=F32) / ROPE_DIM)
    ang = pos[:, None] * inv_freq[None, :]
    cos, sin = jnp.cos(ang), jnp.sin(ang)
    half = ROPE_DIM // 2
    pad = jnp.zeros((seq, HEAD_DIM - ROPE_DIM), F32)
    zeros = jnp.zeros((seq, half), F32)
    cos_h = jnp.concatenate([cos, cos, pad + 1.0], axis=1)
    sa_h = jnp.concatenate([zeros, sin, pad], axis=1)
    sb_h = jnp.concatenate([-sin, zeros, pad], axis=1)
    rep = LANES // HEAD_DIM
    return tuple(jnp.tile(t, (1, rep)) for t in (cos_h, sa_h, sb_h))


def _block_diag_ones():
    r = jnp.arange(MXU_WIDTH) // HEAD_DIM
    return (r[:, None] == r[None, :]).astype(BF16)


def kernel(x, attn_norm, a_w_qkv, a_q_norm, a_k_norm, a_w_o, b_w_qkv, b_q_norm, b_k_norm, b_w_o,
           ffn_norm, ffn_w_up, ffn_conv_w, ffn_conv_b, ffn_w_down):
    batch, seq, d_model = x.shape
    depth = attn_norm.shape[0]
    hd_all = N_HEADS * HEAD_DIM
    rope = _rope_tables(seq)
    bd = _block_diag_ones()
    x2 = x.reshape(batch * seq, d_model)
    tm = 512
    q_scale = ATTN_SCALE * LOG2_E

    for layer in range(depth):
        j = layer // 2
        gain = attn_norm[layer][None, :]
        dilated = layer % 2 == 0
        if dilated:
            n_groups = N_DIL
            qg = jnp.tile(a_q_norm[j][:, None, :], (1, N_HEADS, 1)).reshape(-1) * q_scale
            kg = jnp.tile(a_k_norm[j][:, None, :], (1, N_HEADS, 1)).reshape(-1)
            w_qkv, w_o = a_w_qkv[j], a_w_o[j]
        else:
            n_groups = 1
            qg = jnp.tile(b_q_norm[j], N_HEADS) * q_scale
            kg = jnp.tile(b_k_norm[j], N_HEADS)
            w_qkv, w_o = b_w_qkv[j], b_w_o[j]
        n_qk = 2 * n_groups * hd_all
        colgain = jnp.concatenate([qg, kg])[None, :]
        w_qkv = w_qkv.astype(BF16)
        act_dtype = F32 if dilated else BF16
        qk = _norm_qk_proj(x2, gain, w_qkv, colgain, rope, bd, seq, act_dtype, tm, 512)
        v = _norm_v_proj(x2, gain, w_qkv, n_qk // 2, act_dtype, tm, 512)
        qk, v = qk.reshape(batch, seq, -1), v.reshape(batch, seq, -1)
        attn = _dilated_attention(qk, v, batch, seq) if dilated else _moba_attention(qk, v, batch, seq)
        x2 = _attn_out_conv_ffn(x2, attn.reshape(batch * seq, hd_all), w_o.astype(BF16),
                                ffn_norm[layer][None, :], ffn_w_up[layer].astype(BF16),
                                ffn_conv_w[layer], ffn_conv_b[layer][None, :],
                                ffn_w_down[layer].astype(BF16), seq, 1024, 256, 11)
    return x2.reshape(batch, seq, d_model)
```

```python
import functools
import math

import jax
import jax.numpy as jnp
from jax import lax
from jax.experimental import pallas as pl
from jax.experimental.pallas import tpu as pltpu

N_HEADS = 16
HEAD_DIM = 64
ROPE_DIM = HEAD_DIM // 4
ROPE_THETA = 500000.0
ATTN_SCALE = HEAD_DIM ** -0.5
DILATED_PAIRS = ((128, 1), (512, 4), (2048, 16))
N_DIL = len(DILATED_PAIRS)
BAND_BLOCK = 128
MOBA_BLOCK = 256
MOBA_TOPK = 3
CONV_WIDTH = 3
RMS_EPS = 1e-6

LANES = 128
BF16_SUBLANES = 16
MXU_WIDTH = 256
HEADS_PER_SLAB = LANES // HEAD_DIM
MASK_VALUE = -1e30
MOBA_CHAINS = 4
PROJ_ROWS = 512
PROJ_COLS = 512
FFN_ROWS = 1024
FFN_CHUNK = 256
CONV_HALO = 8
VMEM_LIMIT = 56 * 1024 * 1024
LOG2_E = math.log2(math.e)

F32 = jnp.float32
BF16 = jnp.bfloat16
NT_DIMS = (((1,), (1,)), ((), ()))


def _params(semantics):
    return pltpu.CompilerParams(dimension_semantics=semantics, vmem_limit_bytes=VMEM_LIMIT)


def _rmsnorm_bf16(x_ref, g_ref):
    x = x_ref[...]
    ms = jnp.mean(x * x, axis=-1, keepdims=True)
    return (x * lax.rsqrt(ms + RMS_EPS) * g_ref[...]).astype(BF16)


def _qk_proj_kernel(x_ref, g_ref, w_ref, cg_ref, cos_ref, sa_ref, sb_ref, bd_ref, o_ref, *, tn):
    hn = _rmsnorm_bf16(x_ref, g_ref)
    bd = bd_ref[...]
    half = ROPE_DIM // 2
    for lo in range(0, w_ref.shape[1], tn):
        acc = jnp.dot(hn, w_ref[:, lo:lo + tn], preferred_element_type=F32)
        for c in range(tn // MXU_WIDTH):
            wide = acc[:, c * MXU_WIDTH:(c + 1) * MXU_WIDTH]
            ss_wide = jnp.dot((wide * wide).astype(BF16), bd, preferred_element_type=F32)
            for h in range(MXU_WIDTH // LANES):
                sl = slice(lo + c * MXU_WIDTH + h * LANES, lo + c * MXU_WIDTH + (h + 1) * LANES)
                a = wide[:, h * LANES:(h + 1) * LANES]
                ss = ss_wide[:, h * LANES:(h + 1) * LANES]
                y = a * lax.rsqrt(ss * (1.0 / HEAD_DIM) + RMS_EPS) * cg_ref[:, sl]
                y = (y * cos_ref[...]
                     + pltpu.roll(y, half, 1) * sa_ref[...]
                     + pltpu.roll(y, LANES - half, 1) * sb_ref[...])
                o_ref[:, sl] = y.astype(o_ref.dtype)


def _v_proj_kernel(x_ref, g_ref, w_ref, o_ref, *, tn):
    hn = _rmsnorm_bf16(x_ref, g_ref)
    for lo in range(0, w_ref.shape[1], tn):
        o_ref[:, lo:lo + tn] = jnp.dot(hn, w_ref[:, lo:lo + tn], preferred_element_type=F32).astype(o_ref.dtype)


def _norm_qk_proj(x2, gain, w, colgain, rope, bd, seq, out_dtype, tm, tn):
    m, d = x2.shape
    n = colgain.shape[1]
    cos_t, sa_t, sb_t = rope
    tiles_per_seq = seq // tm
    rope_spec = pl.BlockSpec((tm, LANES), lambda i: (i % tiles_per_seq, 0))
    return pl.pallas_call(
        functools.partial(_qk_proj_kernel, tn=tn),
        grid=(m // tm,),
        in_specs=[
            pl.BlockSpec((tm, d), lambda i: (i, 0)),
            pl.BlockSpec((1, d), lambda i: (0, 0)),
            pl.BlockSpec((d, n), lambda i: (0, 0), pipeline_mode=pl.Buffered(1)),
            pl.BlockSpec((1, n), lambda i: (0, 0)),
            rope_spec, rope_spec, rope_spec,
            pl.BlockSpec((MXU_WIDTH, MXU_WIDTH), lambda i: (0, 0)),
        ],
        out_specs=pl.BlockSpec((tm, n), lambda i: (i, 0)),
        out_shape=jax.ShapeDtypeStruct((m, n), out_dtype),
        compiler_params=_params(("parallel",)),
        name="norm_qk_proj",
    )(x2, gain, w, colgain, cos_t, sa_t, sb_t, bd)


def _norm_v_proj(x2, gain, w, n, out_dtype, tm, tn):
    m, d = x2.shape
    last = w.shape[1] // n - 1
    return pl.pallas_call(
        functools.partial(_v_proj_kernel, tn=tn),
        grid=(m // tm,),
        in_specs=[
            pl.BlockSpec((tm, d), lambda i: (i, 0)),
            pl.BlockSpec((1, d), lambda i: (0, 0)),
            pl.BlockSpec((d, n), lambda i: (0, last), pipeline_mode=pl.Buffered(1)),
        ],
        out_specs=pl.BlockSpec((tm, n), lambda i: (i, 0)),
        out_shape=jax.ShapeDtypeStruct((m, n), out_dtype),
        compiler_params=_params(("parallel",)),
        name="norm_v_proj",
    )(x2, gain, w)


def _dilated_kernel(q0, q1, q2, k0, k1, k2, v0, v1, v2, o_ref, num_ref, m_ref, den_ref, mask_ref, *, tq):
    t = pl.program_id(2)
    q_refs, k_refs, v_refs = (q0, q1, q2), (k0, k1, k2), (v0, v1, v2)
    blk = BAND_BLOCK

    lane = lax.broadcasted_iota(jnp.int32, (blk, LANES), 1)
    first_head = lane < HEAD_DIM
    ones_rhs = jnp.ones((2 * blk, LANES), BF16)

    reaches = sorted({w // d for w, d in DILATED_PAIRS})
    qi = lax.broadcasted_iota(jnp.int32, (2 * blk, 2 * blk), 0) % blk
    kj = lax.broadcasted_iota(jnp.int32, (2 * blk, 2 * blk), 1)
    dist = qi + blk - kj
    for n, reach in enumerate(reaches):
        band = (dist >= 0) & (dist <= reach)
        mask_ref[n, 0] = jnp.where(band, 0.0, MASK_VALUE)
        mask_ref[n, 1] = jnp.where(band & (kj >= blk), 0.0, MASK_VALUE)

    for g, (window, dil) in enumerate(DILATED_PAIRS):
        reach = window // dil
        span = blk * dil
        assert dil & (dil - 1) == 0 and reach <= blk
        qg, kg, vg = q_refs[g], k_refs[g], v_refs[g]
        mask_g = reaches.index(reach)

        def load_kv(base):
            if dil == 1:
                base = pl.multiple_of(base, blk)
            rows = pl.ds(base, blk, stride=dil)
            return kg[rows, :].astype(BF16), vg[rows, :].astype(BF16)

        for r in range(dil):
            prev = None
            for u in range(tq // span):
                qbase = u * span + r
                kbase = t * tq + qbase
                kc, vc = load_kv(kbase)
                if prev is None:
                    pbase = kbase - span
                    no_prev = (pbase < 0).astype(jnp.int32)
                    kp, vp = load_kv(jnp.maximum(pbase, 0))
                else:
                    no_prev = 0
                    kp, vp = prev
                prev = (kc, vc)
                q = qg[pl.ds(qbase, blk, stride=dil), :]
                zero = jnp.zeros_like(q)
                q2 = jnp.concatenate([jnp.where(first_head, q, zero),
                                      jnp.where(first_head, zero, q)], axis=0).astype(BF16)
                kcat = jnp.concatenate([kp, kc], axis=0)
                vcat = jnp.concatenate([jnp.concatenate([vp, vc], axis=0), ones_rhs], axis=1)
                s = lax.dot_general(q2, kcat, NT_DIMS, preferred_element_type=F32)
                s = s + mask_ref[mask_g, no_prev]
                m = jnp.max(s, axis=-1, keepdims=True)
                p = jnp.exp2(s - m).astype(BF16)
                ov = jnp.dot(p, vcat, preferred_element_type=F32)
                mb = jnp.broadcast_to(m, (2 * blk, LANES))
                rows = pl.ds(qbase, blk, stride=dil)
                num_ref[g, rows, :] = jnp.where(first_head, ov[:blk, :LANES], ov[blk:, :LANES])
                den_ref[g, rows, :] = jnp.where(first_head, ov[:blk, LANES:], ov[blk:, LANES:])
                m_ref[g, rows, :] = jnp.where(first_head, mb[:blk], mb[blk:])

    rows = 2 * blk

    def merge(c, carry):
        sl = pl.ds(pl.multiple_of(c * rows, rows), rows)
        ms = [m_ref[g, sl, :] for g in range(N_DIL)]
        mx = functools.reduce(jnp.maximum, ms)
        ws = [jnp.exp2(mg - mx) for mg in ms]
        num = sum(w * num_ref[g, sl, :] for g, w in enumerate(ws))
        den = sum(w * den_ref[g, sl, :] for g, w in enumerate(ws))
        o_ref[sl, :] = (num / den).astype(o_ref.dtype)
        return carry

    lax.fori_loop(0, tq // rows, merge, 0)


def _dilated_attention(qk, v, batch, seq):
    span_max = BAND_BLOCK * max(d for _, d in DILATED_PAIRS)
    tq = span_max
    assert seq % tq == 0
    slabs = N_HEADS // HEADS_PER_SLAB
    n_reach = len({w // d for w, d in DILATED_PAIRS})

    q_specs = [pl.BlockSpec((None, tq, LANES), lambda b, s, t, g=g: (b, t, g * slabs + s)) for g in range(N_DIL)]
    k_specs = [pl.BlockSpec((None, seq, LANES), lambda b, s, t, g=g: (b, 0, (N_DIL + g) * slabs + s))
               for g in range(N_DIL)]
    v_specs = [pl.BlockSpec((None, seq, LANES), lambda b, s, t, g=g: (b, 0, g * slabs + s)) for g in range(N_DIL)]
    stat = pltpu.VMEM((N_DIL, tq, LANES), F32)
    return pl.pallas_call(
        functools.partial(_dilated_kernel, tq=tq),
        grid=(batch, slabs, seq // tq),
        in_specs=q_specs + k_specs + v_specs,
        out_specs=pl.BlockSpec((None, tq, LANES), lambda b, s, t: (b, t, s)),
        out_shape=jax.ShapeDtypeStruct((batch, seq, N_HEADS * HEAD_DIM), BF16),
        scratch_shapes=[stat, stat, stat, pltpu.VMEM((n_reach, 2, 2 * BAND_BLOCK, 2 * BAND_BLOCK), F32)],
        compiler_params=_params(("parallel", "parallel", "arbitrary")),
        name="dilated_attention",
    )(*([qk] * 6 + [v] * 3))


def _moba_kernel(q_ref, k_ref, v_ref, oh_ref, mask_ref, o_ref, km_ref, kmhl_ref, vaug_ref, s_ref, m_ref, acc_ref,
                 *, nblk, chains):
    own_pair = pl.program_id(2)
    blk = MOBA_BLOCK
    both = range(2)
    nrow = km_ref.shape[1]
    sup = 2 * blk
    lane = lax.broadcasted_iota(jnp.int32, (blk, LANES), 1)
    first_head = lane < HEAD_DIM
    cs = range(chains)

    def slab(c):
        return slice(c * LANES, (c + 1) * LANES)

    @pl.when(own_pair == 0)
    def _():
        for c in cs:
            km_ref[c] = jnp.zeros(km_ref.shape[1:], F32)
            for j in range(nblk):
                rows = slice(j * blk, (j + 1) * blk)
                km_ref[c, j:j + 1, :] = jnp.sum(k_ref[rows, slab(c)].astype(F32), axis=0,
                                                keepdims=True) * (1.0 / blk)
                vj = v_ref[rows, slab(c)].astype(F32)
                vaug_ref[c, 0, rows, :] = jnp.where(first_head, vj, 1.0).astype(BF16)
                vaug_ref[c, 1, rows, :] = jnp.where(first_head, 1.0, vj).astype(BF16)
            km = km_ref[c]
            hi = km.astype(BF16)
            kmhl_ref[c] = jnp.concatenate([hi, (km - hi.astype(F32)).astype(BF16)], axis=0)

    row = lax.broadcasted_iota(jnp.int32, (nrow, 2 * blk), 0)
    rowf = row.astype(F32)

    def augmented_queries(j, c):
        i = 2 * own_pair + j
        q = q_ref[j * blk:(j + 1) * blk, slab(c)]
        zero = jnp.zeros_like(q)
        q2 = jnp.concatenate([jnp.where(first_head, q, zero), jnp.where(first_head, zero, q)], axis=0)
        gate2 = lax.dot_general(kmhl_ref[c], q2, NT_DIMS, preferred_element_type=F32)
        gate = gate2[:nrow] + gate2[nrow:]
        remaining = row < i
        sel = row >= i
        for _ in range(MOBA_TOPK):
            gm = jnp.max(jnp.where(remaining, gate, -jnp.inf), axis=0, keepdims=True)
            cand = remaining & (gate == gm)
            first = jnp.min(jnp.where(cand, rowf, float(LANES)), axis=0, keepdims=True)
            pick = rowf == first
            sel = sel | pick
            remaining = remaining & jnp.logical_not(pick)
        bias_t = jnp.concatenate([jnp.where(sel, 0.0, MASK_VALUE),
                                  jnp.zeros((LANES - nrow, 2 * blk), F32)], axis=0)
        return jnp.concatenate([q2, bias_t.T.astype(BF16)], axis=1)

    q_aug = [jnp.concatenate([augmented_queries(j, c) for j in both], axis=0) for c in cs]

    own_base = own_pair * sup

    def scores(c, base):
        rows = pl.ds(pl.multiple_of(base, sup), sup)
        k_aug = jnp.concatenate([k_ref[rows, slab(c)], oh_ref[rows, :]], axis=1)
        return lax.dot_general(q_aug[c], k_aug, NT_DIMS, preferred_element_type=F32)

    def store_scores(c, buf, s, masked=False):
        for j in both:
            sj = s[j * sup:(j + 1) * sup]
            s_ref[j, c, buf] = sj + mask_ref[j] if masked else sj

    def accumulate(j, c, buf, tile):
        base = jnp.where(tile == 0, own_base, (tile - 1) * sup)
        s = s_ref[j, c, buf]
        m_old = m_ref[j, c]
        m_new = jnp.maximum(m_old, jnp.max(s, axis=-1, keepdims=True))
        alpha = jnp.exp2(m_old - m_new)
        p = jnp.exp2(s - jnp.concatenate([m_new] * (sup // LANES), axis=1)).astype(BF16)
        rows = pl.ds(pl.multiple_of(base, sup), sup)
        pv = jnp.concatenate([jnp.dot(p[:blk], vaug_ref[c, 0, rows, :], preferred_element_type=F32),
                              jnp.dot(p[blk:], vaug_ref[c, 1, rows, :], preferred_element_type=F32)], axis=0)
        acc_ref[j, c] = acc_ref[j, c] * alpha + pv
        m_ref[j, c] = m_new

    def accumulate_all(buf, tile):
        for c in cs:
            for j in both:
                accumulate(j, c, buf, tile)

    for c in cs:
        store_scores(c, 0, scores(c, own_base), masked=True)
        for j in both:
            m_ref[j, c] = jnp.full(m_ref.shape[2:], MASK_VALUE, F32)
            acc_ref[j, c] = jnp.zeros(acc_ref.shape[2:], F32)

    def advance(dst, src, tile):
        for c in cs:
            store_scores(c, dst, scores(c, (tile - 1) * sup))
        accumulate_all(src, tile - 1)

    def body(n, carry):
        advance(1, 0, 2 * n + 1)
        advance(0, 1, 2 * n + 2)
        return carry

    lax.fori_loop(0, lax.shift_right_logical(own_pair, 1), body, 0)

    @pl.when(lax.bitwise_and(own_pair, 1) == 1)
    def _():
        advance(1, 0, own_pair)
        accumulate_all(1, own_pair)

    @pl.when(lax.bitwise_and(own_pair, 1) == 0)
    def _():
        accumulate_all(0, own_pair)

    for c in cs:
        for j in both:
            acc = acc_ref[j, c]
            o = acc / pltpu.roll(acc, HEAD_DIM, 1)
            o_ref[j * blk:(j + 1) * blk, slab(c)] = jnp.where(first_head, o[:blk], o[blk:]).astype(o_ref.dtype)


def _moba_attention(qk, v, batch, seq):
    assert seq % (2 * MOBA_BLOCK) == 0
    nblk = seq // MOBA_BLOCK
    assert nblk <= LANES
    nrow = -(-nblk // BF16_SUBLANES) * BF16_SUBLANES
    slabs = N_HEADS * HEAD_DIM // LANES
    blk = MOBA_BLOCK
    block_id = jnp.arange(seq, dtype=jnp.int32) // blk
    onehot = (block_id[:, None] == jnp.arange(LANES, dtype=jnp.int32)[None, :]).astype(BF16)
    qi = jnp.arange(2 * blk, dtype=jnp.int32)[:, None] % blk
    kj = jnp.arange(2 * blk, dtype=jnp.int32)[None, :]
    own_mask = jnp.stack([jnp.where(kj - qi <= parity * blk, 0.0, MASK_VALUE) for parity in (0, 1)]).astype(F32)
    chains = MOBA_CHAINS
    assert slabs % chains == 0
    width = chains * LANES
    once = dict(pipeline_mode=pl.Buffered(1))
    return pl.pallas_call(
        functools.partial(_moba_kernel, nblk=nblk, chains=chains),
        grid=(batch, slabs // chains, nblk // 2),
        in_specs=[
            pl.BlockSpec((None, 2 * blk, width), lambda b, s, i: (b, i, s)),
            pl.BlockSpec((None, seq, width), lambda b, s, i: (b, 0, slabs // chains + s), **once),
            pl.BlockSpec((None, seq, width), lambda b, s, i: (b, 0, s), **once),
            pl.BlockSpec((seq, LANES), lambda b, s, i: (0, 0), **once),
            pl.BlockSpec((2, 2 * blk, 2 * blk), lambda b, s, i: (0, 0, 0), **once),
        ],
        out_specs=pl.BlockSpec((None, 2 * blk, width), lambda b, s, i: (b, i, s)),
        out_shape=jax.ShapeDtypeStruct((batch, seq, N_HEADS * HEAD_DIM), BF16),
        scratch_shapes=[
            pltpu.VMEM((chains, nrow, LANES), F32), pltpu.VMEM((chains, 2 * nrow, LANES), BF16),
            pltpu.VMEM((chains, HEADS_PER_SLAB, seq, LANES), BF16),
            pltpu.VMEM((2, chains, 2, 2 * blk, 2 * blk), F32),
            pltpu.VMEM((2, chains, 2 * blk, LANES), F32), pltpu.VMEM((2, chains, 2 * blk, LANES), F32),
        ],
        compiler_params=_params(("parallel", "parallel", "arbitrary")),
        name="moba_attention",
    )(qk, qk, v, onehot, own_mask)


def _ffn_kernel(x_ref, xh_ref, a_ref, ah_ref, wo_ref, g_ref, wu_ref, cw_ref, cb_ref, wd_ref, o_ref,
                *, tiles_per_seq, tf):
    i = pl.program_id(0)
    halo = CONV_HALO
    d_ff = wd_ref.shape[0]

    def rms(x):
        ms = jnp.mean(x * x, axis=-1, keepdims=True)
        return x * lax.rsqrt(ms + RMS_EPS) * g_ref[...]

    wo = wo_ref[...]
    x1 = x_ref[...] + jnp.dot(a_ref[...], wo, preferred_element_type=F32)
    xh1 = xh_ref[...] + jnp.dot(ah_ref[...], wo, preferred_element_type=F32)[ah_ref.shape[0] - halo:]

    keep = (i % tiles_per_seq != 0).astype(F32)
    hn = jnp.concatenate([(rms(xh1) * keep).astype(BF16), rms(x1).astype(BF16)], axis=0)

    def conv(col):
        u = jnp.dot(hn, wu_ref[:, col:col + tf], preferred_element_type=F32)
        cw = cw_ref[:, col:col + tf]
        return (cb_ref[:, col:col + tf]
                + cw[0:1, :] * pltpu.roll(u, 2, 0)[halo:]
                + cw[1:2, :] * pltpu.roll(u, 1, 0)[halo:]
                + cw[2:3, :] * u[halo:])

    acts = []
    for col in range(0, d_ff, tf):
        gate = conv(col)
        val = conv(d_ff + col)
        acts.append((gate * jax.nn.sigmoid(gate) * val).astype(BF16))
    o_ref[...] = x1 + jnp.dot(jnp.concatenate(acts, axis=1), wd_ref[...], preferred_element_type=F32)


def _attn_out_conv_ffn(x2, attn2, w_o, gain, w_up, conv_w, conv_b, w_down, seq, tm, tf):
    m, d = x2.shape
    da = attn2.shape[1]
    d_ff = w_down.shape[0]
    assert d_ff % tf == 0
    halo_blocks = tm // CONV_HALO
    attn_halo = BF16_SUBLANES
    attn_halo_blocks = tm // attn_halo
    resident = dict(pipeline_mode=pl.Buffered(1))
    return pl.pallas_call(
        functools.partial(_ffn_kernel, tiles_per_seq=seq // tm, tf=tf),
        grid=(m // tm,),
        in_specs=[
            pl.BlockSpec((tm, d), lambda i: (i, 0)),
            pl.BlockSpec((CONV_HALO, d), lambda i: (jnp.maximum(i * halo_blocks - 1, 0), 0)),
            pl.BlockSpec((tm, da), lambda i: (i, 0)),
            pl.BlockSpec((attn_halo, da), lambda i: (jnp.maximum(i * attn_halo_blocks - 1, 0), 0)),
            pl.BlockSpec((da, d), lambda i: (0, 0), **resident),
            pl.BlockSpec((1, d), lambda i: (0, 0)),
            pl.BlockSpec((d, 2 * d_ff), lambda i: (0, 0), **resident),
            pl.BlockSpec((CONV_WIDTH, 2 * d_ff), lambda i: (0, 0)),
            pl.BlockSpec((1, 2 * d_ff), lambda i: (0, 0)),
            pl.BlockSpec((d_ff, d), lambda i: (0, 0), **resident),
        ],
        out_specs=pl.BlockSpec((tm, d), lambda i: (i, 0)),
        out_shape=jax.ShapeDtypeStruct((m, d), F32),
        compiler_params=_params(("parallel",)),
        name="attn_out_conv_ffn",
    )(x2, x2, attn2, attn2, w_o, gain, w_up, conv_w, conv_b, w_down)


def _rope_tables(seq):
    pos = jnp.arange(seq, dtype=F32)
    inv_freq = ROPE_THETA ** (-jnp.arange(0, ROPE_DIM, 2, dtype=F32) / ROPE_DIM)
    ang = pos[:, None] * inv_freq[None, :]
    cos, sin = jnp.cos(ang), jnp.sin(ang)
    half = ROPE_DIM // 2
    pad = jnp.zeros((seq, HEAD_DIM - ROPE_DIM), F32)
    zeros = jnp.zeros((seq, half), F32)
    cos_h = jnp.concatenate([cos, cos, pad + 1.0], axis=1)
    sa_h = jnp.concatenate([zeros, sin, pad], axis=1)
    sb_h = jnp.concatenate([-sin, zeros, pad], axis=1)
    rep = LANES // HEAD_DIM
    return tuple(jnp.tile(t, (1, rep)) for t in (cos_h, sa_h, sb_h))


def _block_diag_ones():
    r = jnp.arange(MXU_WIDTH) // HEAD_DIM
    return (r[:, None] == r[None, :]).astype(BF16)


def kernel(x, attn_norm, a_w_qkv, a_q_norm, a_k_norm, a_w_o, b_w_qkv, b_q_norm, b_k_norm, b_w_o,
           ffn_norm, ffn_w_up, ffn_conv_w, ffn_conv_b, ffn_w_down):
    batch, seq, d_model = x.shape
    depth = attn_norm.shape[0]
    hd_all = N_HEADS * HEAD_DIM
    rope = _rope_tables(seq)
    bd = _block_diag_ones()
    x2 = x.reshape(batch * seq, d_model)
    q_scale = ATTN_SCALE * LOG2_E

    for layer in range(depth):
        j = layer // 2
        gain = attn_norm[layer][None, :]
        dilated = layer % 2 == 0
        if dilated:
            n_groups = N_DIL
            qg = jnp.tile(a_q_norm[j][:, None, :], (1, N_HEADS, 1)).reshape(-1) * q_scale
            kg = jnp.tile(a_k_norm[j][:, None, :], (1, N_HEADS, 1)).reshape(-1)
            w_qkv, w_o = a_w_qkv[j], a_w_o[j]
        else:
            n_groups = 1
            qg = jnp.tile(b_q_norm[j], N_HEADS) * q_scale
            kg = jnp.tile(b_k_norm[j], N_HEADS)
            w_qkv, w_o = b_w_qkv[j], b_w_o[j]
        n_qk = 2 * n_groups * hd_all
        colgain = jnp.concatenate([qg, kg])[None, :]
        w_qkv = w_qkv.astype(BF16)
        act_dtype = F32 if dilated else BF16
        qk = _norm_qk_proj(x2, gain, w_qkv, colgain, rope, bd, seq, act_dtype, PROJ_ROWS, PROJ_COLS)
        v = _norm_v_proj(x2, gain, w_qkv, n_qk // 2, act_dtype, PROJ_ROWS, PROJ_COLS)
        qk, v = qk.reshape(batch, seq, -1), v.reshape(batch, seq, -1)
        attn = _dilated_attention(qk, v, batch, seq) if dilated else _moba_attention(qk, v, batch, seq)
        x2 = _attn_out_conv_ffn(x2, attn.reshape(batch * seq, hd_all), w_o.astype(BF16),
                                ffn_norm[layer][None, :], ffn_w_up[layer].astype(BF16),
                                ffn_conv_w[layer], ffn_conv_b[layer][None, :],
                                ffn_w_down[layer].astype(BF16), seq, FFN_ROWS, FFN_CHUNK)
    return x2.reshape(batch, seq, d_model)
```

```python
import functools
import math

import jax
import jax.numpy as jnp
from jax import lax
from jax.experimental import pallas as pl
from jax.experimental.pallas import tpu as pltpu

N_HEADS = 16
HEAD_DIM = 64
ROPE_DIM = HEAD_DIM // 4
ROPE_THETA = 500000.0
ATTN_SCALE = HEAD_DIM ** -0.5
DILATED_PAIRS = ((128, 1), (512, 4), (2048, 16))
N_DIL = len(DILATED_PAIRS)
BAND_BLOCK = 128
MOBA_BLOCK = 256
MOBA_TOPK = 3
CONV_WIDTH = 3
RMS_EPS = 1e-6

LANES = 128
BF16_SUBLANES = 16
MXU_WIDTH = 256
HEADS_PER_SLAB = LANES // HEAD_DIM
MASK_VALUE = -1e30
MOBA_CHAINS = 4
PROJ_ROWS = 512
PROJ_COLS = 512
FFN_ROWS = 1024
FFN_CHUNK = 256
CONV_HALO = 8
VMEM_LIMIT = 56 * 1024 * 1024
LOG2_E = math.log2(math.e)

F32 = jnp.float32
BF16 = jnp.bfloat16
NT_DIMS = (((1,), (1,)), ((), ()))


def _params(semantics):
    return pltpu.CompilerParams(dimension_semantics=semantics, vmem_limit_bytes=VMEM_LIMIT)


def _rmsnorm_bf16(x_ref, g_ref):
    x = x_ref[...]
    ms = jnp.mean(x * x, axis=-1, keepdims=True)
    return (x * lax.rsqrt(ms + RMS_EPS) * g_ref[...]).astype(BF16)


def _qk_proj_kernel(x_ref, g_ref, w_ref, cg_ref, cos_ref, sa_ref, sb_ref, bd_ref, o_ref, *, tn):
    hn = _rmsnorm_bf16(x_ref, g_ref)
    bd = bd_ref[...]
    half = ROPE_DIM // 2
    for lo in range(0, w_ref.shape[1], tn):
        acc = jnp.dot(hn, w_ref[:, lo:lo + tn], preferred_element_type=F32)
        for c in range(tn // MXU_WIDTH):
            wide = acc[:, c * MXU_WIDTH:(c + 1) * MXU_WIDTH]
            ss_wide = jnp.dot((wide * wide).astype(BF16), bd, preferred_element_type=F32)
            for h in range(MXU_WIDTH // LANES):
                sl = slice(lo + c * MXU_WIDTH + h * LANES, lo + c * MXU_WIDTH + (h + 1) * LANES)
                a = wide[:, h * LANES:(h + 1) * LANES]
                ss = ss_wide[:, h * LANES:(h + 1) * LANES]
                y = a * lax.rsqrt(ss * (1.0 / HEAD_DIM) + RMS_EPS) * cg_ref[:, sl]
                y = (y * cos_ref[...]
                     + pltpu.roll(y, half, 1) * sa_ref[...]
                     + pltpu.roll(y, LANES - half, 1) * sb_ref[...])
                o_ref[:, sl] = y.astype(o_ref.dtype)


def _v_proj_kernel(x_ref, g_ref, w_ref, o_ref, *, tn):
    hn = _rmsnorm_bf16(x_ref, g_ref)
    for lo in range(0, w_ref.shape[1], tn):
        o_ref[:, lo:lo + tn] = jnp.dot(hn, w_ref[:, lo:lo + tn], preferred_element_type=F32).astype(o_ref.dtype)


def _norm_qk_proj(x2, gain, w, colgain, rope, bd, seq, out_dtype, tm, tn):
    m, d = x2.shape
    n = colgain.shape[1]
    cos_t, sa_t, sb_t = rope
    tiles_per_seq = seq // tm
    rope_spec = pl.BlockSpec((tm, LANES), lambda i: (i % tiles_per_seq, 0))
    return pl.pallas_call(
        functools.partial(_qk_proj_kernel, tn=tn),
        grid=(m // tm,),
        in_specs=[
            pl.BlockSpec((tm, d), lambda i: (i, 0)),
            pl.BlockSpec((1, d), lambda i: (0, 0)),
            pl.BlockSpec((d, n), lambda i: (0, 0), pipeline_mode=pl.Buffered(1)),
            pl.BlockSpec((1, n), lambda i: (0, 0)),
            rope_spec, rope_spec, rope_spec,
            pl.BlockSpec((MXU_WIDTH, MXU_WIDTH), lambda i: (0, 0)),
        ],
        out_specs=pl.BlockSpec((tm, n), lambda i: (i, 0)),
        out_shape=jax.ShapeDtypeStruct((m, n), out_dtype),
        compiler_params=_params(("parallel",)),
        name="norm_qk_proj",
    )(x2, gain, w, colgain, cos_t, sa_t, sb_t, bd)


def _norm_v_proj(x2, gain, w, n, out_dtype, tm, tn):
    m, d = x2.shape
    last = w.shape[1] // n - 1
    return pl.pallas_call(
        functools.partial(_v_proj_kernel, tn=tn),
        grid=(m // tm,),
        in_specs=[
            pl.BlockSpec((tm, d), lambda i: (i, 0)),
            pl.BlockSpec((1, d), lambda i: (0, 0)),
            pl.BlockSpec((d, n), lambda i: (0, last), pipeline_mode=pl.Buffered(1)),
        ],
        out_specs=pl.BlockSpec((tm, n), lambda i: (i, 0)),
        out_shape=jax.ShapeDtypeStruct((m, n), out_dtype),
        compiler_params=_params(("parallel",)),
        name="norm_v_proj",
    )(x2, gain, w)


def _dilated_kernel(q0, q1, q2, k0, k1, k2, v0, v1, v2, o_ref, num_ref, m_ref, den_ref, mask_ref, *, tq):
    t = pl.program_id(2)
    q_refs, k_refs, v_refs = (q0, q1, q2), (k0, k1, k2), (v0, v1, v2)
    blk = BAND_BLOCK

    lane = lax.broadcasted_iota(jnp.int32, (blk, LANES), 1)
    first_head = lane < HEAD_DIM
    ones_rhs = jnp.ones((2 * blk, LANES), BF16)

    reaches = sorted({w // d for w, d in DILATED_PAIRS})
    qi = lax.broadcasted_iota(jnp.int32, (2 * blk, 2 * blk), 0) % blk
    kj = lax.broadcasted_iota(jnp.int32, (2 * blk, 2 * blk), 1)
    dist = qi + blk - kj
    for n, reach in enumerate(reaches):
        band = (dist >= 0) & (dist <= reach)
        mask_ref[n, 0] = jnp.where(band, 0.0, MASK_VALUE)
        mask_ref[n, 1] = jnp.where(band & (kj >= blk), 0.0, MASK_VALUE)

    for g, (window, dil) in enumerate(DILATED_PAIRS):
        reach = window // dil
        span = blk * dil
        assert dil & (dil - 1) == 0 and reach <= blk
        qg, kg, vg = q_refs[g], k_refs[g], v_refs[g]
        mask_g = reaches.index(reach)

        def load_kv(base):
            if dil == 1:
                base = pl.multiple_of(base, blk)
            rows = pl.ds(base, blk, stride=dil)
            return kg[rows, :].astype(BF16), vg[rows, :].astype(BF16)

        for r in range(dil):
            prev = None
            for u in range(tq // span):
                qbase = u * span + r
                kbase = t * tq + qbase
                kc, vc = load_kv(kbase)
                if prev is None:
                    pbase = kbase - span
                    no_prev = (pbase < 0).astype(jnp.int32)
                    kp, vp = load_kv(jnp.maximum(pbase, 0))
                else:
                    no_prev = 0
                    kp, vp = prev
                prev = (kc, vc)
                q = qg[pl.ds(qbase, blk, stride=dil), :]
                zero = jnp.zeros_like(q)
                q2 = jnp.concatenate([jnp.where(first_head, q, zero),
                                      jnp.where(first_head, zero, q)], axis=0).astype(BF16)
                kcat = jnp.concatenate([kp, kc], axis=0)
                vcat = jnp.concatenate([jnp.concatenate([vp, vc], axis=0), ones_rhs], axis=1)
                s = lax.dot_general(q2, kcat, NT_DIMS, preferred_element_type=F32)
                s = s + mask_ref[mask_g, no_prev]
                m = jnp.max(s, axis=-1, keepdims=True)
                p = jnp.exp2(s - m).astype(BF16)
                ov = jnp.dot(p, vcat, preferred_element_type=F32)
                mb = jnp.broadcast_to(m, (2 * blk, LANES))
                rows = pl.ds(qbase, blk, stride=dil)
                num_ref[g, rows, :] = jnp.where(first_head, ov[:blk, :LANES], ov[blk:, :LANES])
                den_ref[g, rows, :] = jnp.where(first_head, ov[:blk, LANES:], ov[blk:, LANES:])
                m_ref[g, rows, :] = jnp.where(first_head, mb[:blk], mb[blk:])

    rows = 2 * blk

    def merge(c, carry):
        sl = pl.ds(pl.multiple_of(c * rows, rows), rows)
        ms = [m_ref[g, sl, :] for g in range(N_DIL)]
        mx = functools.reduce(jnp.maximum, ms)
        ws = [jnp.exp2(mg - mx) for mg in ms]
        num = sum(w * num_ref[g, sl, :] for g, w in enumerate(ws))
        den = sum(w * den_ref[g, sl, :] for g, w in enumerate(ws))
        o_ref[sl, :] = (num / den).astype(o_ref.dtype)
        return carry

    lax.fori_loop(0, tq // rows, merge, 0)


def _dilated_attention(qk, v, batch, seq):
    span_max = BAND_BLOCK * max(d for _, d in DILATED_PAIRS)
    tq = span_max
    assert seq % tq == 0
    slabs = N_HEADS // HEADS_PER_SLAB
    n_reach = len({w // d for w, d in DILATED_PAIRS})

    q_specs = [pl.BlockSpec((None, tq, LANES), lambda b, s, t, g=g: (b, t, g * slabs + s)) for g in range(N_DIL)]
    k_specs = [pl.BlockSpec((None, seq, LANES), lambda b, s, t, g=g: (b, 0, (N_DIL + g) * slabs + s))
               for g in range(N_DIL)]
    v_specs = [pl.BlockSpec((None, seq, LANES), lambda b, s, t, g=g: (b, 0, g * slabs + s)) for g in range(N_DIL)]
    stat = pltpu.VMEM((N_DIL, tq, LANES), F32)
    return pl.pallas_call(
        functools.partial(_dilated_kernel, tq=tq),
        grid=(batch, slabs, seq // tq),
        in_specs=q_specs + k_specs + v_specs,
        out_specs=pl.BlockSpec((None, tq, LANES), lambda b, s, t: (b, t, s)),
        out_shape=jax.ShapeDtypeStruct((batch, seq, N_HEADS * HEAD_DIM), BF16),
        scratch_shapes=[stat, stat, stat, pltpu.VMEM((n_reach, 2, 2 * BAND_BLOCK, 2 * BAND_BLOCK), F32)],
        compiler_params=_params(("parallel", "parallel", "arbitrary")),
        name="dilated_attention",
    )(*([qk] * 6 + [v] * 3))


def _moba_kernel(q_ref, k_ref, v_ref, oh_ref, mask_ref, o_ref, km_ref, kmhl_ref, vaug_ref, s_ref, m_ref, acc_ref,
                 *, nblk, chains):
    own_pair = pl.program_id(2)
    blk = MOBA_BLOCK
    both = range(2)
    nrow = km_ref.shape[1]
    sup = 2 * blk
    lane = lax.broadcasted_iota(jnp.int32, (blk, LANES), 1)
    first_head = lane < HEAD_DIM
    cs = range(chains)

    def slab(c):
        return slice(c * LANES, (c + 1) * LANES)

    @pl.when(own_pair == 0)
    def _():
        for c in cs:
            km_ref[c] = jnp.zeros(km_ref.shape[1:], F32)
            for j in range(nblk):
                rows = slice(j * blk, (j + 1) * blk)
                km_ref[c, j:j + 1, :] = jnp.sum(k_ref[rows, slab(c)].astype(F32), axis=0,
                                                keepdims=True) * (1.0 / blk)
                vj = v_ref[rows, slab(c)].astype(F32)
                vaug_ref[c, 0, rows, :] = jnp.where(first_head, vj, 1.0).astype(BF16)
                vaug_ref[c, 1, rows, :] = jnp.where(first_head, 1.0, vj).astype(BF16)
            km = km_ref[c]
            hi = km.astype(BF16)
            kmhl_ref[c] = jnp.concatenate([hi, (km - hi.astype(F32)).astype(BF16)], axis=0)

    row = lax.broadcasted_iota(jnp.int32, (nrow, 2 * blk), 0)
    rowf = row.astype(F32)

    def augmented_queries(j, c):
        i = 2 * own_pair + j
        q = q_ref[j * blk:(j + 1) * blk, slab(c)]
        zero = jnp.zeros_like(q)
        q2 = jnp.concatenate([jnp.where(first_head, q, zero), jnp.where(first_head, zero, q)], axis=0)
        gate2 = lax.dot_general(kmhl_ref[c], q2, NT_DIMS, preferred_element_type=F32)
        gate = gate2[:nrow] + gate2[nrow:]
        remaining = row < i
        sel = row >= i
        for _ in range(MOBA_TOPK):
            gm = jnp.max(jnp.where(remaining, gate, -jnp.inf), axis=0, keepdims=True)
            cand = remaining & (gate == gm)
            first = jnp.min(jnp.where(cand, rowf, float(LANES)), axis=0, keepdims=True)
            pick = rowf == first
            sel = sel | pick
            remaining = remaining & jnp.logical_not(pick)
        bias_t = jnp.concatenate([jnp.where(sel, 0.0, MASK_VALUE),
                                  jnp.zeros((LANES - nrow, 2 * blk), F32)], axis=0)
        return jnp.concatenate([q2, bias_t.T.astype(BF16)], axis=1)

    q_aug = [jnp.concatenate([augmented_queries(j, c) for j in both], axis=0) for c in cs]

    own_base = own_pair * sup

    def scores(c, base):
        rows = pl.ds(pl.multiple_of(base, sup), sup)
        k_aug = jnp.concatenate([k_ref[rows, slab(c)], oh_ref[rows, :]], axis=1)
        return lax.dot_general(q_aug[c], k_aug, NT_DIMS, preferred_element_type=F32)

    def store_scores(c, buf, s, masked=False):
        for j in both:
            sj = s[j * sup:(j + 1) * sup]
            s_ref[j, c, buf] = sj + mask_ref[j] if masked else sj

    def accumulate(c, buf, tile):
        base = jnp.where(tile == 0, own_base, (tile - 1) * sup)
        rows = pl.ds(pl.multiple_of(base, sup), sup)
        probs, alphas = [], []
        for j in both:
            s = s_ref[j, c, buf]
            m_old = m_ref[j, c]
            m_new = jnp.maximum(m_old, jnp.max(s, axis=-1, keepdims=True))
            alphas.append(jnp.exp2(m_old - m_new))
            probs.append(jnp.exp2(s - jnp.concatenate([m_new] * (sup // LANES), axis=1)).astype(BF16))
            m_ref[j, c] = m_new
        pv = [jnp.dot(jnp.concatenate([p[h * blk:(h + 1) * blk] for p in probs], axis=0),
                      vaug_ref[c, h, rows, :], preferred_element_type=F32) for h in range(HEADS_PER_SLAB)]
        for j in both:
            pv_j = jnp.concatenate([pv_h[j * blk:(j + 1) * blk] for pv_h in pv], axis=0)
            acc_ref[j, c] = acc_ref[j, c] * alphas[j] + pv_j

    def accumulate_all(buf, tile):
        for c in cs:
            accumulate(c, buf, tile)

    for c in cs:
        store_scores(c, 0, scores(c, own_base), masked=True)
        for j in both:
            m_ref[j, c] = jnp.full(m_ref.shape[2:], MASK_VALUE, F32)
            acc_ref[j, c] = jnp.zeros(acc_ref.shape[2:], F32)

    def advance(dst, src, tile):
        for c in cs:
            store_scores(c, dst, scores(c, (tile - 1) * sup))
        accumulate_all(src, tile - 1)

    def body(n, carry):
        advance(1, 0, 2 * n + 1)
        advance(0, 1, 2 * n + 2)
        return carry

    lax.fori_loop(0, lax.shift_right_logical(own_pair, 1), body, 0)

    @pl.when(lax.bitwise_and(own_pair, 1) == 1)
    def _():
        advance(1, 0, own_pair)
        accumulate_all(1, own_pair)

    @pl.when(lax.bitwise_and(own_pair, 1) == 0)
    def _():
        accumulate_all(0, own_pair)

    for c in cs:
        for j in both:
            acc = acc_ref[j, c]
            o = acc / pltpu.roll(acc, HEAD_DIM, 1)
            o_ref[j * blk:(j + 1) * blk, slab(c)] = jnp.where(first_head, o[:blk], o[blk:]).astype(o_ref.dtype)


def _moba_attention(qk, v, batch, seq):
    assert seq % (2 * MOBA_BLOCK) == 0
    nblk = seq // MOBA_BLOCK
    assert nblk <= LANES
    nrow = -(-nblk // BF16_SUBLANES) * BF16_SUBLANES
    slabs = N_HEADS * HEAD_DIM // LANES
    blk = MOBA_BLOCK
    block_id = jnp.arange(seq, dtype=jnp.int32) // blk
    onehot = (block_id[:, None] == jnp.arange(LANES, dtype=jnp.int32)[None, :]).astype(BF16)
    qi = jnp.arange(2 * blk, dtype=jnp.int32)[:, None] % blk
    kj = jnp.arange(2 * blk, dtype=jnp.int32)[None, :]
    own_mask = jnp.stack([jnp.where(kj - qi <= parity * blk, 0.0, MASK_VALUE) for parity in (0, 1)]).astype(F32)
    chains = MOBA_CHAINS
    assert slabs % chains == 0
    width = chains * LANES
    once = dict(pipeline_mode=pl.Buffered(1))
    return pl.pallas_call(
        functools.partial(_moba_kernel, nblk=nblk, chains=chains),
        grid=(batch, slabs // chains, nblk // 2),
        in_specs=[
            pl.BlockSpec((None, 2 * blk, width), lambda b, s, i: (b, i, s)),
            pl.BlockSpec((None, seq, width), lambda b, s, i: (b, 0, slabs // chains + s), **once),
            pl.BlockSpec((None, seq, width), lambda b, s, i: (b, 0, s), **once),
            pl.BlockSpec((seq, LANES), lambda b, s, i: (0, 0), **once),
            pl.BlockSpec((2, 2 * blk, 2 * blk), lambda b, s, i: (0, 0, 0), **once),
        ],
        out_specs=pl.BlockSpec((None, 2 * blk, width), lambda b, s, i: (b, i, s)),
        out_shape=jax.ShapeDtypeStruct((batch, seq, N_HEADS * HEAD_DIM), BF16),
        scratch_shapes=[
            pltpu.VMEM((chains, nrow, LANES), F32), pltpu.VMEM((chains, 2 * nrow, LANES), BF16),
            pltpu.VMEM((chains, HEADS_PER_SLAB, seq, LANES), BF16),
            pltpu.VMEM((2, chains, 2, 2 * blk, 2 * blk), F32),
            pltpu.VMEM((2, chains, 2 * blk, LANES), F32), pltpu.VMEM((2, chains, 2 * blk, LANES), F32),
        ],
        compiler_params=_params(("parallel", "parallel", "arbitrary")),
        name="moba_attention",
    )(qk, qk, v, onehot, own_mask)


def _ffn_kernel(x_ref, xh_ref, a_ref, ah_ref, wo_ref, g_ref, wu_ref, cw_ref, cb_ref, wd_ref, o_ref,
                *, tiles_per_seq, tf):
    i = pl.program_id(0)
    halo = CONV_HALO
    d_ff = wd_ref.shape[0]

    def rms(x):
        ms = jnp.mean(x * x, axis=-1, keepdims=True)
        return x * lax.rsqrt(ms + RMS_EPS) * g_ref[...]

    wo = wo_ref[...]
    x1 = x_ref[...] + jnp.dot(a_ref[...], wo, preferred_element_type=F32)
    xh1 = xh_ref[...] + jnp.dot(ah_ref[...], wo, preferred_element_type=F32)[ah_ref.shape[0] - halo:]

    keep = (i % tiles_per_seq != 0).astype(F32)
    hn = jnp.concatenate([(rms(xh1) * keep).astype(BF16), rms(x1).astype(BF16)], axis=0)

    def conv(col):
        u = jnp.dot(hn, wu_ref[:, col:col + tf], preferred_element_type=F32)
        cw = cw_ref[:, col:col + tf]
        return (cb_ref[:, col:col + tf]
                + cw[0:1, :] * pltpu.roll(u, 2, 0)[halo:]
                + cw[1:2, :] * pltpu.roll(u, 1, 0)[halo:]
                + cw[2:3, :] * u[halo:])

    acts = []
    for col in range(0, d_ff, tf):
        gate = conv(col)
        val = conv(d_ff + col)
        acts.append((gate * jax.nn.sigmoid(gate) * val).astype(BF16))
    o_ref[...] = x1 + jnp.dot(jnp.concatenate(acts, axis=1), wd_ref[...], preferred_element_type=F32)


def _attn_out_conv_ffn(x2, attn2, w_o, gain, w_up, conv_w, conv_b, w_down, seq, tm, tf):
    m, d = x2.shape
    da = attn2.shape[1]
    d_ff = w_down.shape[0]
    assert d_ff % tf == 0
    halo_blocks = tm // CONV_HALO
    attn_halo = BF16_SUBLANES
    attn_halo_blocks = tm // attn_halo
    resident = dict(pipeline_mode=pl.Buffered(1))
    return pl.pallas_call(
        functools.partial(_ffn_kernel, tiles_per_seq=seq // tm, tf=tf),
        grid=(m // tm,),
        in_specs=[
            pl.BlockSpec((tm, d), lambda i: (i, 0)),
            pl.BlockSpec((CONV_HALO, d), lambda i: (jnp.maximum(i * halo_blocks - 1, 0), 0)),
            pl.BlockSpec((tm, da), lambda i: (i, 0)),
            pl.BlockSpec((attn_halo, da), lambda i: (jnp.maximum(i * attn_halo_blocks - 1, 0), 0)),
            pl.BlockSpec((da, d), lambda i: (0, 0), **resident),
            pl.BlockSpec((1, d), lambda i: (0, 0)),
            pl.BlockSpec((d, 2 * d_ff), lambda i: (0, 0), **resident),
            pl.BlockSpec((CONV_WIDTH, 2 * d_ff), lambda i: (0, 0)),
            pl.BlockSpec((1, 2 * d_ff), lambda i: (0, 0)),
            pl.BlockSpec((d_ff, d), lambda i: (0, 0), **resident),
        ],
        out_specs=pl.BlockSpec((tm, d), lambda i: (i, 0)),
        out_shape=jax.ShapeDtypeStruct((m, d), F32),
        compiler_params=_params(("parallel",)),
        name="attn_out_conv_ffn",
    )(x2, x2, attn2, attn2, w_o, gain, w_up, conv_w, conv_b, w_down)


def _rope_tables(seq):
    pos = jnp.arange(seq, dtype=F32)
    inv_freq = ROPE_THETA ** (-jnp.arange(0, ROPE_DIM, 2, dtype=F32) / ROPE_DIM)
    ang = pos[:, None] * inv_freq[None, :]
    cos, sin = jnp.cos(ang), jnp.sin(ang)
    half = ROPE_DIM // 2
    pad = jnp.zeros((seq, HEAD_DIM - ROPE_DIM), F32)
    zeros = jnp.zeros((seq, half), F32)
    cos_h = jnp.concatenate([cos, cos, pad + 1.0], axis=1)
    sa_h = jnp.concatenate([zeros, sin, pad], axis=1)
    sb_h = jnp.concatenate([-sin, zeros, pad], axis=1)
    rep = LANES // HEAD_DIM
    return tuple(jnp.tile(t, (1, rep)) for t in (cos_h, sa_h, sb_h))


def _block_diag_ones():
    r = jnp.arange(MXU_WIDTH) // HEAD_DIM
    return (r[:, None] == r[None, :]).astype(BF16)


def kernel(x, attn_norm, a_w_qkv, a_q_norm, a_k_norm, a_w_o, b_w_qkv, b_q_norm, b_k_norm, b_w_o,
           ffn_norm, ffn_w_up, ffn_conv_w, ffn_conv_b, ffn_w_down):
    batch, seq, d_model = x.shape
    depth = attn_norm.shape[0]
    hd_all = N_HEADS * HEAD_DIM
    rope = _rope_tables(seq)
    bd = _block_diag_ones()
    x2 = x.reshape(batch * seq, d_model)
    q_scale = ATTN_SCALE * LOG2_E

    for layer in range(depth):
        j = layer // 2
        gain = attn_norm[layer][None, :]
        dilated = layer % 2 == 0
        if dilated:
            n_groups = N_DIL
            qg = jnp.tile(a_q_norm[j][:, None, :], (1, N_HEADS, 1)).reshape(-1) * q_scale
            kg = jnp.tile(a_k_norm[j][:, None, :], (1, N_HEADS, 1)).reshape(-1)
            w_qkv, w_o = a_w_qkv[j], a_w_o[j]
        else:
            n_groups = 1
            qg = jnp.tile(b_q_norm[j], N_HEADS) * q_scale
            kg = jnp.tile(b_k_norm[j], N_HEADS)
            w_qkv, w_o = b_w_qkv[j], b_w_o[j]
        n_qk = 2 * n_groups * hd_all
        colgain = jnp.concatenate([qg, kg])[None, :]
        w_qkv = w_qkv.astype(BF16)
        act_dtype = F32 if dilated else BF16
        qk = _norm_qk_proj(x2, gain, w_qkv, colgain, rope, bd, seq, act_dtype, PROJ_ROWS, PROJ_COLS)
        v = _norm_v_proj(x2, gain, w_qkv, n_qk // 2, act_dtype, PROJ_ROWS, PROJ_COLS)
        qk, v = qk.reshape(batch, seq, -1), v.reshape(batch, seq, -1)
        attn = _dilated_attention(qk, v, batch, seq) if dilated else _moba_attention(qk, v, batch, seq)
        x2 = _attn_out_conv_ffn(x2, attn.reshape(batch * seq, hd_all), w_o.astype(BF16),
                                ffn_norm[layer][None, :], ffn_w_up[layer].astype(BF16),
                                ffn_conv_w[layer], ffn_conv_b[layer][None, :],
                                ffn_w_down[layer].astype(BF16), seq, FFN_ROWS, FFN_CHUNK)
    return x2.reshape(batch, seq, d_model)
```

```python
import functools
import math

import jax
import jax.numpy as jnp
from jax import lax
from jax.experimental import pallas as pl
from jax.experimental.pallas import tpu as pltpu

N_HEADS = 16
HEAD_DIM = 64
ROPE_DIM = HEAD_DIM // 4
ROPE_THETA = 500000.0
ATTN_SCALE = HEAD_DIM ** -0.5
DILATED_PAIRS = ((128, 1), (512, 4), (2048, 16))
N_DIL = len(DILATED_PAIRS)
BAND_BLOCK = 128
MOBA_BLOCK = 256
MOBA_TOPK = 3
CONV_WIDTH = 3
RMS_EPS = 1e-6

LANES = 128
BF16_SUBLANES = 16
MXU_WIDTH = 256
HEADS_PER_SLAB = LANES // HEAD_DIM
MASK_VALUE = -1e30
MOBA_CHAINS = 4
PROJ_ROWS = 512
FUSED_PROJ_ROWS = 1024
PROJ_COLS = 512
FFN_ROWS = 1024
FFN_CHUNK = 256
CONV_HALO = 8
VMEM_LIMIT = 56 * 1024 * 1024
LOG2_E = math.log2(math.e)

F32 = jnp.float32
BF16 = jnp.bfloat16
NT_DIMS = (((1,), (1,)), ((), ()))


def _params(semantics):
    return pltpu.CompilerParams(dimension_semantics=semantics, vmem_limit_bytes=VMEM_LIMIT)


def _rmsnorm_bf16(x_ref, g_ref):
    x = x_ref[...]
    ms = jnp.mean(x * x, axis=-1, keepdims=True)
    return (x * lax.rsqrt(ms + RMS_EPS) * g_ref[...]).astype(BF16)


def _qk_proj_kernel(x_ref, g_ref, w_ref, cg_ref, cos_ref, sa_ref, sb_ref, bd_ref, o_ref, *maybe_v_ref, tn):
    hn = _rmsnorm_bf16(x_ref, g_ref)
    bd = bd_ref[...]
    half = ROPE_DIM // 2
    n_qk = o_ref.shape[1]
    for v_ref in maybe_v_ref:
        for lo in range(n_qk, w_ref.shape[1], tn):
            v_ref[:, lo - n_qk:lo - n_qk + tn] = jnp.dot(hn, w_ref[:, lo:lo + tn],
                                                         preferred_element_type=F32).astype(v_ref.dtype)
    for lo in range(0, n_qk, tn):
        acc = jnp.dot(hn, w_ref[:, lo:lo + tn], preferred_element_type=F32)
        for c in range(tn // MXU_WIDTH):
            wide = acc[:, c * MXU_WIDTH:(c + 1) * MXU_WIDTH]
            ss_wide = jnp.dot((wide * wide).astype(BF16), bd, preferred_element_type=F32)
            for h in range(MXU_WIDTH // LANES):
                sl = slice(lo + c * MXU_WIDTH + h * LANES, lo + c * MXU_WIDTH + (h + 1) * LANES)
                a = wide[:, h * LANES:(h + 1) * LANES]
                ss = ss_wide[:, h * LANES:(h + 1) * LANES]
                y = a * lax.rsqrt(ss * (1.0 / HEAD_DIM) + RMS_EPS) * cg_ref[:, sl]
                y = (y * cos_ref[...]
                     + pltpu.roll(y, half, 1) * sa_ref[...]
                     + pltpu.roll(y, LANES - half, 1) * sb_ref[...])
                o_ref[:, sl] = y.astype(o_ref.dtype)


def _v_proj_kernel(x_ref, g_ref, w_ref, o_ref, *, tn):
    hn = _rmsnorm_bf16(x_ref, g_ref)
    for lo in range(0, w_ref.shape[1], tn):
        o_ref[:, lo:lo + tn] = jnp.dot(hn, w_ref[:, lo:lo + tn], preferred_element_type=F32).astype(o_ref.dtype)


def _norm_qk_proj(x2, gain, w, colgain, rope, bd, seq, out_dtype, tm, tn, with_v):
    m, d = x2.shape
    n = colgain.shape[1]
    n_w = w.shape[1] if with_v else n
    cos_t, sa_t, sb_t = rope
    tiles_per_seq = seq // tm
    rope_spec = pl.BlockSpec((tm, LANES), lambda i: (i % tiles_per_seq, 0))
    out_cols = [n, n_w - n] if with_v else [n]
    outs = pl.pallas_call(
        functools.partial(_qk_proj_kernel, tn=tn),
        grid=(m // tm,),
        in_specs=[
            pl.BlockSpec((tm, d), lambda i: (i, 0)),
            pl.BlockSpec((1, d), lambda i: (0, 0)),
            pl.BlockSpec((d, n_w), lambda i: (0, 0), pipeline_mode=pl.Buffered(1)),
            pl.BlockSpec((1, n), lambda i: (0, 0)),
            rope_spec, rope_spec, rope_spec,
            pl.BlockSpec((MXU_WIDTH, MXU_WIDTH), lambda i: (0, 0)),
        ],
        out_specs=[pl.BlockSpec((tm, c), lambda i: (i, 0)) for c in out_cols],
        out_shape=[jax.ShapeDtypeStruct((m, c), out_dtype) for c in out_cols],
        compiler_params=_params(("parallel",)),
        name="norm_qk_proj",
    )(x2, gain, w, colgain, cos_t, sa_t, sb_t, bd)
    return outs if with_v else outs[0]


def _norm_v_proj(x2, gain, w, n, out_dtype, tm, tn):
    m, d = x2.shape
    last = w.shape[1] // n - 1
    return pl.pallas_call(
        functools.partial(_v_proj_kernel, tn=tn),
        grid=(m // tm,),
        in_specs=[
            pl.BlockSpec((tm, d), lambda i: (i, 0)),
            pl.BlockSpec((1, d), lambda i: (0, 0)),
            pl.BlockSpec((d, n), lambda i: (0, last), pipeline_mode=pl.Buffered(1)),
        ],
        out_specs=pl.BlockSpec((tm, n), lambda i: (i, 0)),
        out_shape=jax.ShapeDtypeStruct((m, n), out_dtype),
        compiler_params=_params(("parallel",)),
        name="norm_v_proj",
    )(x2, gain, w)


def _dilated_kernel(q0, q1, q2, k0, k1, k2, v0, v1, v2, o_ref, num_ref, m_ref, den_ref, mask_ref, *, tq):
    t = pl.program_id(2)
    q_refs, k_refs, v_refs = (q0, q1, q2), (k0, k1, k2), (v0, v1, v2)
    blk = BAND_BLOCK

    lane = lax.broadcasted_iota(jnp.int32, (blk, LANES), 1)
    first_head = lane < HEAD_DIM
    ones_rhs = jnp.ones((2 * blk, LANES), BF16)

    reaches = sorted({w // d for w, d in DILATED_PAIRS})
    qi = lax.broadcasted_iota(jnp.int32, (2 * blk, 2 * blk), 0) % blk
    kj = lax.broadcasted_iota(jnp.int32, (2 * blk, 2 * blk), 1)
    dist = qi + blk - kj
    for n, reach in enumerate(reaches):
        band = (dist >= 0) & (dist <= reach)
        mask_ref[n, 0] = jnp.where(band, 0.0, MASK_VALUE)
        mask_ref[n, 1] = jnp.where(band & (kj >= blk), 0.0, MASK_VALUE)

    for g, (window, dil) in enumerate(DILATED_PAIRS):
        reach = window // dil
        span = blk * dil
        assert dil & (dil - 1) == 0 and reach <= blk
        qg, kg, vg = q_refs[g], k_refs[g], v_refs[g]
        mask_g = reaches.index(reach)

        def load_kv(base):
            if dil == 1:
                base = pl.multiple_of(base, blk)
            rows = pl.ds(base, blk, stride=dil)
            return kg[rows, :].astype(BF16), vg[rows, :].astype(BF16)

        for r in range(dil):
            prev = None
            for u in range(tq // span):
                qbase = u * span + r
                kbase = t * tq + qbase
                kc, vc = load_kv(kbase)
                if prev is None:
                    pbase = kbase - span
                    no_prev = (pbase < 0).astype(jnp.int32)
                    kp, vp = load_kv(jnp.maximum(pbase, 0))
                else:
                    no_prev = 0
                    kp, vp = prev
                prev = (kc, vc)
                q = qg[pl.ds(qbase, blk, stride=dil), :]
                zero = jnp.zeros_like(q)
                q2 = jnp.concatenate([jnp.where(first_head, q, zero),
                                      jnp.where(first_head, zero, q)], axis=0).astype(BF16)
                kcat = jnp.concatenate([kp, kc], axis=0)
                vcat = jnp.concatenate([jnp.concatenate([vp, vc], axis=0), ones_rhs], axis=1)
                s = lax.dot_general(q2, kcat, NT_DIMS, preferred_element_type=F32)
                s = s + mask_ref[mask_g, no_prev]
                m = jnp.max(s, axis=-1, keepdims=True)
                p = jnp.exp2(s - m).astype(BF16)
                ov = jnp.dot(p, vcat, preferred_element_type=F32)
                mb = jnp.broadcast_to(m, (2 * blk, LANES))
                rows = pl.ds(qbase, blk, stride=dil)
                num_ref[g, rows, :] = jnp.where(first_head, ov[:blk, :LANES], ov[blk:, :LANES])
                den_ref[g, rows, :] = jnp.where(first_head, ov[:blk, LANES:], ov[blk:, LANES:])
                m_ref[g, rows, :] = jnp.where(first_head, mb[:blk], mb[blk:])

    rows = 2 * blk

    def merge(c, carry):
        sl = pl.ds(pl.multiple_of(c * rows, rows), rows)
        ms = [m_ref[g, sl, :] for g in range(N_DIL)]
        mx = functools.reduce(jnp.maximum, ms)
        ws = [jnp.exp2(mg - mx) for mg in ms]
        num = sum(w * num_ref[g, sl, :] for g, w in enumerate(ws))
        den = sum(w * den_ref[g, sl, :] for g, w in enumerate(ws))
        o_ref[sl, :] = (num / den).astype(o_ref.dtype)
        return carry

    lax.fori_loop(0, tq // rows, merge, 0)


def _dilated_attention(qk, v, batch, seq):
    span_max = BAND_BLOCK * max(d for _, d in DILATED_PAIRS)
    tq = span_max
    assert seq % tq == 0
    slabs = N_HEADS // HEADS_PER_SLAB
    n_reach = len({w // d for w, d in DILATED_PAIRS})

    q_specs = [pl.BlockSpec((None, tq, LANES), lambda b, s, t, g=g: (b, t, g * slabs + s)) for g in range(N_DIL)]
    k_specs = [pl.BlockSpec((None, seq, LANES), lambda b, s, t, g=g: (b, 0, (N_DIL + g) * slabs + s))
               for g in range(N_DIL)]
    v_specs = [pl.BlockSpec((None, seq, LANES), lambda b, s, t, g=g: (b, 0, g * slabs + s)) for g in range(N_DIL)]
    stat = pltpu.VMEM((N_DIL, tq, LANES), F32)
    return pl.pallas_call(
        functools.partial(_dilated_kernel, tq=tq),
        grid=(batch, slabs, seq // tq),
        in_specs=q_specs + k_specs + v_specs,
        out_specs=pl.BlockSpec((None, tq, LANES), lambda b, s, t: (b, t, s)),
        out_shape=jax.ShapeDtypeStruct((batch, seq, N_HEADS * HEAD_DIM), BF16),
        scratch_shapes=[stat, stat, stat, pltpu.VMEM((n_reach, 2, 2 * BAND_BLOCK, 2 * BAND_BLOCK), F32)],
        compiler_params=_params(("parallel", "parallel", "arbitrary")),
        name="dilated_attention",
    )(*([qk] * 6 + [v] * 3))


def _moba_kernel(q_ref, k_ref, v_ref, oh_ref, mask_ref, o_ref, km_ref, kmhl_ref, vaug_ref, s_ref, m_ref, acc_ref,
                 *, nblk, chains):
    own_pair = pl.program_id(2)
    blk = MOBA_BLOCK
    both = range(2)
    nrow = km_ref.shape[1]
    sup = 2 * blk
    lane = lax.broadcasted_iota(jnp.int32, (blk, LANES), 1)
    first_head = lane < HEAD_DIM
    cs = range(chains)

    def slab(c):
        return slice(c * LANES, (c + 1) * LANES)

    @pl.when(own_pair == 0)
    def _():
        for c in cs:
            km_ref[c] = jnp.zeros(km_ref.shape[1:], F32)
            for j in range(nblk):
                rows = slice(j * blk, (j + 1) * blk)
                km_ref[c, j:j + 1, :] = jnp.sum(k_ref[rows, slab(c)].astype(F32), axis=0,
                                                keepdims=True) * (1.0 / blk)
                vj = v_ref[rows, slab(c)].astype(F32)
                vaug_ref[c, 0, rows, :] = jnp.where(first_head, vj, 1.0).astype(BF16)
                vaug_ref[c, 1, rows, :] = jnp.where(first_head, 1.0, vj).astype(BF16)
            km = km_ref[c]
            hi = km.astype(BF16)
            kmhl_ref[c] = jnp.concatenate([hi, (km - hi.astype(F32)).astype(BF16)], axis=0)

    row = lax.broadcasted_iota(jnp.int32, (nrow, 2 * blk), 0)
    rowf = row.astype(F32)

    def augmented_queries(j, c):
        i = 2 * own_pair + j
        q = q_ref[j * blk:(j + 1) * blk, slab(c)]
        zero = jnp.zeros_like(q)
        q2 = jnp.concatenate([jnp.where(first_head, q, zero), jnp.where(first_head, zero, q)], axis=0)
        gate2 = lax.dot_general(kmhl_ref[c], q2, NT_DIMS, preferred_element_type=F32)
        gate = gate2[:nrow] + gate2[nrow:]
        remaining = row < i
        sel = row >= i
        for _ in range(MOBA_TOPK):
            gm = jnp.max(jnp.where(remaining, gate, -jnp.inf), axis=0, keepdims=True)
            cand = remaining & (gate == gm)
            first = jnp.min(jnp.where(cand, rowf, float(LANES)), axis=0, keepdims=True)
            pick = rowf == first
            sel = sel | pick
            remaining = remaining & jnp.logical_not(pick)
        bias_t = jnp.concatenate([jnp.where(sel, 0.0, MASK_VALUE),
                                  jnp.zeros((LANES - nrow, 2 * blk), F32)], axis=0)
        return jnp.concatenate([q2, bias_t.T.astype(BF16)], axis=1)

    q_aug = [jnp.concatenate([augmented_queries(j, c) for j in both], axis=0) for c in cs]

    own_base = own_pair * sup

    def scores(c, base):
        rows = pl.ds(pl.multiple_of(base, sup), sup)
        k_aug = jnp.concatenate([k_ref[rows, slab(c)], oh_ref[rows, :]], axis=1)
        return lax.dot_general(q_aug[c], k_aug, NT_DIMS, preferred_element_type=F32)

    def store_scores(c, buf, s, masked=False):
        for j in both:
            sj = s[j * sup:(j + 1) * sup]
            s_ref[j, c, buf] = sj + mask_ref[j] if masked else sj

    def accumulate(c, buf, tile):
        base = jnp.where(tile == 0, own_base, (tile - 1) * sup)
        rows = pl.ds(pl.multiple_of(base, sup), sup)
        probs, alphas = [], []
        for j in both:
            s = s_ref[j, c, buf]
            m_old = m_ref[j, c]
            m_new = jnp.maximum(m_old, jnp.max(s, axis=-1, keepdims=True))
            alphas.append(jnp.exp2(m_old - m_new))
            probs.append(jnp.exp2(s - jnp.concatenate([m_new] * (sup // LANES), axis=1)).astype(BF16))
            m_ref[j, c] = m_new
        pv = [jnp.dot(jnp.concatenate([p[h * blk:(h + 1) * blk] for p in probs], axis=0),
                      vaug_ref[c, h, rows, :], preferred_element_type=F32) for h in range(HEADS_PER_SLAB)]
        for j in both:
            pv_j = jnp.concatenate([pv_h[j * blk:(j + 1) * blk] for pv_h in pv], axis=0)
            acc_ref[j, c] = acc_ref[j, c] * alphas[j] + pv_j

    def accumulate_all(buf, tile):
        for c in cs:
            accumulate(c, buf, tile)

    for c in cs:
        store_scores(c, 0, scores(c, own_base), masked=True)
        for j in both:
            m_ref[j, c] = jnp.full(m_ref.shape[2:], MASK_VALUE, F32)
            acc_ref[j, c] = jnp.zeros(acc_ref.shape[2:], F32)

    def advance(dst, src, tile):
        for c in cs:
            store_scores(c, dst, scores(c, (tile - 1) * sup))
        accumulate_all(src, tile - 1)

    def body(n, carry):
        advance(1, 0, 2 * n + 1)
        advance(0, 1, 2 * n + 2)
        return carry

    lax.fori_loop(0, lax.shift_right_logical(own_pair, 1), body, 0)

    @pl.when(lax.bitwise_and(own_pair, 1) == 1)
    def _():
        advance(1, 0, own_pair)
        accumulate_all(1, own_pair)

    @pl.when(lax.bitwise_and(own_pair, 1) == 0)
    def _():
        accumulate_all(0, own_pair)

    for c in cs:
        for j in both:
            acc = acc_ref[j, c]
            o = acc / pltpu.roll(acc, HEAD_DIM, 1)
            o_ref[j * blk:(j + 1) * blk, slab(c)] = jnp.where(first_head, o[:blk], o[blk:]).astype(o_ref.dtype)


def _moba_attention(qk, v, batch, seq):
    assert seq % (2 * MOBA_BLOCK) == 0
    nblk = seq // MOBA_BLOCK
    assert nblk <= LANES
    nrow = -(-nblk // BF16_SUBLANES) * BF16_SUBLANES
    slabs = N_HEADS * HEAD_DIM // LANES
    blk = MOBA_BLOCK
    block_id = jnp.arange(seq, dtype=jnp.int32) // blk
    onehot = (block_id[:, None] == jnp.arange(LANES, dtype=jnp.int32)[None, :]).astype(BF16)
    qi = jnp.arange(2 * blk, dtype=jnp.int32)[:, None] % blk
    kj = jnp.arange(2 * blk, dtype=jnp.int32)[None, :]
    own_mask = jnp.stack([jnp.where(kj - qi <= parity * blk, 0.0, MASK_VALUE) for parity in (0, 1)]).astype(F32)
    chains = MOBA_CHAINS
    assert slabs % chains == 0
    width = chains * LANES
    once = dict(pipeline_mode=pl.Buffered(1))
    return pl.pallas_call(
        functools.partial(_moba_kernel, nblk=nblk, chains=chains),
        grid=(batch, slabs // chains, nblk // 2),
        in_specs=[
            pl.BlockSpec((None, 2 * blk, width), lambda b, s, i: (b, i, s)),
            pl.BlockSpec((None, seq, width), lambda b, s, i: (b, 0, slabs // chains + s), **once),
            pl.BlockSpec((None, seq, width), lambda b, s, i: (b, 0, s), **once),
            pl.BlockSpec((seq, LANES), lambda b, s, i: (0, 0), **once),
            pl.BlockSpec((2, 2 * blk, 2 * blk), lambda b, s, i: (0, 0, 0), **once),
        ],
        out_specs=pl.BlockSpec((None, 2 * blk, width), lambda b, s, i: (b, i, s)),
        out_shape=jax.ShapeDtypeStruct((batch, seq, N_HEADS * HEAD_DIM), BF16),
        scratch_shapes=[
            pltpu.VMEM((chains, nrow, LANES), F32), pltpu.VMEM((chains, 2 * nrow, LANES), BF16),
            pltpu.VMEM((chains, HEADS_PER_SLAB, seq, LANES), BF16),
            pltpu.VMEM((2, chains, 2, 2 * blk, 2 * blk), F32),
            pltpu.VMEM((2, chains, 2 * blk, LANES), F32), pltpu.VMEM((2, chains, 2 * blk, LANES), F32),
        ],
        compiler_params=_params(("parallel", "parallel", "arbitrary")),
        name="moba_attention",
    )(qk, qk, v, onehot, own_mask)


def _ffn_kernel(x_ref, xh_ref, a_ref, ah_ref, wo_ref, g_ref, wu_ref, cw_ref, cb_ref, wd_ref, o_ref,
                *, tiles_per_seq, tf):
    i = pl.program_id(0)
    halo = CONV_HALO
    d_ff = wd_ref.shape[0]

    def rms(x):
        ms = jnp.mean(x * x, axis=-1, keepdims=True)
        return x * lax.rsqrt(ms + RMS_EPS) * g_ref[...]

    wo = wo_ref[...]
    x1 = x_ref[...] + jnp.dot(a_ref[...], wo, preferred_element_type=F32)
    xh1 = xh_ref[...] + jnp.dot(ah_ref[...], wo, preferred_element_type=F32)[ah_ref.shape[0] - halo:]

    keep = (i % tiles_per_seq != 0).astype(F32)
    hn = jnp.concatenate([(rms(xh1) * keep).astype(BF16), rms(x1).astype(BF16)], axis=0)

    def conv(col):
        u = jnp.dot(hn, wu_ref[:, col:col + tf], preferred_element_type=F32)
        cw = cw_ref[:, col:col + tf]
        return (cb_ref[:, col:col + tf]
                + cw[0:1, :] * pltpu.roll(u, 2, 0)[halo:]
                + cw[1:2, :] * pltpu.roll(u, 1, 0)[halo:]
                + cw[2:3, :] * u[halo:])

    acts = []
    for col in range(0, d_ff, tf):
        gate = conv(col)
        val = conv(d_ff + col)
        acts.append((gate * jax.nn.sigmoid(gate) * val).astype(BF16))
    o_ref[...] = x1 + jnp.dot(jnp.concatenate(acts, axis=1), wd_ref[...], preferred_element_type=F32)


def _attn_out_conv_ffn(x2, attn2, w_o, gain, w_up, conv_w, conv_b, w_down, seq, tm, tf):
    m, d = x2.shape
    da = attn2.shape[1]
    d_ff = w_down.shape[0]
    assert d_ff % tf == 0
    halo_blocks = tm // CONV_HALO
    attn_halo = BF16_SUBLANES
    attn_halo_blocks = tm // attn_halo
    resident = dict(pipeline_mode=pl.Buffered(1))
    return pl.pallas_call(
        functools.partial(_ffn_kernel, tiles_per_seq=seq // tm, tf=tf),
        grid=(m // tm,),
        in_specs=[
            pl.BlockSpec((tm, d), lambda i: (i, 0)),
            pl.BlockSpec((CONV_HALO, d), lambda i: (jnp.maximum(i * halo_blocks - 1, 0), 0)),
            pl.BlockSpec((tm, da), lambda i: (i, 0)),
            pl.BlockSpec((attn_halo, da), lambda i: (jnp.maximum(i * attn_halo_blocks - 1, 0), 0)),
            pl.BlockSpec((da, d), lambda i: (0, 0), **resident),
            pl.BlockSpec((1, d), lambda i: (0, 0)),
            pl.BlockSpec((d, 2 * d_ff), lambda i: (0, 0), **resident),
            pl.BlockSpec((CONV_WIDTH, 2 * d_ff), lambda i: (0, 0)),
            pl.BlockSpec((1, 2 * d_ff), lambda i: (0, 0)),
            pl.BlockSpec((d_ff, d), lambda i: (0, 0), **resident),
        ],
        out_specs=pl.BlockSpec((tm, d), lambda i: (i, 0)),
        out_shape=jax.ShapeDtypeStruct((m, d), F32),
        compiler_params=_params(("parallel",)),
        name="attn_out_conv_ffn",
    )(x2, x2, attn2, attn2, w_o, gain, w_up, conv_w, conv_b, w_down)


def _rope_tables(seq):
    pos = jnp.arange(seq, dtype=F32)
    inv_freq = ROPE_THETA ** (-jnp.arange(0, ROPE_DIM, 2, dtype=F32) / ROPE_DIM)
    ang = pos[:, None] * inv_freq[None, :]
    cos, sin = jnp.cos(ang), jnp.sin(ang)
    half = ROPE_DIM // 2
    pad = jnp.zeros((seq, HEAD_DIM - ROPE_DIM), F32)
    zeros = jnp.zeros((seq, half), F32)
    cos_h = jnp.concatenate([cos, cos, pad + 1.0], axis=1)
    sa_h = jnp.concatenate([zeros, sin, pad], axis=1)
    sb_h = jnp.concatenate([-sin, zeros, pad], axis=1)
    rep = LANES // HEAD_DIM
    return tuple(jnp.tile(t, (1, rep)) for t in (cos_h, sa_h, sb_h))


def _block_diag_ones():
    r = jnp.arange(MXU_WIDTH) // HEAD_DIM
    return (r[:, None] == r[None, :]).astype(BF16)


def kernel(x, attn_norm, a_w_qkv, a_q_norm, a_k_norm, a_w_o, b_w_qkv, b_q_norm, b_k_norm, b_w_o,
           ffn_norm, ffn_w_up, ffn_conv_w, ffn_conv_b, ffn_w_down):
    batch, seq, d_model = x.shape
    depth = attn_norm.shape[0]
    hd_all = N_HEADS * HEAD_DIM
    rope = _rope_tables(seq)
    bd = _block_diag_ones()
    x2 = x.reshape(batch * seq, d_model)
    q_scale = ATTN_SCALE * LOG2_E

    for layer in range(depth):
        j = layer // 2
        gain = attn_norm[layer][None, :]
        dilated = layer % 2 == 0
        if dilated:
            n_groups = N_DIL
            qg = jnp.tile(a_q_norm[j][:, None, :], (1, N_HEADS, 1)).reshape(-1) * q_scale
            kg = jnp.tile(a_k_norm[j][:, None, :], (1, N_HEADS, 1)).reshape(-1)
            w_qkv, w_o = a_w_qkv[j], a_w_o[j]
        else:
            n_groups = 1
            qg = jnp.tile(b_q_norm[j], N_HEADS) * q_scale
            kg = jnp.tile(b_k_norm[j], N_HEADS)
            w_qkv, w_o = b_w_qkv[j], b_w_o[j]
        n_qk = 2 * n_groups * hd_all
        colgain = jnp.concatenate([qg, kg])[None, :]
        w_qkv = w_qkv.astype(BF16)
        act_dtype = F32 if dilated else BF16
        if dilated:
            qk = _norm_qk_proj(x2, gain, w_qkv, colgain, rope, bd, seq, act_dtype, PROJ_ROWS, PROJ_COLS, False)
            v = _norm_v_proj(x2, gain, w_qkv, n_qk // 2, act_dtype, PROJ_ROWS, PROJ_COLS)
        else:
            qk, v = _norm_qk_proj(x2, gain, w_qkv, colgain, rope, bd, seq, act_dtype, FUSED_PROJ_ROWS, PROJ_COLS,
                                  True)
        qk, v = qk.reshape(batch, seq, -1), v.reshape(batch, seq, -1)
        attn = _dilated_attention(qk, v, batch, seq) if dilated else _moba_attention(qk, v, batch, seq)
        x2 = _attn_out_conv_ffn(x2, attn.reshape(batch * seq, hd_all), w_o.astype(BF16),
                                ffn_norm[layer][None, :], ffn_w_up[layer].astype(BF16),
                                ffn_conv_w[layer], ffn_conv_b[layer][None, :],
                                ffn_w_down[layer].astype(BF16), seq, FFN_ROWS, FFN_CHUNK)
    return x2.reshape(batch, seq, d_model)
```

```python
import functools
import math

import jax
import jax.numpy as jnp
from jax import lax
from jax.experimental import pallas as pl
from jax.experimental.pallas import tpu as pltpu

N_HEADS = 16
HEAD_DIM = 64
ROPE_DIM = HEAD_DIM // 4
ROPE_THETA = 500000.0
ATTN_SCALE = HEAD_DIM ** -0.5
DILATED_PAIRS = ((128, 1), (512, 4), (2048, 16))
N_DIL = len(DILATED_PAIRS)
BAND_BLOCK = 128
MOBA_BLOCK = 256
MOBA_TOPK = 3
CONV_WIDTH = 3
RMS_EPS = 1e-6

LANES = 128
BF16_SUBLANES = 16
MXU_WIDTH = 256
HEADS_PER_SLAB = LANES // HEAD_DIM
MASK_VALUE = -1e30
MOBA_CHAINS = 4
PROJ_ROWS = 512
FUSED_PROJ_ROWS = 1024
PROJ_COLS = 512
FFN_ROWS = 1024
FFN_CHUNK = 256
CONV_HALO = 8
VMEM_LIMIT = 56 * 1024 * 1024
LOG2_E = math.log2(math.e)

F32 = jnp.float32
BF16 = jnp.bfloat16
NT_DIMS = (((1,), (1,)), ((), ()))


def _params(semantics):
    return pltpu.CompilerParams(dimension_semantics=semantics, vmem_limit_bytes=VMEM_LIMIT)


def _rmsnorm_bf16(x_ref, g_ref):
    x = x_ref[...]
    ms = jnp.mean(x * x, axis=-1, keepdims=True)
    return (x * lax.rsqrt(ms + RMS_EPS) * g_ref[...]).astype(BF16)


def _qk_proj_kernel(x_ref, g_ref, w_ref, cg_ref, cos_ref, sa_ref, sb_ref, bd_ref, o_ref, *maybe_v_ref, tn):
    hn = _rmsnorm_bf16(x_ref, g_ref)
    bd = bd_ref[...]
    half = ROPE_DIM // 2
    n_qk = o_ref.shape[1]
    for v_ref in maybe_v_ref:
        for lo in range(n_qk, w_ref.shape[1], tn):
            v_ref[:, lo - n_qk:lo - n_qk + tn] = jnp.dot(hn, w_ref[:, lo:lo + tn],
                                                         preferred_element_type=F32).astype(v_ref.dtype)
    for lo in range(0, n_qk, tn):
        acc = jnp.dot(hn, w_ref[:, lo:lo + tn], preferred_element_type=F32)
        for c in range(tn // MXU_WIDTH):
            wide = acc[:, c * MXU_WIDTH:(c + 1) * MXU_WIDTH]
            ss_wide = jnp.dot((wide * wide).astype(BF16), bd, preferred_element_type=F32)
            for h in range(MXU_WIDTH // LANES):
                sl = slice(lo + c * MXU_WIDTH + h * LANES, lo + c * MXU_WIDTH + (h + 1) * LANES)
                a = wide[:, h * LANES:(h + 1) * LANES]
                ss = ss_wide[:, h * LANES:(h + 1) * LANES]
                y = a * lax.rsqrt(ss * (1.0 / HEAD_DIM) + RMS_EPS) * cg_ref[:, sl]
                y = (y * cos_ref[...]
                     + pltpu.roll(y, half, 1) * sa_ref[...]
                     + pltpu.roll(y, LANES - half, 1) * sb_ref[...])
                o_ref[:, sl] = y.astype(o_ref.dtype)


def _v_proj_kernel(x_ref, g_ref, w_ref, o_ref, *, tn):
    hn = _rmsnorm_bf16(x_ref, g_ref)
    for lo in range(0, w_ref.shape[1], tn):
        o_ref[:, lo:lo + tn] = jnp.dot(hn, w_ref[:, lo:lo + tn], preferred_element_type=F32).astype(o_ref.dtype)


def _norm_qk_proj(x2, gain, w, colgain, rope, bd, seq, out_dtype, tm, tn, with_v):
    m, d = x2.shape
    n = colgain.shape[1]
    n_w = w.shape[1] if with_v else n
    cos_t, sa_t, sb_t = rope
    tiles_per_seq = seq // tm
    rope_spec = pl.BlockSpec((tm, LANES), lambda i: (i % tiles_per_seq, 0))
    out_cols = [n, n_w - n] if with_v else [n]
    outs = pl.pallas_call(
        functools.partial(_qk_proj_kernel, tn=tn),
        grid=(m // tm,),
        in_specs=[
            pl.BlockSpec((tm, d), lambda i: (i, 0)),
            pl.BlockSpec((1, d), lambda i: (0, 0)),
            pl.BlockSpec((d, n_w), lambda i: (0, 0), pipeline_mode=pl.Buffered(1)),
            pl.BlockSpec((1, n), lambda i: (0, 0)),
            rope_spec, rope_spec, rope_spec,
            pl.BlockSpec((MXU_WIDTH, MXU_WIDTH), lambda i: (0, 0)),
        ],
        out_specs=[pl.BlockSpec((tm, c), lambda i: (i, 0)) for c in out_cols],
        out_shape=[jax.ShapeDtypeStruct((m, c), out_dtype) for c in out_cols],
        compiler_params=_params(("parallel",)),
        name="norm_qk_proj",
    )(x2, gain, w, colgain, cos_t, sa_t, sb_t, bd)
    return outs if with_v else outs[0]


def _norm_v_proj(x2, gain, w, n, out_dtype, tm, tn):
    m, d = x2.shape
    last = w.shape[1] // n - 1
    return pl.pallas_call(
        functools.partial(_v_proj_kernel, tn=tn),
        grid=(m // tm,),
        in_specs=[
            pl.BlockSpec((tm, d), lambda i: (i, 0)),
            pl.BlockSpec((1, d), lambda i: (0, 0)),
            pl.BlockSpec((d, n), lambda i: (0, last), pipeline_mode=pl.Buffered(1)),
        ],
        out_specs=pl.BlockSpec((tm, n), lambda i: (i, 0)),
        out_shape=jax.ShapeDtypeStruct((m, n), out_dtype),
        compiler_params=_params(("parallel",)),
        name="norm_v_proj",
    )(x2, gain, w)


def _dilated_kernel(q0, q1, q2, kl0, kl1, kl2, kh0, kh1, kh2, vl0, vl1, vl2, vh0, vh1, vh2,
                    o_ref, num_ref, m_ref, den_ref, mask_ref, *, tq):
    t = pl.program_id(2)
    q_refs = (q0, q1, q2)
    k_halves = ((kl0, kl1, kl2), (kh0, kh1, kh2))
    v_halves = ((vl0, vl1, vl2), (vh0, vh1, vh2))
    blk = BAND_BLOCK

    lane = lax.broadcasted_iota(jnp.int32, (blk, LANES), 1)
    first_head = lane < HEAD_DIM
    ones_rhs = jnp.ones((2 * blk, LANES), BF16)

    reaches = sorted({w // d for w, d in DILATED_PAIRS})
    qi = lax.broadcasted_iota(jnp.int32, (2 * blk, 2 * blk), 0) % blk
    kj = lax.broadcasted_iota(jnp.int32, (2 * blk, 2 * blk), 1)
    dist = qi + blk - kj
    for n, reach in enumerate(reaches):
        band = (dist >= 0) & (dist <= reach)
        mask_ref[n, 0] = jnp.where(band, 0.0, MASK_VALUE)
        mask_ref[n, 1] = jnp.where(band & (kj >= blk), 0.0, MASK_VALUE)

    def tile(half):
      for g, (window, dil) in enumerate(DILATED_PAIRS):
        reach = window // dil
        span = blk * dil
        assert dil & (dil - 1) == 0 and reach <= blk
        qg = q_refs[g]
        mask_g = reaches.index(reach)

        def load_kv(which, base):
            rows = pl.ds(base, blk, stride=dil)
            return k_halves[which][g][rows, :].astype(BF16), v_halves[which][g][rows, :].astype(BF16)

        for r in range(dil):
            prev = None
            for u in range(tq // span):
                qbase = u * span + r
                kc, vc = load_kv(half, qbase)
                if prev is not None:
                    no_prev = 0
                    kp, vp = prev
                elif half == 0:
                    no_prev = 1
                    kp, vp = kc, vc
                else:
                    no_prev = 0
                    kp, vp = load_kv(0, tq - span + r)
                prev = (kc, vc)
                q = qg[pl.ds(qbase, blk, stride=dil), :]
                zero = jnp.zeros_like(q)
                q2 = jnp.concatenate([jnp.where(first_head, q, zero),
                                      jnp.where(first_head, zero, q)], axis=0).astype(BF16)
                kcat = jnp.concatenate([kp, kc], axis=0)
                vcat = jnp.concatenate([jnp.concatenate([vp, vc], axis=0), ones_rhs], axis=1)
                s = lax.dot_general(q2, kcat, NT_DIMS, preferred_element_type=F32)
                s = s + mask_ref[mask_g, no_prev]
                m = jnp.max(s, axis=-1, keepdims=True)
                p = jnp.exp2(s - m).astype(BF16)
                ov = jnp.dot(p, vcat, preferred_element_type=F32)
                mb = jnp.broadcast_to(m, (2 * blk, LANES))
                rows = pl.ds(qbase, blk, stride=dil)
                num_ref[g, rows, :] = jnp.where(first_head, ov[:blk, :LANES], ov[blk:, :LANES])
                den_ref[g, rows, :] = jnp.where(first_head, ov[:blk, LANES:], ov[blk:, LANES:])
                m_ref[g, rows, :] = jnp.where(first_head, mb[:blk], mb[blk:])

    for half in range(2):
        pl.when(t == half)(functools.partial(tile, half))

    rows = 2 * blk

    def merge(c, carry):
        sl = pl.ds(pl.multiple_of(c * rows, rows), rows)
        ms = [m_ref[g, sl, :] for g in range(N_DIL)]
        mx = functools.reduce(jnp.maximum, ms)
        ws = [jnp.exp2(mg - mx) for mg in ms]
        num = sum(w * num_ref[g, sl, :] for g, w in enumerate(ws))
        den = sum(w * den_ref[g, sl, :] for g, w in enumerate(ws))
        o_ref[sl, :] = (num / den).astype(o_ref.dtype)
        return carry

    lax.fori_loop(0, tq // rows, merge, 0)


def _dilated_attention(qk, v, batch, seq):
    span_max = BAND_BLOCK * max(d for _, d in DILATED_PAIRS)
    tq = span_max
    assert seq == 2 * tq
    slabs = N_HEADS // HEADS_PER_SLAB
    n_reach = len({w // d for w, d in DILATED_PAIRS})

    def low(col):
        return lambda b, s, t: (b, 0, col + s)

    def high(col):
        def index(b, s, t):
            s_prev = jnp.where(s > 0, s - 1, slabs - 1)
            b_prev = jnp.where(s > 0, b, jnp.maximum(b - 1, 0))
            return (jnp.where(t == 1, b, b_prev), 1, col + jnp.where(t == 1, s, s_prev))
        return index

    half_block = (None, tq, LANES)
    q_specs = [pl.BlockSpec(half_block, lambda b, s, t, g=g: (b, t, g * slabs + s)) for g in range(N_DIL)]
    kv_specs = [pl.BlockSpec(half_block, part((first + g) * slabs))
                for first in (N_DIL, 0) for part in (low, high) for g in range(N_DIL)]
    stat = pltpu.VMEM((N_DIL, tq, LANES), F32)
    return pl.pallas_call(
        functools.partial(_dilated_kernel, tq=tq),
        grid=(batch, slabs, 2),
        in_specs=q_specs + kv_specs,
        out_specs=pl.BlockSpec((None, tq, LANES), lambda b, s, t: (b, t, s)),
        out_shape=jax.ShapeDtypeStruct((batch, seq, N_HEADS * HEAD_DIM), BF16),
        scratch_shapes=[stat, stat, stat, pltpu.VMEM((n_reach, 2, 2 * BAND_BLOCK, 2 * BAND_BLOCK), F32)],
        compiler_params=_params(("parallel", "parallel", "arbitrary")),
        name="dilated_attention",
    )(*([qk] * 9 + [v] * 6))


def _moba_kernel(q_ref, k_ref, v_ref, oh_ref, mask_ref, o_ref, km_ref, kmhl_ref, vaug_ref, s_ref, m_ref, acc_ref,
                 *, nblk, chains):
    own_pair = pl.program_id(2)
    blk = MOBA_BLOCK
    both = range(2)
    nrow = km_ref.shape[1]
    sup = 2 * blk
    lane = lax.broadcasted_iota(jnp.int32, (blk, LANES), 1)
    first_head = lane < HEAD_DIM
    cs = range(chains)

    def slab(c):
        return slice(c * LANES, (c + 1) * LANES)

    @pl.when(own_pair == 0)
    def _():
        for c in cs:
            km_ref[c] = jnp.zeros(km_ref.shape[1:], F32)
            for j in range(nblk):
                rows = slice(j * blk, (j + 1) * blk)
                km_ref[c, j:j + 1, :] = jnp.sum(k_ref[rows, slab(c)].astype(F32), axis=0,
                                                keepdims=True) * (1.0 / blk)
                vj = v_ref[rows, slab(c)].astype(F32)
                vaug_ref[c, 0, rows, :] = jnp.where(first_head, vj, 1.0).astype(BF16)
                vaug_ref[c, 1, rows, :] = jnp.where(first_head, 1.0, vj).astype(BF16)
            km = km_ref[c]
            hi = km.astype(BF16)
            kmhl_ref[c] = jnp.concatenate([hi, (km - hi.astype(F32)).astype(BF16)], axis=0)

    row = lax.broadcasted_iota(jnp.int32, (nrow, 2 * blk), 0)
    rowf = row.astype(F32)

    def augmented_queries(j, c):
        i = 2 * own_pair + j
        q = q_ref[j * blk:(j + 1) * blk, slab(c)]
        zero = jnp.zeros_like(q)
        q2 = jnp.concatenate([jnp.where(first_head, q, zero), jnp.where(first_head, zero, q)], axis=0)
        gate2 = lax.dot_general(kmhl_ref[c], q2, NT_DIMS, preferred_element_type=F32)
        gate = gate2[:nrow] + gate2[nrow:]
        remaining = row < i
        sel = row >= i
        for _ in range(MOBA_TOPK):
            gm = jnp.max(jnp.where(remaining, gate, -jnp.inf), axis=0, keepdims=True)
            cand = remaining & (gate == gm)
            first = jnp.min(jnp.where(cand, rowf, float(LANES)), axis=0, keepdims=True)
            pick = rowf == first
            sel = sel | pick
            remaining = remaining & jnp.logical_not(pick)
        bias_t = jnp.concatenate([jnp.where(sel, 0.0, MASK_VALUE),
                                  jnp.zeros((LANES - nrow, 2 * blk), F32)], axis=0)
        return jnp.concatenate([q2, bias_t.T.astype(BF16)], axis=1)

    q_aug = [jnp.concatenate([augmented_queries(j, c) for j in both], axis=0) for c in cs]

    own_base = own_pair * sup

    def scores(c, base):
        rows = pl.ds(pl.multiple_of(base, sup), sup)
        k_aug = jnp.concatenate([k_ref[rows, slab(c)], oh_ref[rows, :]], axis=1)
        return lax.dot_general(q_aug[c], k_aug, NT_DIMS, preferred_element_type=F32)

    def store_scores(c, buf, s, masked=False):
        for j in both:
            sj = s[j * sup:(j + 1) * sup]
            s_ref[j, c, buf] = sj + mask_ref[j] if masked else sj

    def accumulate(c, buf, tile):
        base = jnp.where(tile == 0, own_base, (tile - 1) * sup)
        rows = pl.ds(pl.multiple_of(base, sup), sup)
        probs, alphas = [], []
        for j in both:
            s = s_ref[j, c, buf]
            m_old = m_ref[j, c]
            m_new = jnp.maximum(m_old, jnp.max(s, axis=-1, keepdims=True))
            alphas.append(jnp.exp2(m_old - m_new))
            probs.append(jnp.exp2(s - jnp.concatenate([m_new] * (sup // LANES), axis=1)).astype(BF16))
            m_ref[j, c] = m_new
        pv = [jnp.dot(jnp.concatenate([p[h * blk:(h + 1) * blk] for p in probs], axis=0),
                      vaug_ref[c, h, rows, :], preferred_element_type=F32) for h in range(HEADS_PER_SLAB)]
        for j in both:
            pv_j = jnp.concatenate([pv_h[j * blk:(j + 1) * blk] for pv_h in pv], axis=0)
            acc_ref[j, c] = acc_ref[j, c] * alphas[j] + pv_j

    def accumulate_all(buf, tile):
        for c in cs:
            accumulate(c, buf, tile)

    for c in cs:
        store_scores(c, 0, scores(c, own_base), masked=True)
        for j in both:
            m_ref[j, c] = jnp.full(m_ref.shape[2:], MASK_VALUE, F32)
            acc_ref[j, c] = jnp.zeros(acc_ref.shape[2:], F32)

    def advance(dst, src, tile):
        for c in cs:
            store_scores(c, dst, scores(c, (tile - 1) * sup))
        accumulate_all(src, tile - 1)

    def body(n, carry):
        advance(1, 0, 2 * n + 1)
        advance(0, 1, 2 * n + 2)
        return carry

    lax.fori_loop(0, lax.shift_right_logical(own_pair, 1), body, 0)

    @pl.when(lax.bitwise_and(own_pair, 1) == 1)
    def _():
        advance(1, 0, own_pair)
        accumulate_all(1, own_pair)

    @pl.when(lax.bitwise_and(own_pair, 1) == 0)
    def _():
        accumulate_all(0, own_pair)

    for c in cs:
        for j in both:
            acc = acc_ref[j, c]
            o = acc / pltpu.roll(acc, HEAD_DIM, 1)
            o_ref[j * blk:(j + 1) * blk, slab(c)] = jnp.where(first_head, o[:blk], o[blk:]).astype(o_ref.dtype)


def _moba_attention(qk, v, batch, seq):
    assert seq % (2 * MOBA_BLOCK) == 0
    nblk = seq // MOBA_BLOCK
    assert nblk <= LANES
    nrow = -(-nblk // BF16_SUBLANES) * BF16_SUBLANES
    slabs = N_HEADS * HEAD_DIM // LANES
    blk = MOBA_BLOCK
    block_id = jnp.arange(seq, dtype=jnp.int32) // blk
    onehot = (block_id[:, None] == jnp.arange(LANES, dtype=jnp.int32)[None, :]).astype(BF16)
    qi = jnp.arange(2 * blk, dtype=jnp.int32)[:, None] % blk
    kj = jnp.arange(2 * blk, dtype=jnp.int32)[None, :]
    own_mask = jnp.stack([jnp.where(kj - qi <= parity * blk, 0.0, MASK_VALUE) for parity in (0, 1)]).astype(F32)
    chains = MOBA_CHAINS
    assert slabs % chains == 0
    width = chains * LANES
    once = dict(pipeline_mode=pl.Buffered(1))
    return pl.pallas_call(
        functools.partial(_moba_kernel, nblk=nblk, chains=chains),
        grid=(batch, slabs // chains, nblk // 2),
        in_specs=[
            pl.BlockSpec((None, 2 * blk, width), lambda b, s, i: (b, i, s)),
            pl.BlockSpec((None, seq, width), lambda b, s, i: (b, 0, slabs // chains + s), **once),
            pl.BlockSpec((None, seq, width), lambda b, s, i: (b, 0, s), **once),
            pl.BlockSpec((seq, LANES), lambda b, s, i: (0, 0), **once),
            pl.BlockSpec((2, 2 * blk, 2 * blk), lambda b, s, i: (0, 0, 0), **once),
        ],
        out_specs=pl.BlockSpec((None, 2 * blk, width), lambda b, s, i: (b, i, s)),
        out_shape=jax.ShapeDtypeStruct((batch, seq, N_HEADS * HEAD_DIM), BF16),
        scratch_shapes=[
            pltpu.VMEM((chains, nrow, LANES), F32), pltpu.VMEM((chains, 2 * nrow, LANES), BF16),
            pltpu.VMEM((chains, HEADS_PER_SLAB, seq, LANES), BF16),
            pltpu.VMEM((2, chains, 2, 2 * blk, 2 * blk), F32),
            pltpu.VMEM((2, chains, 2 * blk, LANES), F32), pltpu.VMEM((2, chains, 2 * blk, LANES), F32),
        ],
        compiler_params=_params(("parallel", "parallel", "arbitrary")),
        name="moba_attention",
    )(qk, qk, v, onehot, own_mask)


def _ffn_kernel(x_ref, xh_ref, a_ref, ah_ref, wo_ref, g_ref, wu_ref, cw_ref, cb_ref, wd_ref, o_ref,
                *, tiles_per_seq, tf):
    i = pl.program_id(0)
    halo = CONV_HALO
    d_ff = wd_ref.shape[0]

    def rms(x):
        ms = jnp.mean(x * x, axis=-1, keepdims=True)
        return x * lax.rsqrt(ms + RMS_EPS) * g_ref[...]

    wo = wo_ref[...]
    x1 = x_ref[...] + jnp.dot(a_ref[...], wo, preferred_element_type=F32)
    xh1 = xh_ref[...] + jnp.dot(ah_ref[...], wo, preferred_element_type=F32)[ah_ref.shape[0] - halo:]

    keep = (i % tiles_per_seq != 0).astype(F32)
    hn = jnp.concatenate([(rms(xh1) * keep).astype(BF16), rms(x1).astype(BF16)], axis=0)

    def conv(col):
        u = jnp.dot(hn, wu_ref[:, col:col + tf], preferred_element_type=F32)
        cw = cw_ref[:, col:col + tf]
        return (cb_ref[:, col:col + tf]
                + cw[0:1, :] * pltpu.roll(u, 2, 0)[halo:]
                + cw[1:2, :] * pltpu.roll(u, 1, 0)[halo:]
                + cw[2:3, :] * u[halo:])

    acts = []
    for col in range(0, d_ff, tf):
        gate = conv(col)
        val = conv(d_ff + col)
        acts.append((gate * jax.nn.sigmoid(gate) * val).astype(BF16))
    o_ref[...] = x1 + jnp.dot(jnp.concatenate(acts, axis=1), wd_ref[...], preferred_element_type=F32)


def _attn_out_conv_ffn(x2, attn2, w_o, gain, w_up, conv_w, conv_b, w_down, seq, tm, tf):
    m, d = x2.shape
    da = attn2.shape[1]
    d_ff = w_down.shape[0]
    assert d_ff % tf == 0
    halo_blocks = tm // CONV_HALO
    attn_halo = BF16_SUBLANES
    attn_halo_blocks = tm // attn_halo
    resident = dict(pipeline_mode=pl.Buffered(1))
    return pl.pallas_call(
        functools.partial(_ffn_kernel, tiles_per_seq=seq // tm, tf=tf),
        grid=(m // tm,),
        in_specs=[
            pl.BlockSpec((tm, d), lambda i: (i, 0)),
            pl.BlockSpec((CONV_HALO, d), lambda i: (jnp.maximum(i * halo_blocks - 1, 0), 0)),
            pl.BlockSpec((tm, da), lambda i: (i, 0)),
            pl.BlockSpec((attn_halo, da), lambda i: (jnp.maximum(i * attn_halo_blocks - 1, 0), 0)),
            pl.BlockSpec((da, d), lambda i: (0, 0), **resident),
            pl.BlockSpec((1, d), lambda i: (0, 0)),
            pl.BlockSpec((d, 2 * d_ff), lambda i: (0, 0), **resident),
            pl.BlockSpec((CONV_WIDTH, 2 * d_ff), lambda i: (0, 0)),
            pl.BlockSpec((1, 2 * d_ff), lambda i: (0, 0)),
            pl.BlockSpec((d_ff, d), lambda i: (0, 0), **resident),
        ],
        out_specs=pl.BlockSpec((tm, d), lambda i: (i, 0)),
        out_shape=jax.ShapeDtypeStruct((m, d), F32),
        compiler_params=_params(("parallel",)),
        name="attn_out_conv_ffn",
    )(x2, x2, attn2, attn2, w_o, gain, w_up, conv_w, conv_b, w_down)


def _rope_tables(seq):
    pos = jnp.arange(seq, dtype=F32)
    inv_freq = ROPE_THETA ** (-jnp.arange(0, ROPE_DIM, 2, dtype=F32) / ROPE_DIM)
    ang = pos[:, None] * inv_freq[None, :]
    cos, sin = jnp.cos(ang), jnp.sin(ang)
    half = ROPE_DIM // 2
    pad = jnp.zeros((seq, HEAD_DIM - ROPE_DIM), F32)
    zeros = jnp.zeros((seq, half), F32)
    cos_h = jnp.concatenate([cos, cos, pad + 1.0], axis=1)
    sa_h = jnp.concatenate([zeros, sin, pad], axis=1)
    sb_h = jnp.concatenate([-sin, zeros, pad], axis=1)
    rep = LANES // HEAD_DIM
    return tuple(jnp.tile(t, (1, rep)) for t in (cos_h, sa_h, sb_h))


def _block_diag_ones():
    r = jnp.arange(MXU_WIDTH) // HEAD_DIM
    return (r[:, None] == r[None, :]).astype(BF16)


def kernel(x, attn_norm, a_w_qkv, a_q_norm, a_k_norm, a_w_o, b_w_qkv, b_q_norm, b_k_norm, b_w_o,
           ffn_norm, ffn_w_up, ffn_conv_w, ffn_conv_b, ffn_w_down):
    batch, seq, d_model = x.shape
    depth = attn_norm.shape[0]
    hd_all = N_HEADS * HEAD_DIM
    rope = _rope_tables(seq)
    bd = _block_diag_ones()
    x2 = x.reshape(batch * seq, d_model)
    q_scale = ATTN_SCALE * LOG2_E

    for layer in range(depth):
        j = layer // 2
        gain = attn_norm[layer][None, :]
        dilated = layer % 2 == 0
        if dilated:
            n_groups = N_DIL
            qg = jnp.tile(a_q_norm[j][:, None, :], (1, N_HEADS, 1)).reshape(-1) * q_scale
            kg = jnp.tile(a_k_norm[j][:, None, :], (1, N_HEADS, 1)).reshape(-1)
            w_qkv, w_o = a_w_qkv[j], a_w_o[j]
        else:
            n_groups = 1
            qg = jnp.tile(b_q_norm[j], N_HEADS) * q_scale
            kg = jnp.tile(b_k_norm[j], N_HEADS)
            w_qkv, w_o = b_w_qkv[j], b_w_o[j]
        n_qk = 2 * n_groups * hd_all
        colgain = jnp.concatenate([qg, kg])[None, :]
        w_qkv = w_qkv.astype(BF16)
        act_dtype = F32 if dilated else BF16
        if dilated:
            qk = _norm_qk_proj(x2, gain, w_qkv, colgain, rope, bd, seq, act_dtype, PROJ_ROWS, PROJ_COLS, False)
            v = _norm_v_proj(x2, gain, w_qkv, n_qk // 2, act_dtype, PROJ_ROWS, PROJ_COLS)
        else:
            qk, v = _norm_qk_proj(x2, gain, w_qkv, colgain, rope, bd, seq, act_dtype, FUSED_PROJ_ROWS, PROJ_COLS,
                                  True)
        qk, v = qk.reshape(batch, seq, -1), v.reshape(batch, seq, -1)
        attn = _dilated_attention(qk, v, batch, seq) if dilated else _moba_attention(qk, v, batch, seq)
        x2 = _attn_out_conv_ffn(x2, attn.reshape(batch * seq, hd_all), w_o.astype(BF16),
                                ffn_norm[layer][None, :], ffn_w_up[layer].astype(BF16),
                                ffn_conv_w[layer], ffn_conv_b[layer][None, :],
                                ffn_w_down[layer].astype(BF16), seq, FFN_ROWS, FFN_CHUNK)
    return x2.reshape(batch, seq, d_model)
```

```python
import functools
import math

import jax
import jax.numpy as jnp
from jax import lax
from jax.experimental import pallas as pl
from jax.experimental.pallas import tpu as pltpu

N_HEADS = 16
HEAD_DIM = 64
ROPE_DIM = HEAD_DIM // 4
ROPE_THETA = 500000.0
ATTN_SCALE = HEAD_DIM ** -0.5
DILATED_PAIRS = ((128, 1), (512, 4), (2048, 16))
N_DIL = len(DILATED_PAIRS)
BAND_BLOCK = 128
MOBA_BLOCK = 256
MOBA_TOPK = 3
CONV_WIDTH = 3
RMS_EPS = 1e-6

LANES = 128
BF16_SUBLANES = 16
MXU_WIDTH = 256
HEADS_PER_SLAB = LANES // HEAD_DIM
MASK_VALUE = -1e30
MOBA_CHAINS = 4
PROJ_ROWS = 512
FUSED_PROJ_ROWS = 1024
PROJ_COLS = 512
FFN_ROWS = 1024
FFN_CHUNK = 256
CONV_HALO = 8
VMEM_LIMIT = 56 * 1024 * 1024
LOG2_E = math.log2(math.e)

F32 = jnp.float32
BF16 = jnp.bfloat16
NT_DIMS = (((1,), (1,)), ((), ()))


def _params(semantics):
    return pltpu.CompilerParams(dimension_semantics=semantics, vmem_limit_bytes=VMEM_LIMIT)


def _rmsnorm_bf16(x_ref, g_ref):
    x = x_ref[...]
    ms = jnp.mean(x * x, axis=-1, keepdims=True)
    return (x * lax.rsqrt(ms + RMS_EPS) * g_ref[...]).astype(BF16)


def _qk_proj_kernel(x_ref, g_ref, w_ref, cg_ref, cos_ref, sa_ref, sb_ref, bd_ref, o_ref, *maybe_v_ref, tn):
    hn = _rmsnorm_bf16(x_ref, g_ref)
    bd = bd_ref[...]
    half = ROPE_DIM // 2
    n_qk = o_ref.shape[1]
    for v_ref in maybe_v_ref:
        for lo in range(n_qk, w_ref.shape[1], tn):
            v_ref[:, lo - n_qk:lo - n_qk + tn] = jnp.dot(hn, w_ref[:, lo:lo + tn],
                                                         preferred_element_type=F32).astype(v_ref.dtype)
    for lo in range(0, n_qk, tn):
        acc = jnp.dot(hn, w_ref[:, lo:lo + tn], preferred_element_type=F32)
        for c in range(tn // MXU_WIDTH):
            wide = acc[:, c * MXU_WIDTH:(c + 1) * MXU_WIDTH]
            ss_wide = jnp.dot((wide * wide).astype(BF16), bd, preferred_element_type=F32)
            for h in range(MXU_WIDTH // LANES):
                sl = slice(lo + c * MXU_WIDTH + h * LANES, lo + c * MXU_WIDTH + (h + 1) * LANES)
                a = wide[:, h * LANES:(h + 1) * LANES]
                ss = ss_wide[:, h * LANES:(h + 1) * LANES]
                y = a * lax.rsqrt(ss * (1.0 / HEAD_DIM) + RMS_EPS) * cg_ref[:, sl]
                y = (y * cos_ref[...]
                     + pltpu.roll(y, half, 1) * sa_ref[...]
                     + pltpu.roll(y, LANES - half, 1) * sb_ref[...])
                o_ref[:, sl] = y.astype(o_ref.dtype)


def _v_proj_kernel(x_ref, g_ref, w_ref, o_ref, *, tn):
    hn = _rmsnorm_bf16(x_ref, g_ref)
    for lo in range(0, w_ref.shape[1], tn):
        o_ref[:, lo:lo + tn] = jnp.dot(hn, w_ref[:, lo:lo + tn], preferred_element_type=F32).astype(o_ref.dtype)


def _norm_qk_proj(x2, gain, w, colgain, rope, bd, seq, out_dtype, tm, tn, with_v):
    m, d = x2.shape
    n = colgain.shape[1]
    n_w = w.shape[1] if with_v else n
    cos_t, sa_t, sb_t = rope
    tiles_per_seq = seq // tm
    rope_spec = pl.BlockSpec((tm, LANES), lambda i: (i % tiles_per_seq, 0))
    out_cols = [n, n_w - n] if with_v else [n]
    outs = pl.pallas_call(
        functools.partial(_qk_proj_kernel, tn=tn),
        grid=(m // tm,),
        in_specs=[
            pl.BlockSpec((tm, d), lambda i: (i, 0)),
            pl.BlockSpec((1, d), lambda i: (0, 0)),
            pl.BlockSpec((d, n_w), lambda i: (0, 0), pipeline_mode=pl.Buffered(1)),
            pl.BlockSpec((1, n), lambda i: (0, 0)),
            rope_spec, rope_spec, rope_spec,
            pl.BlockSpec((MXU_WIDTH, MXU_WIDTH), lambda i: (0, 0)),
        ],
        out_specs=[pl.BlockSpec((tm, c), lambda i: (i, 0)) for c in out_cols],
        out_shape=[jax.ShapeDtypeStruct((m, c), out_dtype) for c in out_cols],
        compiler_params=_params(("parallel",)),
        name="norm_qk_proj",
    )(x2, gain, w, colgain, cos_t, sa_t, sb_t, bd)
    return outs if with_v else outs[0]


def _norm_v_proj(x2, gain, w, n, out_dtype, tm, tn):
    m, d = x2.shape
    last = w.shape[1] // n - 1
    return pl.pallas_call(
        functools.partial(_v_proj_kernel, tn=tn),
        grid=(m // tm,),
        in_specs=[
            pl.BlockSpec((tm, d), lambda i: (i, 0)),
            pl.BlockSpec((1, d), lambda i: (0, 0)),
            pl.BlockSpec((d, n), lambda i: (0, last), pipeline_mode=pl.Buffered(1)),
        ],
        out_specs=pl.BlockSpec((tm, n), lambda i: (i, 0)),
        out_shape=jax.ShapeDtypeStruct((m, n), out_dtype),
        compiler_params=_params(("parallel",)),
        name="norm_v_proj",
    )(x2, gain, w)


def _dilated_kernel(q0, q1, q2, kl0, kl1, kl2, kh0, kh1, kh2, vl0, vl1, vl2, vh0, vh1, vh2, mask_ref,
                    o_ref, num_ref, m_ref, den_ref, *, tq):
    t = pl.program_id(2)
    q_refs = (q0, q1, q2)
    k_halves = ((kl0, kl1, kl2), (kh0, kh1, kh2))
    v_halves = ((vl0, vl1, vl2), (vh0, vh1, vh2))
    blk = BAND_BLOCK

    lane = lax.broadcasted_iota(jnp.int32, (blk, LANES), 1)
    first_head = lane < HEAD_DIM
    ones_rhs = jnp.ones((2 * blk, LANES), BF16)
    reaches = sorted({w // d for w, d in DILATED_PAIRS})

    def band_block(g, half, r, u, prev):
        window, dil = DILATED_PAIRS[g]
        span = blk * dil
        mask_g = reaches.index(window // dil)
        assert dil & (dil - 1) == 0 and window // dil <= blk

        def load_kv(which, base):
            rows = pl.ds(base, blk, stride=dil)
            return k_halves[which][g][rows, :].astype(BF16), v_halves[which][g][rows, :].astype(BF16)

        qbase = u * span + r
        kc, vc = load_kv(half, qbase)
        if prev is not None:
            no_prev = 0
            kp, vp = prev
        elif half == 0:
            no_prev = 1
            kp, vp = kc, vc
        else:
            no_prev = 0
            kp, vp = load_kv(0, tq - span + r)
        q = q_refs[g][pl.ds(qbase, blk, stride=dil), :]
        zero = jnp.zeros_like(q)
        q2 = jnp.concatenate([jnp.where(first_head, q, zero),
                              jnp.where(first_head, zero, q)], axis=0).astype(BF16)
        kcat = jnp.concatenate([kp, kc], axis=0)
        vcat = jnp.concatenate([jnp.concatenate([vp, vc], axis=0), ones_rhs], axis=1)
        s = lax.dot_general(q2, kcat, NT_DIMS, preferred_element_type=F32)
        s = s + mask_ref[mask_g, no_prev]
        m = jnp.max(s, axis=-1, keepdims=True)
        p = jnp.exp2(s - m).astype(BF16)
        ov = jnp.dot(p, vcat, preferred_element_type=F32)
        mb = jnp.broadcast_to(m, (2 * blk, LANES))
        rows = pl.ds(qbase, blk, stride=dil)
        num_ref[g, rows, :] = jnp.where(first_head, ov[:blk, :LANES], ov[blk:, :LANES])
        den_ref[g, rows, :] = jnp.where(first_head, ov[:blk, LANES:], ov[blk:, LANES:])
        m_ref[g, rows, :] = jnp.where(first_head, mb[:blk], mb[blk:])
        return kc, vc

    def tile(half):
        for g, (_, dil) in enumerate(DILATED_PAIRS):
            for r in range(dil):
                prev = None
                for u in range(tq // (blk * dil)):
                    prev = band_block(g, half, r, u, prev)

    for half in range(2):
        pl.when(t == half)(functools.partial(tile, half))

    rows = 2 * blk

    def merge(c, carry):
        sl = pl.ds(pl.multiple_of(c * rows, rows), rows)
        ms = [m_ref[g, sl, :] for g in range(N_DIL)]
        mx = functools.reduce(jnp.maximum, ms)
        ws = [jnp.exp2(mg - mx) for mg in ms]
        num = sum(w * num_ref[g, sl, :] for g, w in enumerate(ws))
        den = sum(w * den_ref[g, sl, :] for g, w in enumerate(ws))
        o_ref[sl, :] = (num / den).astype(o_ref.dtype)
        return carry

    lax.fori_loop(0, tq // rows, merge, 0)


def _dilated_attention(qk, v, batch, seq):
    span_max = BAND_BLOCK * max(d for _, d in DILATED_PAIRS)
    tq = span_max
    assert seq == 2 * tq
    slabs = N_HEADS // HEADS_PER_SLAB
    blk = BAND_BLOCK
    qi = jnp.arange(2 * blk, dtype=jnp.int32)[:, None] % blk
    kj = jnp.arange(2 * blk, dtype=jnp.int32)[None, :]
    dist = qi + blk - kj
    bands = [(dist >= 0) & (dist <= reach) for reach in sorted({w // d for w, d in DILATED_PAIRS})]
    masks = jnp.stack([jnp.stack([jnp.where(band, 0.0, MASK_VALUE), jnp.where(band & (kj >= blk), 0.0, MASK_VALUE)])
                       for band in bands]).astype(F32)

    def low(col):
        return lambda b, s, t: (b, 0, col + s)

    def high(col):
        def index(b, s, t):
            s_prev = jnp.where(s > 0, s - 1, slabs - 1)
            b_prev = jnp.where(s > 0, b, jnp.maximum(b - 1, 0))
            return (jnp.where(t == 1, b, b_prev), 1, col + jnp.where(t == 1, s, s_prev))
        return index

    half_block = (None, tq, LANES)
    q_specs = [pl.BlockSpec(half_block, lambda b, s, t, g=g: (b, t, g * slabs + s)) for g in range(N_DIL)]
    kv_specs = [pl.BlockSpec(half_block, part((first + g) * slabs))
                for first in (N_DIL, 0) for part in (low, high) for g in range(N_DIL)]
    stat = pltpu.VMEM((N_DIL, tq, LANES), F32)
    return pl.pallas_call(
        functools.partial(_dilated_kernel, tq=tq),
        grid=(batch, slabs, 2),
        in_specs=q_specs + kv_specs + [pl.BlockSpec(masks.shape, lambda b, s, t: (0, 0, 0, 0),
                                                    pipeline_mode=pl.Buffered(1))],
        out_specs=pl.BlockSpec((None, tq, LANES), lambda b, s, t: (b, t, s)),
        out_shape=jax.ShapeDtypeStruct((batch, seq, N_HEADS * HEAD_DIM), BF16),
        scratch_shapes=[stat, stat, stat],
        compiler_params=_params(("parallel", "parallel", "arbitrary")),
        name="dilated_attention",
    )(*([qk] * 9 + [v] * 6 + [masks]))


def _moba_kernel(q_ref, k_ref, v_ref, oh_ref, mask_ref, o_ref, km_ref, kmhl_ref, vaug_ref, s_ref, m_ref, acc_ref,
                 *, nblk, chains):
    own_pair = pl.program_id(2)
    blk = MOBA_BLOCK
    both = range(2)
    nrow = km_ref.shape[1]
    sup = 2 * blk
    lane = lax.broadcasted_iota(jnp.int32, (blk, LANES), 1)
    first_head = lane < HEAD_DIM
    cs = range(chains)

    def slab(c):
        return slice(c * LANES, (c + 1) * LANES)

    @pl.when(own_pair == 0)
    def _():
        for c in cs:
            km_ref[c] = jnp.zeros(km_ref.shape[1:], F32)
            for j in range(nblk):
                rows = slice(j * blk, (j + 1) * blk)
                km_ref[c, j:j + 1, :] = jnp.sum(k_ref[rows, slab(c)].astype(F32), axis=0,
                                                keepdims=True) * (1.0 / blk)
                vj = v_ref[rows, slab(c)].astype(F32)
                vaug_ref[c, 0, rows, :] = jnp.where(first_head, vj, 1.0).astype(BF16)
                vaug_ref[c, 1, rows, :] = jnp.where(first_head, 1.0, vj).astype(BF16)
            km = km_ref[c]
            hi = km.astype(BF16)
            kmhl_ref[c] = jnp.concatenate([hi, (km - hi.astype(F32)).astype(BF16)], axis=0)

    row = lax.broadcasted_iota(jnp.int32, (nrow, 2 * blk), 0)
    rowf = row.astype(F32)

    def augmented_queries(j, c):
        i = 2 * own_pair + j
        q = q_ref[j * blk:(j + 1) * blk, slab(c)]
        zero = jnp.zeros_like(q)
        q2 = jnp.concatenate([jnp.where(first_head, q, zero), jnp.where(first_head, zero, q)], axis=0)
        gate2 = lax.dot_general(kmhl_ref[c], q2, NT_DIMS, preferred_element_type=F32)
        gate = gate2[:nrow] + gate2[nrow:]
        remaining = row < i
        sel = row >= i
        for _ in range(MOBA_TOPK):
            gm = jnp.max(jnp.where(remaining, gate, -jnp.inf), axis=0, keepdims=True)
            cand = remaining & (gate == gm)
            first = jnp.min(jnp.where(cand, rowf, float(LANES)), axis=0, keepdims=True)
            pick = rowf == first
            sel = sel | pick
            remaining = remaining & jnp.logical_not(pick)
        bias_t = jnp.concatenate([jnp.where(sel, 0.0, MASK_VALUE),
                                  jnp.zeros((LANES - nrow, 2 * blk), F32)], axis=0)
        return jnp.concatenate([q2, bias_t.T.astype(BF16)], axis=1)

    q_aug = [jnp.concatenate([augmented_queries(j, c) for j in both], axis=0) for c in cs]

    own_base = own_pair * sup

    def scores(c, base):
        rows = pl.ds(pl.multiple_of(base, sup), sup)
        k_aug = jnp.concatenate([k_ref[rows, slab(c)], oh_ref[rows, :]], axis=1)
        return lax.dot_general(q_aug[c], k_aug, NT_DIMS, preferred_element_type=F32)

    def store_scores(c, buf, s, masked=False):
        for j in both:
            sj = s[j * sup:(j + 1) * sup]
            s_ref[j, c, buf] = sj + mask_ref[j] if masked else sj

    def accumulate(c, buf, tile):
        base = jnp.where(tile == 0, own_base, (tile - 1) * sup)
        rows = pl.ds(pl.multiple_of(base, sup), sup)
        probs, alphas = [], []
        for j in both:
            s = s_ref[j, c, buf]
            m_old = m_ref[j, c]
            m_new = jnp.maximum(m_old, jnp.max(s, axis=-1, keepdims=True))
            alphas.append(jnp.exp2(m_old - m_new))
            probs.append(jnp.exp2(s - jnp.concatenate([m_new] * (sup // LANES), axis=1)).astype(BF16))
            m_ref[j, c] = m_new
        pv = [jnp.dot(jnp.concatenate([p[h * blk:(h + 1) * blk] for p in probs], axis=0),
                      vaug_ref[c, h, rows, :], preferred_element_type=F32) for h in range(HEADS_PER_SLAB)]
        for j in both:
            pv_j = jnp.concatenate([pv_h[j * blk:(j + 1) * blk] for pv_h in pv], axis=0)
            acc_ref[j, c] = acc_ref[j, c] * alphas[j] + pv_j

    def accumulate_all(buf, tile):
        for c in cs:
            accumulate(c, buf, tile)

    for c in cs:
        store_scores(c, 0, scores(c, own_base), masked=True)
        for j in both:
            m_ref[j, c] = jnp.full(m_ref.shape[2:], MASK_VALUE, F32)
            acc_ref[j, c] = jnp.zeros(acc_ref.shape[2:], F32)

    def advance(dst, src, tile):
        for c in cs:
            store_scores(c, dst, scores(c, (tile - 1) * sup))
        accumulate_all(src, tile - 1)

    def body(n, carry):
        advance(1, 0, 2 * n + 1)
        advance(0, 1, 2 * n + 2)
        return carry

    lax.fori_loop(0, lax.shift_right_logical(own_pair, 1), body, 0)

    @pl.when(lax.bitwise_and(own_pair, 1) == 1)
    def _():
        advance(1, 0, own_pair)
        accumulate_all(1, own_pair)

    @pl.when(lax.bitwise_and(own_pair, 1) == 0)
    def _():
        accumulate_all(0, own_pair)

    for c in cs:
        for j in both:
            acc = acc_ref[j, c]
            o = acc / pltpu.roll(acc, HEAD_DIM, 1)
            o_ref[j * blk:(j + 1) * blk, slab(c)] = jnp.where(first_head, o[:blk], o[blk:]).astype(o_ref.dtype)


def _moba_attention(qk, v, batch, seq):
    assert seq % (2 * MOBA_BLOCK) == 0
    nblk = seq // MOBA_BLOCK
    assert nblk <= LANES
    nrow = -(-nblk // BF16_SUBLANES) * BF16_SUBLANES
    slabs = N_HEADS * HEAD_DIM // LANES
    blk = MOBA_BLOCK
    block_id = jnp.arange(seq, dtype=jnp.int32) // blk
    onehot = (block_id[:, None] == jnp.arange(LANES, dtype=jnp.int32)[None, :]).astype(BF16)
    qi = jnp.arange(2 * blk, dtype=jnp.int32)[:, None] % blk
    kj = jnp.arange(2 * blk, dtype=jnp.int32)[None, :]
    own_mask = jnp.stack([jnp.where(kj - qi <= parity * blk, 0.0, MASK_VALUE) for parity in (0, 1)]).astype(F32)
    chains = MOBA_CHAINS
    assert slabs % chains == 0
    width = chains * LANES
    once = dict(pipeline_mode=pl.Buffered(1))
    return pl.pallas_call(
        functools.partial(_moba_kernel, nblk=nblk, chains=chains),
        grid=(batch, slabs // chains, nblk // 2),
        in_specs=[
            pl.BlockSpec((None, 2 * blk, width), lambda b, s, i: (b, i, s)),
            pl.BlockSpec((None, seq, width), lambda b, s, i: (b, 0, slabs // chains + s), **once),
            pl.BlockSpec((None, seq, width), lambda b, s, i: (b, 0, s), **once),
            pl.BlockSpec((seq, LANES), lambda b, s, i: (0, 0), **once),
            pl.BlockSpec((2, 2 * blk, 2 * blk), lambda b, s, i: (0, 0, 0), **once),
        ],
        out_specs=pl.BlockSpec((None, 2 * blk, width), lambda b, s, i: (b, i, s)),
        out_shape=jax.ShapeDtypeStruct((batch, seq, N_HEADS * HEAD_DIM), BF16),
        scratch_shapes=[
            pltpu.VMEM((chains, nrow, LANES), F32), pltpu.VMEM((chains, 2 * nrow, LANES), BF16),
            pltpu.VMEM((chains, HEADS_PER_SLAB, seq, LANES), BF16),
            pltpu.VMEM((2, chains, 2, 2 * blk, 2 * blk), F32),
            pltpu.VMEM((2, chains, 2 * blk, LANES), F32), pltpu.VMEM((2, chains, 2 * blk, LANES), F32),
        ],
        compiler_params=_params(("parallel", "parallel", "arbitrary")),
        name="moba_attention",
    )(qk, qk, v, onehot, own_mask)


def _ffn_kernel(x_ref, xh_ref, a_ref, ah_ref, wo_ref, g_ref, wu_ref, cw_ref, cb_ref, wd_ref, o_ref,
                *, tiles_per_seq, tf):
    i = pl.program_id(0)
    halo = CONV_HALO
    d_ff = wd_ref.shape[0]

    def rms(x):
        ms = jnp.mean(x * x, axis=-1, keepdims=True)
        return x * lax.rsqrt(ms + RMS_EPS) * g_ref[...]

    wo = wo_ref[...]
    x1 = x_ref[...] + jnp.dot(a_ref[...], wo, preferred_element_type=F32)
    xh1 = xh_ref[...] + jnp.dot(ah_ref[...], wo, preferred_element_type=F32)[ah_ref.shape[0] - halo:]

    keep = (i % tiles_per_seq != 0).astype(F32)
    hn = jnp.concatenate([(rms(xh1) * keep).astype(BF16), rms(x1).astype(BF16)], axis=0)

    def conv(col):
        u = jnp.dot(hn, wu_ref[:, col:col + tf], preferred_element_type=F32)
        cw = cw_ref[:, col:col + tf]
        return (cb_ref[:, col:col + tf]
                + cw[0:1, :] * pltpu.roll(u, 2, 0)[halo:]
                + cw[1:2, :] * pltpu.roll(u, 1, 0)[halo:]
                + cw[2:3, :] * u[halo:])

    acts = []
    for col in range(0, d_ff, tf):
        gate = conv(col)
        val = conv(d_ff + col)
        acts.append((gate * jax.nn.sigmoid(gate) * val).astype(BF16))
    o_ref[...] = x1 + jnp.dot(jnp.concatenate(acts, axis=1), wd_ref[...], preferred_element_type=F32)


def _attn_out_conv_ffn(x2, attn2, w_o, gain, w_up, conv_w, conv_b, w_down, seq, tm, tf):
    m, d = x2.shape
    da = attn2.shape[1]
    d_ff = w_down.shape[0]
    assert d_ff % tf == 0
    halo_blocks = tm // CONV_HALO
    attn_halo = BF16_SUBLANES
    attn_halo_blocks = tm // attn_halo
    resident = dict(pipeline_mode=pl.Buffered(1))
    return pl.pallas_call(
        functools.partial(_ffn_kernel, tiles_per_seq=seq // tm, tf=tf),
        grid=(m // tm,),
        in_specs=[
            pl.BlockSpec((tm, d), lambda i: (i, 0)),
            pl.BlockSpec((CONV_HALO, d), lambda i: (jnp.maximum(i * halo_blocks - 1, 0), 0)),
            pl.BlockSpec((tm, da), lambda i: (i, 0)),
            pl.BlockSpec((attn_halo, da), lambda i: (jnp.maximum(i * attn_halo_blocks - 1, 0), 0)),
            pl.BlockSpec((da, d), lambda i: (0, 0), **resident),
            pl.BlockSpec((1, d), lambda i: (0, 0)),
            pl.BlockSpec((d, 2 * d_ff), lambda i: (0, 0), **resident),
            pl.BlockSpec((CONV_WIDTH, 2 * d_ff), lambda i: (0, 0)),
            pl.BlockSpec((1, 2 * d_ff), lambda i: (0, 0)),
            pl.BlockSpec((d_ff, d), lambda i: (0, 0), **resident),
        ],
        out_specs=pl.BlockSpec((tm, d), lambda i: (i, 0)),
        out_shape=jax.ShapeDtypeStruct((m, d), F32),
        compiler_params=_params(("parallel",)),
        name="attn_out_conv_ffn",
    )(x2, x2, attn2, attn2, w_o, gain, w_up, conv_w, conv_b, w_down)


def _rope_tables(seq):
    pos = jnp.arange(seq, dtype=F32)
    inv_freq = ROPE_THETA ** (-jnp.arange(0, ROPE_DIM, 2, dtype=F32) / ROPE_DIM)
    ang = pos[:, None] * inv_freq[None, :]
    cos, sin = jnp.cos(ang), jnp.sin(ang)
    half = ROPE_DIM // 2
    pad = jnp.zeros((seq, HEAD_DIM - ROPE_DIM), F32)
    zeros = jnp.zeros((seq, half), F32)
    cos_h = jnp.concatenate([cos, cos, pad + 1.0], axis=1)
    sa_h = jnp.concatenate([zeros, sin, pad], axis=1)
    sb_h = jnp.concatenate([-sin, zeros, pad], axis=1)
    rep = LANES // HEAD_DIM
    return tuple(jnp.tile(t, (1, rep)) for t in (cos_h, sa_h, sb_h))


def _block_diag_ones():
    r = jnp.arange(MXU_WIDTH) // HEAD_DIM
    return (r[:, None] == r[None, :]).astype(BF16)


def kernel(x, attn_norm, a_w_qkv, a_q_norm, a_k_norm, a_w_o, b_w_qkv, b_q_norm, b_k_norm, b_w_o,
           ffn_norm, ffn_w_up, ffn_conv_w, ffn_conv_b, ffn_w_down):
    batch, seq, d_model = x.shape
    depth = attn_norm.shape[0]
    hd_all = N_HEADS * HEAD_DIM
    rope = _rope_tables(seq)
    bd = _block_diag_ones()
    x2 = x.reshape(batch * seq, d_model)
    q_scale = ATTN_SCALE * LOG2_E

    for layer in range(depth):
        j = layer // 2
        gain = attn_norm[layer][None, :]
        dilated = layer % 2 == 0
        if dilated:
            n_groups = N_DIL
            qg = jnp.tile(a_q_norm[j][:, None, :], (1, N_HEADS, 1)).reshape(-1) * q_scale
            kg = jnp.tile(a_k_norm[j][:, None, :], (1, N_HEADS, 1)).reshape(-1)
            w_qkv, w_o = a_w_qkv[j], a_w_o[j]
        else:
            n_groups = 1
            qg = jnp.tile(b_q_norm[j], N_HEADS) * q_scale
            kg = jnp.tile(b_k_norm[j], N_HEADS)
            w_qkv, w_o = b_w_qkv[j], b_w_o[j]
        n_qk = 2 * n_groups * hd_all
        colgain = jnp.concatenate([qg, kg])[None, :]
        w_qkv = w_qkv.astype(BF16)
        act_dtype = F32 if dilated else BF16
        if dilated:
            qk = _norm_qk_proj(x2, gain, w_qkv, colgain, rope, bd, seq, act_dtype, PROJ_ROWS, PROJ_COLS, False)
            v = _norm_v_proj(x2, gain, w_qkv, n_qk // 2, act_dtype, PROJ_ROWS, PROJ_COLS)
        else:
            qk, v = _norm_qk_proj(x2, gain, w_qkv, colgain, rope, bd, seq, act_dtype, FUSED_PROJ_ROWS, PROJ_COLS,
                                  True)
        qk, v = qk.reshape(batch, seq, -1), v.reshape(batch, seq, -1)
        attn = _dilated_attention(qk, v, batch, seq) if dilated else _moba_attention(qk, v, batch, seq)
        x2 = _attn_out_conv_ffn(x2, attn.reshape(batch * seq, hd_all), w_o.astype(BF16),
                                ffn_norm[layer][None, :], ffn_w_up[layer].astype(BF16),
                                ffn_conv_w[layer], ffn_conv_b[layer][None, :],
                                ffn_w_down[layer].astype(BF16), seq, FFN_ROWS, FFN_CHUNK)
    return x2.reshape(batch, seq, d_model)
```

```python
import functools
import math

import jax
import jax.numpy as jnp
from jax import lax
from jax.experimental import pallas as pl
from jax.experimental.pallas import tpu as pltpu

N_HEADS = 16
HEAD_DIM = 64
ROPE_DIM = HEAD_DIM // 4
ROPE_THETA = 500000.0
ATTN_SCALE = HEAD_DIM ** -0.5
DILATED_PAIRS = ((128, 1), (512, 4), (2048, 16))
N_DIL = len(DILATED_PAIRS)
BAND_BLOCK = 128
MOBA_BLOCK = 256
MOBA_TOPK = 3
CONV_WIDTH = 3
RMS_EPS = 1e-6

LANES = 128
BF16_SUBLANES = 16
MXU_WIDTH = 256
HEADS_PER_SLAB = LANES // HEAD_DIM
MASK_VALUE = -1e30
MOBA_CHAINS = 4
PROJ_ROWS = 512
FUSED_PROJ_ROWS = 1024
PROJ_COLS = 512
FFN_ROWS = 1024
FFN_CHUNK = 256
CONV_HALO = 8
VMEM_LIMIT = 56 * 1024 * 1024
LOG2_E = math.log2(math.e)

F32 = jnp.float32
BF16 = jnp.bfloat16
NT_DIMS = (((1,), (1,)), ((), ()))


def _params(semantics):
    return pltpu.CompilerParams(dimension_semantics=semantics, vmem_limit_bytes=VMEM_LIMIT)


def _rmsnorm_bf16(x_ref, g_ref):
    x = x_ref[...]
    ms = jnp.mean(x * x, axis=-1, keepdims=True)
    return (x * lax.rsqrt(ms + RMS_EPS) * g_ref[...]).astype(BF16)


def _qk_proj_kernel(x_ref, g_ref, w_ref, cg_ref, cos_ref, sa_ref, sb_ref, bd_ref, o_ref, *maybe_v_ref, tn):
    hn = _rmsnorm_bf16(x_ref, g_ref)
    bd = bd_ref[...]
    half = ROPE_DIM // 2
    n_qk = o_ref.shape[1]
    for v_ref in maybe_v_ref:
        for lo in range(n_qk, w_ref.shape[1], tn):
            v_ref[:, lo - n_qk:lo - n_qk + tn] = jnp.dot(hn, w_ref[:, lo:lo + tn],
                                                         preferred_element_type=F32).astype(v_ref.dtype)
    for lo in range(0, n_qk, tn):
        acc = jnp.dot(hn, w_ref[:, lo:lo + tn], preferred_element_type=F32)
        for c in range(tn // MXU_WIDTH):
            wide = acc[:, c * MXU_WIDTH:(c + 1) * MXU_WIDTH]
            ss_wide = jnp.dot((wide * wide).astype(BF16), bd, preferred_element_type=F32)
            for h in range(MXU_WIDTH // LANES):
                sl = slice(lo + c * MXU_WIDTH + h * LANES, lo + c * MXU_WIDTH + (h + 1) * LANES)
                a = wide[:, h * LANES:(h + 1) * LANES]
                ss = ss_wide[:, h * LANES:(h + 1) * LANES]
                y = a * lax.rsqrt(ss * (1.0 / HEAD_DIM) + RMS_EPS) * cg_ref[:, sl]
                y = (y * cos_ref[...]
                     + pltpu.roll(y, half, 1) * sa_ref[...]
                     + pltpu.roll(y, LANES - half, 1) * sb_ref[...])
                o_ref[:, sl] = y.astype(o_ref.dtype)


def _v_proj_kernel(x_ref, g_ref, w_ref, o_ref, *, tn):
    hn = _rmsnorm_bf16(x_ref, g_ref)
    for lo in range(0, w_ref.shape[1], tn):
        o_ref[:, lo:lo + tn] = jnp.dot(hn, w_ref[:, lo:lo + tn], preferred_element_type=F32).astype(o_ref.dtype)


def _norm_qk_proj(x2, gain, w, colgain, rope, bd, seq, out_dtype, tm, tn, with_v):
    m, d = x2.shape
    n = colgain.shape[1]
    n_w = w.shape[1] if with_v else n
    cos_t, sa_t, sb_t = rope
    tiles_per_seq = seq // tm
    rope_spec = pl.BlockSpec((tm, LANES), lambda i: (i % tiles_per_seq, 0))
    out_cols = [n, n_w - n] if with_v else [n]
    outs = pl.pallas_call(
        functools.partial(_qk_proj_kernel, tn=tn),
        grid=(m // tm,),
        in_specs=[
            pl.BlockSpec((tm, d), lambda i: (i, 0)),
            pl.BlockSpec((1, d), lambda i: (0, 0)),
            pl.BlockSpec((d, n_w), lambda i: (0, 0), pipeline_mode=pl.Buffered(1)),
            pl.BlockSpec((1, n), lambda i: (0, 0)),
            rope_spec, rope_spec, rope_spec,
            pl.BlockSpec((MXU_WIDTH, MXU_WIDTH), lambda i: (0, 0)),
        ],
        out_specs=[pl.BlockSpec((tm, c), lambda i: (i, 0)) for c in out_cols],
        out_shape=[jax.ShapeDtypeStruct((m, c), out_dtype) for c in out_cols],
        compiler_params=_params(("parallel",)),
        name="norm_qk_proj",
    )(x2, gain, w, colgain, cos_t, sa_t, sb_t, bd)
    return outs if with_v else outs[0]


def _norm_v_proj(x2, gain, w, n, out_dtype, tm, tn):
    m, d = x2.shape
    last = w.shape[1] // n - 1
    return pl.pallas_call(
        functools.partial(_v_proj_kernel, tn=tn),
        grid=(m // tm,),
        in_specs=[
            pl.BlockSpec((tm, d), lambda i: (i, 0)),
            pl.BlockSpec((1, d), lambda i: (0, 0)),
            pl.BlockSpec((d, n), lambda i: (0, last), pipeline_mode=pl.Buffered(1)),
        ],
        out_specs=pl.BlockSpec((tm, n), lambda i: (i, 0)),
        out_shape=jax.ShapeDtypeStruct((m, n), out_dtype),
        compiler_params=_params(("parallel",)),
        name="norm_v_proj",
    )(x2, gain, w)


def _dilated_kernel(q0, q1, q2, kl0, kl1, kl2, kh0, kh1, kh2, vl0, vl1, vl2, vh0, vh1, vh2, mask_ref,
                    o_ref, num_ref, m_ref, den_ref, *, tq):
    t = pl.program_id(2)
    q_refs = (q0, q1, q2)
    k_halves = ((kl0, kl1, kl2), (kh0, kh1, kh2))
    v_halves = ((vl0, vl1, vl2), (vh0, vh1, vh2))
    blk = BAND_BLOCK

    lane = lax.broadcasted_iota(jnp.int32, (blk, LANES), 1)
    first_head = lane < HEAD_DIM
    ones_rhs = jnp.ones((2 * blk, LANES), BF16)
    reaches = sorted({w // d for w, d in DILATED_PAIRS})

    def band_block(g, half, r, u, prev):
        window, dil = DILATED_PAIRS[g]
        span = blk * dil
        mask_g = reaches.index(window // dil)
        assert dil & (dil - 1) == 0 and window // dil <= blk

        def load_kv(which, base):
            rows = pl.ds(base, blk, stride=dil)
            return k_halves[which][g][rows, :].astype(BF16), v_halves[which][g][rows, :].astype(BF16)

        qbase = u * span + r
        kc, vc = load_kv(half, qbase)
        if prev is not None:
            no_prev = 0
            kp, vp = prev
        elif half == 0:
            no_prev = 1
            kp, vp = kc, vc
        else:
            no_prev = 0
            kp, vp = load_kv(0, tq - span + r)
        q = q_refs[g][pl.ds(qbase, blk, stride=dil), :]
        zero = jnp.zeros_like(q)
        q2 = jnp.concatenate([jnp.where(first_head, q, zero),
                              jnp.where(first_head, zero, q)], axis=0).astype(BF16)
        kcat = jnp.concatenate([kp, kc], axis=0)
        vcat = jnp.concatenate([jnp.concatenate([vp, vc], axis=0), ones_rhs], axis=1)
        s = lax.dot_general(q2, kcat, NT_DIMS, preferred_element_type=F32)
        s = s + mask_ref[mask_g, no_prev]
        m = jnp.max(s, axis=-1, keepdims=True)
        p = jnp.exp2(s - m).astype(BF16)
        ov = jnp.dot(p, vcat, preferred_element_type=F32)
        mb = jnp.broadcast_to(m, (2 * blk, LANES))
        rows = pl.ds(qbase, blk, stride=dil)
        num_ref[g, rows, :] = jnp.where(first_head, ov[:blk, :LANES], ov[blk:, :LANES])
        den_ref[g, rows, :] = jnp.where(first_head, ov[:blk, LANES:], ov[blk:, LANES:])
        m_ref[g, rows, :] = jnp.where(first_head, mb[:blk], mb[blk:])
        return kc, vc

    def tile(half):
        for g, (_, dil) in reversed(list(enumerate(DILATED_PAIRS))):
            for r in range(dil):
                prev = None
                for u in range(tq // (blk * dil)):
                    prev = band_block(g, half, r, u, prev)

    for half in range(2):
        pl.when(t == half)(functools.partial(tile, half))

    rows = 2 * blk

    def merge(c, carry):
        sl = pl.ds(pl.multiple_of(c * rows, rows), rows)
        ms = [m_ref[g, sl, :] for g in range(N_DIL)]
        mx = functools.reduce(jnp.maximum, ms)
        ws = [jnp.exp2(mg - mx) for mg in ms]
        num = sum(w * num_ref[g, sl, :] for g, w in enumerate(ws))
        den = sum(w * den_ref[g, sl, :] for g, w in enumerate(ws))
        o_ref[sl, :] = (num / den).astype(o_ref.dtype)
        return carry

    lax.fori_loop(0, tq // rows, merge, 0)


def _dilated_attention(qk, v, batch, seq):
    span_max = BAND_BLOCK * max(d for _, d in DILATED_PAIRS)
    tq = span_max
    assert seq == 2 * tq
    slabs = N_HEADS // HEADS_PER_SLAB
    blk = BAND_BLOCK
    qi = jnp.arange(2 * blk, dtype=jnp.int32)[:, None] % blk
    kj = jnp.arange(2 * blk, dtype=jnp.int32)[None, :]
    dist = qi + blk - kj
    bands = [(dist >= 0) & (dist <= reach) for reach in sorted({w // d for w, d in DILATED_PAIRS})]
    masks = jnp.stack([jnp.stack([jnp.where(band, 0.0, MASK_VALUE), jnp.where(band & (kj >= blk), 0.0, MASK_VALUE)])
                       for band in bands]).astype(F32)

    def low(col):
        return lambda b, s, t: (b, 0, col + s)

    def high(col):
        def index(b, s, t):
            s_prev = jnp.where(s > 0, s - 1, slabs - 1)
            b_prev = jnp.where(s > 0, b, jnp.maximum(b - 1, 0))
            return (jnp.where(t == 1, b, b_prev), 1, col + jnp.where(t == 1, s, s_prev))
        return index

    half_block = (None, tq, LANES)
    q_specs = [pl.BlockSpec(half_block, lambda b, s, t, g=g: (b, t, g * slabs + s)) for g in range(N_DIL)]
    kv_specs = [pl.BlockSpec(half_block, part((first + g) * slabs))
                for first in (N_DIL, 0) for part in (low, high) for g in range(N_DIL)]
    stat = pltpu.VMEM((N_DIL, tq, LANES), F32)
    return pl.pallas_call(
        functools.partial(_dilated_kernel, tq=tq),
        grid=(batch, slabs, 2),
        in_specs=q_specs + kv_specs + [pl.BlockSpec(masks.shape, lambda b, s, t: (0, 0, 0, 0),
                                                    pipeline_mode=pl.Buffered(1))],
        out_specs=pl.BlockSpec((None, tq, LANES), lambda b, s, t: (b, t, s)),
        out_shape=jax.ShapeDtypeStruct((batch, seq, N_HEADS * HEAD_DIM), BF16),
        scratch_shapes=[stat, stat, stat],
        compiler_params=_params(("parallel", "parallel", "arbitrary")),
        name="dilated_attention",
    )(*([qk] * 9 + [v] * 6 + [masks]))


def _moba_kernel(q_ref, k_ref, v_ref, oh_ref, mask_ref, o_ref, km_ref, kmhl_ref, vaug_ref, s_ref, m_ref, acc_ref,
                 *, nblk, chains):
    own_pair = pl.program_id(2)
    blk = MOBA_BLOCK
    both = range(2)
    nrow = km_ref.shape[1]
    sup = 2 * blk
    lane = lax.broadcasted_iota(jnp.int32, (blk, LANES), 1)
    first_head = lane < HEAD_DIM
    cs = range(chains)

    def slab(c):
        return slice(c * LANES, (c + 1) * LANES)

    @pl.when(own_pair == 0)
    def _():
        for c in cs:
            km_ref[c] = jnp.zeros(km_ref.shape[1:], F32)
            for j in range(nblk):
                rows = slice(j * blk, (j + 1) * blk)
                km_ref[c, j:j + 1, :] = jnp.sum(k_ref[rows, slab(c)].astype(F32), axis=0,
                                                keepdims=True) * (1.0 / blk)
                vj = v_ref[rows, slab(c)].astype(F32)
                vaug_ref[c, 0, rows, :] = jnp.where(first_head, vj, 1.0).astype(BF16)
                vaug_ref[c, 1, rows, :] = jnp.where(first_head, 1.0, vj).astype(BF16)
            km = km_ref[c]
            hi = km.astype(BF16)
            kmhl_ref[c] = jnp.concatenate([hi, (km - hi.astype(F32)).astype(BF16)], axis=0)

    row = lax.broadcasted_iota(jnp.int32, (nrow, 2 * blk), 0)
    rowf = row.astype(F32)

    def augmented_queries(j, c):
        i = 2 * own_pair + j
        q = q_ref[j * blk:(j + 1) * blk, slab(c)]
        zero = jnp.zeros_like(q)
        q2 = jnp.concatenate([jnp.where(first_head, q, zero), jnp.where(first_head, zero, q)], axis=0)
        gate2 = lax.dot_general(kmhl_ref[c], q2, NT_DIMS, preferred_element_type=F32)
        gate = gate2[:nrow] + gate2[nrow:]
        remaining = row < i
        sel = row >= i
        for _ in range(MOBA_TOPK):
            gm = jnp.max(jnp.where(remaining, gate, -jnp.inf), axis=0, keepdims=True)
            cand = remaining & (gate == gm)
            first = jnp.min(jnp.where(cand, rowf, float(LANES)), axis=0, keepdims=True)
            pick = rowf == first
            sel = sel | pick
            remaining = remaining & jnp.logical_not(pick)
        bias_t = jnp.concatenate([jnp.where(sel, 0.0, MASK_VALUE),
                                  jnp.zeros((LANES - nrow, 2 * blk), F32)], axis=0)
        return jnp.concatenate([q2, bias_t.T.astype(BF16)], axis=1)

    q_aug = [jnp.concatenate([augmented_queries(j, c) for j in both], axis=0) for c in cs]

    own_base = own_pair * sup

    def scores(c, base):
        rows = pl.ds(pl.multiple_of(base, sup), sup)
        k_aug = jnp.concatenate([k_ref[rows, slab(c)], oh_ref[rows, :]], axis=1)
        return lax.dot_general(q_aug[c], k_aug, NT_DIMS, preferred_element_type=F32)

    def store_scores(c, buf, s, masked=False):
        for j in both:
            sj = s[j * sup:(j + 1) * sup]
            s_ref[j, c, buf] = sj + mask_ref[j] if masked else sj

    def accumulate(c, buf, tile):
        base = jnp.where(tile == 0, own_base, (tile - 1) * sup)
        rows = pl.ds(pl.multiple_of(base, sup), sup)
        probs, alphas = [], []
        for j in both:
            s = s_ref[j, c, buf]
            m_old = m_ref[j, c]
            m_new = jnp.maximum(m_old, jnp.max(s, axis=-1, keepdims=True))
            alphas.append(jnp.exp2(m_old - m_new))
            probs.append(jnp.exp2(s - jnp.concatenate([m_new] * (sup // LANES), axis=1)).astype(BF16))
            m_ref[j, c] = m_new
        pv = [jnp.dot(jnp.concatenate([p[h * blk:(h + 1) * blk] for p in probs], axis=0),
                      vaug_ref[c, h, rows, :], preferred_element_type=F32) for h in range(HEADS_PER_SLAB)]
        for j in both:
            pv_j = jnp.concatenate([pv_h[j * blk:(j + 1) * blk] for pv_h in pv], axis=0)
            acc_ref[j, c] = acc_ref[j, c] * alphas[j] + pv_j

    def accumulate_all(buf, tile):
        for c in cs:
            accumulate(c, buf, tile)

    for c in cs:
        store_scores(c, 0, scores(c, own_base), masked=True)
        for j in both:
            m_ref[j, c] = jnp.full(m_ref.shape[2:], MASK_VALUE, F32)
            acc_ref[j, c] = jnp.zeros(acc_ref.shape[2:], F32)

    def advance(dst, src, tile):
        for c in cs:
            store_scores(c, dst, scores(c, (tile - 1) * sup))
        accumulate_all(src, tile - 1)

    def body(n, carry):
        advance(1, 0, 2 * n + 1)
        advance(0, 1, 2 * n + 2)
        return carry

    lax.fori_loop(0, lax.shift_right_logical(own_pair, 1), body, 0)

    @pl.when(lax.bitwise_and(own_pair, 1) == 1)
    def _():
        advance(1, 0, own_pair)
        accumulate_all(1, own_pair)

    @pl.when(lax.bitwise_and(own_pair, 1) == 0)
    def _():
        accumulate_all(0, own_pair)

    for c in cs:
        for j in both:
            acc = acc_ref[j, c]
            o = acc / pltpu.roll(acc, HEAD_DIM, 1)
            o_ref[j * blk:(j + 1) * blk, slab(c)] = jnp.where(first_head, o[:blk], o[blk:]).astype(o_ref.dtype)


def _moba_attention(qk, v, batch, seq):
    assert seq % (2 * MOBA_BLOCK) == 0
    nblk = seq // MOBA_BLOCK
    assert nblk <= LANES
    nrow = -(-nblk // BF16_SUBLANES) * BF16_SUBLANES
    slabs = N_HEADS * HEAD_DIM // LANES
    blk = MOBA_BLOCK
    block_id = jnp.arange(seq, dtype=jnp.int32) // blk
    onehot = (block_id[:, None] == jnp.arange(LANES, dtype=jnp.int32)[None, :]).astype(BF16)
    qi = jnp.arange(2 * blk, dtype=jnp.int32)[:, None] % blk
    kj = jnp.arange(2 * blk, dtype=jnp.int32)[None, :]
    own_mask = jnp.stack([jnp.where(kj - qi <= parity * blk, 0.0, MASK_VALUE) for parity in (0, 1)]).astype(F32)
    chains = MOBA_CHAINS
    assert slabs % chains == 0
    width = chains * LANES
    once = dict(pipeline_mode=pl.Buffered(1))
    return pl.pallas_call(
        functools.partial(_moba_kernel, nblk=nblk, chains=chains),
        grid=(batch, slabs // chains, nblk // 2),
        in_specs=[
            pl.BlockSpec((None, 2 * blk, width), lambda b, s, i: (b, i, s)),
            pl.BlockSpec((None, seq, width), lambda b, s, i: (b, 0, slabs // chains + s), **once),
            pl.BlockSpec((None, seq, width), lambda b, s, i: (b, 0, s), **once),
            pl.BlockSpec((seq, LANES), lambda b, s, i: (0, 0), **once),
            pl.BlockSpec((2, 2 * blk, 2 * blk), lambda b, s, i: (0, 0, 0), **once),
        ],
        out_specs=pl.BlockSpec((None, 2 * blk, width), lambda b, s, i: (b, i, s)),
        out_shape=jax.ShapeDtypeStruct((batch, seq, N_HEADS * HEAD_DIM), BF16),
        scratch_shapes=[
            pltpu.VMEM((chains, nrow, LANES), F32), pltpu.VMEM((chains, 2 * nrow, LANES), BF16),
            pltpu.VMEM((chains, HEADS_PER_SLAB, seq, LANES), BF16),
            pltpu.VMEM((2, chains, 2, 2 * blk, 2 * blk), F32),
            pltpu.VMEM((2, chains, 2 * blk, LANES), F32), pltpu.VMEM((2, chains, 2 * blk, LANES), F32),
        ],
        compiler_params=_params(("parallel", "parallel", "arbitrary")),
        name="moba_attention",
    )(qk, qk, v, onehot, own_mask)


def _ffn_kernel(x_ref, xh_ref, a_ref, ah_ref, wo_ref, g_ref, wu_ref, cw_ref, cb_ref, wd_ref, o_ref,
                *, tiles_per_seq, tf):
    i = pl.program_id(0)
    halo = CONV_HALO
    d_ff = wd_ref.shape[0]

    def rms(x):
        ms = jnp.mean(x * x, axis=-1, keepdims=True)
        return x * lax.rsqrt(ms + RMS_EPS) * g_ref[...]

    wo = wo_ref[...]
    x1 = x_ref[...] + jnp.dot(a_ref[...], wo, preferred_element_type=F32)
    xh1 = xh_ref[...] + jnp.dot(ah_ref[...], wo, preferred_element_type=F32)[ah_ref.shape[0] - halo:]

    keep = (i % tiles_per_seq != 0).astype(F32)
    hn = jnp.concatenate([(rms(xh1) * keep).astype(BF16), rms(x1).astype(BF16)], axis=0)

    def conv(col):
        u = jnp.dot(hn, wu_ref[:, col:col + tf], preferred_element_type=F32)
        cw = cw_ref[:, col:col + tf]
        return (cb_ref[:, col:col + tf]
                + cw[0:1, :] * pltpu.roll(u, 2, 0)[halo:]
                + cw[1:2, :] * pltpu.roll(u, 1, 0)[halo:]
                + cw[2:3, :] * u[halo:])

    acts = []
    for col in range(0, d_ff, tf):
        gate = conv(col)
        val = conv(d_ff + col)
        acts.append((gate * jax.nn.sigmoid(gate) * val).astype(BF16))
    o_ref[...] = x1 + jnp.dot(jnp.concatenate(acts, axis=1), wd_ref[...], preferred_element_type=F32)


def _attn_out_conv_ffn(x2, attn2, w_o, gain, w_up, conv_w, conv_b, w_down, seq, tm, tf):
    m, d = x2.shape
    da = attn2.shape[1]
    d_ff = w_down.shape[0]
    assert d_ff % tf == 0
    halo_blocks = tm // CONV_HALO
    attn_halo = BF16_SUBLANES
    attn_halo_blocks = tm // attn_halo
    resident = dict(pipeline_mode=pl.Buffered(1))
    return pl.pallas_call(
        functools.partial(_ffn_kernel, tiles_per_seq=seq // tm, tf=tf),
        grid=(m // tm,),
        in_specs=[
            pl.BlockSpec((tm, d), lambda i: (i, 0)),
            pl.BlockSpec((CONV_HALO, d), lambda i: (jnp.maximum(i * halo_blocks - 1, 0), 0)),
            pl.BlockSpec((tm, da), lambda i: (i, 0)),
            pl.BlockSpec((attn_halo, da), lambda i: (jnp.maximum(i * attn_halo_blocks - 1, 0), 0)),
            pl.BlockSpec((da, d), lambda i: (0, 0), **resident),
            pl.BlockSpec((1, d), lambda i: (0, 0)),
            pl.BlockSpec((d, 2 * d_ff), lambda i: (0, 0), **resident),
            pl.BlockSpec((CONV_WIDTH, 2 * d_ff), lambda i: (0, 0)),
            pl.BlockSpec((1, 2 * d_ff), lambda i: (0, 0)),
            pl.BlockSpec((d_ff, d), lambda i: (0, 0), **resident),
        ],
        out_specs=pl.BlockSpec((tm, d), lambda i: (i, 0)),
        out_shape=jax.ShapeDtypeStruct((m, d), F32),
        compiler_params=_params(("parallel",)),
        name="attn_out_conv_ffn",
    )(x2, x2, attn2, attn2, w_o, gain, w_up, conv_w, conv_b, w_down)


def _rope_tables(seq):
    pos = jnp.arange(seq, dtype=F32)
    inv_freq = ROPE_THETA ** (-jnp.arange(0, ROPE_DIM, 2, dtype=F32) / ROPE_DIM)
    ang = pos[:, None] * inv_freq[None, :]
    cos, sin = jnp.cos(ang), jnp.sin(ang)
    half = ROPE_DIM // 2
    pad = jnp.zeros((seq, HEAD_DIM - ROPE_DIM), F32)
    zeros = jnp.zeros((seq, half), F32)
    cos_h = jnp.concatenate([cos, cos, pad + 1.0], axis=1)
    sa_h = jnp.concatenate([zeros, sin, pad], axis=1)
    sb_h = jnp.concatenate([-sin, zeros, pad], axis=1)
    rep = LANES // HEAD_DIM
    return tuple(jnp.tile(t, (1, rep)) for t in (cos_h, sa_h, sb_h))


def _block_diag_ones():
    r = jnp.arange(MXU_WIDTH) // HEAD_DIM
    return (r[:, None] == r[None, :]).astype(BF16)


def kernel(x, attn_norm, a_w_qkv, a_q_norm, a_k_norm, a_w_o, b_w_qkv, b_q_norm, b_k_norm, b_w_o,
           ffn_norm, ffn_w_up, ffn_conv_w, ffn_conv_b, ffn_w_down):
    batch, seq, d_model = x.shape
    depth = attn_norm.shape[0]
    hd_all = N_HEADS * HEAD_DIM
    rope = _rope_tables(seq)
    bd = _block_diag_ones()
    x2 = x.reshape(batch * seq, d_model)
    q_scale = ATTN_SCALE * LOG2_E

    for layer in range(depth):
        j = layer // 2
        gain = attn_norm[layer][None, :]
        dilated = layer % 2 == 0
        if dilated:
            n_groups = N_DIL
            qg = jnp.tile(a_q_norm[j][:, None, :], (1, N_HEADS, 1)).reshape(-1) * q_scale
            kg = jnp.tile(a_k_norm[j][:, None, :], (1, N_HEADS, 1)).reshape(-1)
            w_qkv, w_o = a_w_qkv[j], a_w_o[j]
        else:
            n_groups = 1
            qg = jnp.tile(b_q_norm[j], N_HEADS) * q_scale
            kg = jnp.tile(b_k_norm[j], N_HEADS)
            w_qkv, w_o = b_w_qkv[j], b_w_o[j]
        n_qk = 2 * n_groups * hd_all
        colgain = jnp.concatenate([qg, kg])[None, :]
        w_qkv = w_qkv.astype(BF16)
        act_dtype = F32 if dilated else BF16
        if dilated:
            qk = _norm_qk_proj(x2, gain, w_qkv, colgain, rope, bd, seq, act_dtype, PROJ_ROWS, PROJ_COLS, False)
            v = _norm_v_proj(x2, gain, w_qkv, n_qk // 2, act_dtype, PROJ_ROWS, PROJ_COLS)
        else:
            qk, v = _norm_qk_proj(x2, gain, w_qkv, colgain, rope, bd, seq, act_dtype, FUSED_PROJ_ROWS, PROJ_COLS,
                                  True)
        qk, v = qk.reshape(batch, seq, -1), v.reshape(batch, seq, -1)
        attn = _dilated_attention(qk, v, batch, seq) if dilated else _moba_attention(qk, v, batch, seq)
        x2 = _attn_out_conv_ffn(x2, attn.reshape(batch * seq, hd_all), w_o.astype(BF16),
                                ffn_norm[layer][None, :], ffn_w_up[layer].astype(BF16),
                                ffn_conv_w[layer], ffn_conv_b[layer][None, :],
                                ffn_w_down[layer].astype(BF16), seq, FFN_ROWS, FFN_CHUNK)
    return x2.reshape(batch, seq, d_model)
```
